```python
import math
import jax
import jax.numpy as jnp
from jax import lax
import numpy as np

D_MODEL = 2048
BATCH = 4
SEQ = 4096
DEPTH = 2

CHUNK = 64
N_LEFT_CHUNKS = 8
BAND = (N_LEFT_CHUNKS + 1) * CHUNK
HEAD_DIM = 64
D_ATT = D_MODEL // 2
N_HEADS_ATT = D_ATT // HEAD_DIM
REL_CLIP = 128
N_REL = (CHUNK - 1) + REL_CLIP + 1
D_RWKV = D_MODEL // 2
N_HEADS_RWKV = D_RWKV // HEAD_DIM
DECAY_LORA = 64
AAA_LORA = 64
GATE_LORA = 128
N_B_IN = 3 * D_RWKV + DECAY_LORA + AAA_LORA + GATE_LORA
N_IN_AB = 3 * D_ATT + N_B_IN
D_SSM = D_MODEL // 2
SSM_GROUP = 16
N_SSM_GROUPS = D_SSM // SSM_GROUP
SSM_STATE = 64
D_FF = 5632
D_PLE = 256
RMS_EPS = 1e-6
GN_EPS = 64e-5

kernel_name = 'hybrid_chunk_causal_encoder'


def rms_norm(x, g):
    xf = x.astype(jnp.float32)
    y = xf * lax.rsqrt(jnp.mean(xf * xf, axis=-1, keepdims=True) + RMS_EPS)
    return (y * g.astype(jnp.float32)).astype(x.dtype)


def swiglu_ffn(x, w_gate, w_up, w_down):
    return (jax.nn.silu(x @ w_gate) * (x @ w_up)) @ w_down


def rel_bias_index():
    i = np.arange(CHUNK)[:, None]
    j = np.arange(BAND)[None, :]
    dist = i + N_LEFT_CHUNKS * CHUNK - j
    return np.clip(dist, -(CHUNK - 1), REL_CLIP) + (CHUNK - 1)


def chunked_band_attention(q, k, v, q_gain, k_gain, rel_bias):
    bsz, t, h, dh = q.shape
    n_chunks = t // CHUNK
    pad = N_LEFT_CHUNKS * CHUNK
    q = rms_norm(q, q_gain) * (dh ** -0.5)
    k = rms_norm(k, k_gain)
    k_pad = jnp.pad(k, ((0, 0), (pad, 0), (0, 0), (0, 0)))
    v_pad = jnp.pad(v, ((0, 0), (pad, 0), (0, 0), (0, 0)))
    bias = rel_bias.astype(jnp.float32)[:, rel_bias_index()]
    q_chunks = jnp.swapaxes(q.reshape(bsz, n_chunks, CHUNK, h, dh), 0, 1)
    band_pos = jnp.arange(BAND)

    def one_chunk(args):
        c, q_c = args
        start = c * CHUNK
        k_b = lax.dynamic_slice_in_dim(k_pad, start, BAND, axis=1)
        v_b = lax.dynamic_slice_in_dim(v_pad, start, BAND, axis=1)
        s = jnp.einsum('bqhd,bkhd->bhqk', q_c, k_b).astype(jnp.float32) + bias
        valid = (start + band_pos) >= pad
        s = jnp.where(valid, s, -jnp.inf)
        prob = jax.nn.softmax(s, axis=-1).astype(v.dtype)
        return jnp.einsum('bhqk,bkhd->bqhd', prob, v_b)

    out = lax.map(one_chunk, (jnp.arange(n_chunks), q_chunks))
    return jnp.swapaxes(out, 0, 1).reshape(bsz, t, h, dh)


def token_shift(z, mu):
    prev = jnp.pad(z[:, :-1], ((0, 0), (1, 0), (0, 0)))
    return z + (prev - z) * mu


def rwkv7_time_mix(z, mu, w0, w_up, a0, a_up, g_up, k_k, k_a, r_k, lnx_w, lnx_b):
    f32 = jnp.float32
    bsz, t, _ = z.shape
    z = token_shift(z.astype(f32), mu.astype(f32))
    o1, o2, o3 = D_RWKV, 2 * D_RWKV, 3 * D_RWKV
    o4 = o3 + DECAY_LORA
    o5 = o4 + AAA_LORA
    r, k, v = z[..., :o1], z[..., o1:o2], z[..., o2:o3]
    xw, xa, xg = z[..., o3:o4], z[..., o4:o5], z[..., o5:]
    w_log = -jax.nn.softplus(-(w0.astype(f32) + jnp.tanh(xw) @ w_up.astype(f32))) - 0.5
    decay = jnp.exp(-jnp.exp(w_log))
    a = jax.nn.sigmoid(a0.astype(f32) + xa @ a_up.astype(f32))
    g = jax.nn.sigmoid(xg) @ g_up.astype(f32)

    def heads(u):
        return u.reshape(bsz, t, N_HEADS_RWKV, HEAD_DIM)

    kk = heads(k * k_k.astype(f32))
    kk = kk / jnp.maximum(jnp.sqrt(jnp.sum(kk * kk, axis=-1, keepdims=True)), 1e-12)
    k = k * (1.0 + (a - 1.0) * k_a.astype(f32))
    r_h, k_h, v_h, w_h, a_h = heads(r), heads(k), heads(v), heads(decay), heads(a)

    def step(state, inp):
        r_t, w_t, k_t, v_t, ia_t, ib_t = inp
        sa = jnp.einsum('bhvk,bhk->bhv', state, ia_t)
        state = (state * w_t[:, :, None, :] + sa[..., None] * ib_t[:, :, None, :]
                 + v_t[..., None] * k_t[:, :, None, :])
        return state, jnp.einsum('bhvk,bhk->bhv', state, r_t)

    def tm(u):
        return jnp.swapaxes(u, 0, 1)

    s0 = jnp.zeros((bsz, N_HEADS_RWKV, HEAD_DIM, HEAD_DIM), f32)
    _, y = lax.scan(step, s0, (tm(r_h), tm(w_h), tm(k_h), tm(v_h), tm(-kk), tm(kk * a_h)))
    y = tm(y)
    mean = jnp.mean(y, axis=-1, keepdims=True)
    var = jnp.mean(jnp.square(y - mean), axis=-1, keepdims=True)
    y = ((y - mean) * lax.rsqrt(var + GN_EPS)).reshape(bsz, t, D_RWKV)
    y = y * lnx_w.astype(f32) + lnx_b.astype(f32)
    bonus = jnp.sum(r_h * k_h * r_k.astype(f32), axis=-1, keepdims=True) * v_h
    return (y + bonus.reshape(bsz, t, D_RWKV)) * g


def attn_rwkv_mixer(h, w_in, q_gain, k_gain, rel_bias, mu, w0, w_up, a0, a_up, g_up,
                    k_k, k_a, r_k, lnx_w, lnx_b, w_out):
    bsz, t, _ = h.shape
    proj = h @ w_in

    def heads(u):
        return u.reshape(bsz, t, N_HEADS_ATT, HEAD_DIM)

    q = heads(proj[..., :D_ATT])
    k = heads(proj[..., D_ATT:2 * D_ATT])
    v = heads(proj[..., 2 * D_ATT:3 * D_ATT])
    att = chunked_band_attention(q, k, v, q_gain, k_gain, rel_bias).reshape(bsz, t, D_ATT)
    rw = rwkv7_time_mix(proj[..., 3 * D_ATT:], mu, w0, w_up, a0, a_up, g_up,
                        k_k, k_a, r_k, lnx_w, lnx_b).astype(att.dtype)
    return jnp.concatenate([att, rw], axis=-1) @ w_out


def s5_ssm(u, lam_re, lam_im, log_dt, b_re, b_im, c_re, c_im, d_skip):
    f32 = jnp.float32
    bsz, t, _ = u.shape
    G, P, GS = N_SSM_GROUPS, SSM_STATE, SSM_GROUP
    uf = u.astype(f32).reshape(bsz, t, G, GS)
    lr, li = lam_re.astype(f32), lam_im.astype(f32)
    dt = jnp.exp(log_dt.astype(f32))[:, None]
    mag = jnp.exp(lr * dt)
    ab_re, ab_im = mag * jnp.cos(li * dt), mag * jnp.sin(li * dt)
    denom = lr * lr + li * li
    z_re = ((ab_re - 1.0) * lr + ab_im * li) / denom
    z_im = (ab_im * lr - (ab_re - 1.0) * li) / denom
    br, bi = b_re.astype(f32), b_im.astype(f32)
    bb_re = z_re[..., None] * br - z_im[..., None] * bi
    bb_im = z_re[..., None] * bi + z_im[..., None] * br
    bu_re = jnp.einsum('gpc,btgc->btgp', bb_re, uf)
    bu_im = jnp.einsum('gpc,btgc->btgp', bb_im, uf)
    a_re = jnp.broadcast_to(ab_re[None, None], (1, t, G, P))
    a_im = jnp.broadcast_to(ab_im[None, None], (1, t, G, P))

    def combine(left, right):
        al_re, al_im, bl_re, bl_im = left
        ar_re, ar_im, br_re, br_im = right
        return (ar_re * al_re - ar_im * al_im,
                ar_re * al_im + ar_im * al_re,
                ar_re * bl_re - ar_im * bl_im + br_re,
                ar_re * bl_im + ar_im * bl_re + br_im)

    _, _, h_re, h_im = lax.associative_scan(combine, (a_re, a_im, bu_re, bu_im), axis=1)
    y = (jnp.einsum('gcp,btgp->btgc', c_re.astype(f32), h_re)
         - jnp.einsum('gcp,btgp->btgc', c_im.astype(f32), h_im))
    y = y + d_skip.astype(f32).reshape(G, GS) * uf
    return y.reshape(bsz, t, D_SSM).astype(u.dtype)


def s5_mixer(h, w_in, lam_re, lam_im, log_dt, b_re, b_im, c_re, c_im, d_skip, w_out):
    y = jax.nn.gelu(s5_ssm(h @ w_in, lam_re, lam_im, log_dt, b_re, b_im, c_re, c_im, d_skip))
    z = y @ w_out
    return z[..., :D_MODEL] * jax.nn.sigmoid(z[..., D_MODEL:])


def setup_inputs(seed: int = 0) -> dict:
    key = jax.random.key(seed)
    ks = iter(jax.random.split(key, 48))
    f32 = jnp.float32
    ne, no = (DEPTH + 1) // 2, DEPTH // 2
    G, P, GS = N_SSM_GROUPS, SSM_STATE, SSM_GROUP

    def normal(shape, scale):
        return jax.random.normal(next(ks), shape, f32) * scale

    def gain(shape):
        return 1.0 + normal(shape, 0.02)

    x = normal((BATCH, SEQ, D_MODEL), 1.0)
    p = normal((DEPTH, BATCH, SEQ, D_PLE), 1.0)
    ffn1_norm = gain((DEPTH, D_MODEL))
    ffn1_w_gate = normal((DEPTH, D_MODEL, D_FF), D_MODEL ** -0.5)
    ffn1_w_up = normal((DEPTH, D_MODEL, D_FF), D_MODEL ** -0.5)
    ffn1_w_down = normal((DEPTH, D_FF, D_MODEL), D_FF ** -0.5)
    mix_norm = gain((DEPTH, D_MODEL))
    ffn2_norm = gain((DEPTH, D_MODEL))
    ffn2_w_gate = normal((DEPTH, D_MODEL, D_FF), D_MODEL ** -0.5)
    ffn2_w_up = normal((DEPTH, D_MODEL, D_FF), D_MODEL ** -0.5)
    ffn2_w_down = normal((DEPTH, D_FF, D_MODEL), D_FF ** -0.5)
    ple_norm = gain((DEPTH, D_MODEL))
    ple_w_gate = normal((DEPTH, D_MODEL, D_MODEL), D_MODEL ** -0.5)
    ple_w_proj = normal((DEPTH, D_PLE, D_MODEL), D_PLE ** -0.5)
    ab_w_in = normal((ne, D_MODEL, N_IN_AB), D_MODEL ** -0.5)
    att_q_gain = gain((ne, HEAD_DIM))
    att_k_gain = gain((ne, HEAD_DIM))
    att_rel_bias = normal((ne, N_HEADS_ATT, N_REL), 0.1)
    rwkv_mu = jax.random.uniform(next(ks), (ne, N_B_IN), f32)
    rwkv_w0 = jnp.linspace(-6.0, -1.0, D_RWKV, dtype=f32) + normal((ne, D_RWKV), 0.1)
    rwkv_w_up = normal((ne, DECAY_LORA, D_RWKV), 0.1 * DECAY_LORA ** -0.5)
    rwkv_a0 = normal((ne, D_RWKV), 0.1)
    rwkv_a_up = normal((ne, AAA_LORA, D_RWKV), 0.5 * AAA_LORA ** -0.5)
    rwkv_g_up = normal((ne, GATE_LORA, D_RWKV), GATE_LORA ** -0.5)
    rwkv_k_k = 0.85 + normal((ne, D_RWKV), 0.02)
    rwkv_k_a = gain((ne, D_RWKV))
    rwkv_r_k = normal((ne, N_HEADS_RWKV, HEAD_DIM), 0.1)
    rwkv_lnx_w = gain((ne, D_RWKV))
    rwkv_lnx_b = normal((ne, D_RWKV), 0.02)
    ab_w_out = normal((ne, D_ATT + D_RWKV, D_MODEL), (D_ATT + D_RWKV) ** -0.5)
    ssm_w_in = normal((no, D_MODEL, D_SSM), D_MODEL ** -0.5)
    ssm_lambda_re = -0.5 + normal((no, G, P), 0.01)
    ssm_lambda_im = math.pi * jnp.arange(P, dtype=f32) + normal((no, G, P), 0.01)
    ssm_log_dt = jax.random.uniform(next(ks), (no, G), f32, math.log(1e-3), math.log(1e-1))
    ssm_b_re = normal((no, G, P, GS), (2 * GS) ** -0.5)
    ssm_b_im = normal((no, G, P, GS), (2 * GS) ** -0.5)
    ssm_c_re = normal((no, G, GS, P), (2 * P) ** -0.5)
    ssm_c_im = normal((no, G, GS, P), (2 * P) ** -0.5)
    ssm_d = normal((no, D_SSM), 1.0)
    ssm_w_out = normal((no, D_SSM, 2 * D_MODEL), D_SSM ** -0.5)
    return {
        'x': x, 'p': p,
        'ffn1_norm': ffn1_norm, 'ffn1_w_gate': ffn1_w_gate, 'ffn1_w_up': ffn1_w_up,
        'ffn1_w_down': ffn1_w_down, 'mix_norm': mix_norm,
        'ffn2_norm': ffn2_norm, 'ffn2_w_gate': ffn2_w_gate, 'ffn2_w_up': ffn2_w_up,
        'ffn2_w_down': ffn2_w_down,
        'ple_norm': ple_norm, 'ple_w_gate': ple_w_gate, 'ple_w_proj': ple_w_proj,
        'ab_w_in': ab_w_in, 'att_q_gain': att_q_gain, 'att_k_gain': att_k_gain,
        'att_rel_bias': att_rel_bias, 'rwkv_mu': rwkv_mu, 'rwkv_w0': rwkv_w0,
        'rwkv_w_up': rwkv_w_up, 'rwkv_a0': rwkv_a0, 'rwkv_a_up': rwkv_a_up,
        'rwkv_g_up': rwkv_g_up, 'rwkv_k_k': rwkv_k_k, 'rwkv_k_a': rwkv_k_a,
        'rwkv_r_k': rwkv_r_k, 'rwkv_lnx_w': rwkv_lnx_w, 'rwkv_lnx_b': rwkv_lnx_b,
        'ab_w_out': ab_w_out,
        'ssm_w_in': ssm_w_in, 'ssm_lambda_re': ssm_lambda_re, 'ssm_lambda_im': ssm_lambda_im,
        'ssm_log_dt': ssm_log_dt, 'ssm_b_re': ssm_b_re, 'ssm_b_im': ssm_b_im,
        'ssm_c_re': ssm_c_re, 'ssm_c_im': ssm_c_im, 'ssm_d': ssm_d, 'ssm_w_out': ssm_w_out,
    }


def reference(x, p, ffn1_norm, ffn1_w_gate, ffn1_w_up, ffn1_w_down, mix_norm,
              ffn2_norm, ffn2_w_gate, ffn2_w_up, ffn2_w_down,
              ple_norm, ple_w_gate, ple_w_proj,
              ab_w_in, att_q_gain, att_k_gain, att_rel_bias, rwkv_mu, rwkv_w0,
              rwkv_w_up, rwkv_a0, rwkv_a_up, rwkv_g_up, rwkv_k_k, rwkv_k_a,
              rwkv_r_k, rwkv_lnx_w, rwkv_lnx_b, ab_w_out,
              ssm_w_in, ssm_lambda_re, ssm_lambda_im, ssm_log_dt, ssm_b_re, ssm_b_im,
              ssm_c_re, ssm_c_im, ssm_d, ssm_w_out):
    h = x
    for i in range(DEPTH):
        j = i // 2
        h = h + 0.5 * swiglu_ffn(rms_norm(h, ffn1_norm[i]), ffn1_w_gate[i],
                                 ffn1_w_up[i], ffn1_w_down[i])
        hn = rms_norm(h, mix_norm[i])
        if i % 2 == 0:
            mix = attn_rwkv_mixer(hn, ab_w_in[j], att_q_gain[j], att_k_gain[j],
                                  att_rel_bias[j], rwkv_mu[j], rwkv_w0[j], rwkv_w_up[j],
                                  rwkv_a0[j], rwkv_a_up[j], rwkv_g_up[j], rwkv_k_k[j],
                                  rwkv_k_a[j], rwkv_r_k[j], rwkv_lnx_w[j], rwkv_lnx_b[j],
                                  ab_w_out[j])
        else:
            mix = s5_mixer(hn, ssm_w_in[j], ssm_lambda_re[j], ssm_lambda_im[j],
                           ssm_log_dt[j], ssm_b_re[j], ssm_b_im[j], ssm_c_re[j],
                           ssm_c_im[j], ssm_d[j], ssm_w_out[j])
        h = h + mix
        h = h + 0.5 * swiglu_ffn(rms_norm(h, ffn2_norm[i]), ffn2_w_gate[i],
                                 ffn2_w_up[i], ffn2_w_down[i])
        gate = jax.nn.sigmoid(rms_norm(h, ple_norm[i]) @ ple_w_gate[i])
        h = h + gate * (p[i] @ ple_w_proj[i])
    return h
```

```python
import functools
import math

import jax
import jax.numpy as jnp
from jax import lax
from jax.experimental import pallas as pl
from jax.experimental.pallas import tpu as pltpu

F32 = jnp.float32
BF16 = jnp.bfloat16

RMS_EPS = 1e-6
GN_EPS = 64e-5
CHUNK = 64
N_LEFT_CHUNKS = 8
REL_CLIP = 128
HEAD_DIM = 64
ATT_TQ = 512
RWKV_L = 64
SSM_GROUP = 16
SSM_STATE = 64
SSM_L = 16
NEG = -1e30

ROW_TILE = 512
COL_TILE = 512
VMEM_LIMIT = 52 * 2 ** 20


def _cparams(*sem):
    return pltpu.CompilerParams(dimension_semantics=sem, vmem_limit_bytes=VMEM_LIMIT)


def _dot(a, b):
    return jnp.dot(a, b, preferred_element_type=F32)


def _dot_nt(a, b):
    return lax.dot_general(a, b, (((1,), (1,)), ((), ())), preferred_element_type=F32)


def _dot_tn(a, b):
    return lax.dot_general(a, b, (((0,), (0,)), ((), ())), preferred_element_type=F32)


def _bdot(a, b):
    return _dot(a.astype(BF16), b.astype(BF16))


def _bdot_nt(a, b):
    return _dot_nt(a.astype(BF16), b.astype(BF16))


def _bdot_tn(a, b):
    return _dot_tn(a.astype(BF16), b.astype(BF16))


def _rms(x, g):
    return x * lax.rsqrt(jnp.mean(x * x, axis=-1, keepdims=True) + RMS_EPS) * g


def _sigmoid(x):
    return 1.0 / (1.0 + jnp.exp(-x))


def _ffn_kernel(h_ref, g_ref, wg_ref, wu_ref, wd_ref, o_ref, n_scr, acc_scr):
    f = pl.program_id(1)

    @pl.when(f == 0)
    def _():
        n_scr[...] = _rms(h_ref[...], g_ref[...]).astype(BF16)
        acc_scr[...] = jnp.zeros_like(acc_scr)

    n = n_scr[...]
    gate = _dot(n, wg_ref[...])
    up = _dot(n, wu_ref[...])
    act = (gate * _sigmoid(gate) * up).astype(BF16)
    acc_scr[...] += _dot(act, wd_ref[...])

    @pl.when(f == pl.num_programs(1) - 1)
    def _():
        o_ref[...] = h_ref[...] + 0.5 * acc_scr[...]


def _ffn(h, g, wg, wu, wd):
    n, d = h.shape
    dff = wg.shape[1]
    tm = min(ROW_TILE, n)
    tf = min(COL_TILE, dff)
    return pl.pallas_call(
        _ffn_kernel,
        out_shape=jax.ShapeDtypeStruct((n, d), F32),
        grid=(n // tm, dff // tf),
        in_specs=[
            pl.BlockSpec((tm, d), lambda i, f: (i, 0)),
            pl.BlockSpec((1, d), lambda i, f: (0, 0)),
            pl.BlockSpec((d, tf), lambda i, f: (0, f)),
            pl.BlockSpec((d, tf), lambda i, f: (0, f)),
            pl.BlockSpec((tf, d), lambda i, f: (f, 0)),
        ],
        out_specs=pl.BlockSpec((tm, d), lambda i, f: (i, 0)),
        scratch_shapes=[pltpu.VMEM((tm, d), BF16), pltpu.VMEM((tm, d), F32)],
        compiler_params=_cparams("parallel", "arbitrary"),
        name="ffn",
    )(h, g.reshape(1, d), wg, wu, wd)


def _nmm_kernel(x_ref, g_ref, w_ref, o_ref, n_scr):
    @pl.when(pl.program_id(1) == 0)
    def _():
        n_scr[...] = _rms(x_ref[...], g_ref[...]).astype(BF16)

    o_ref[...] = _dot(n_scr[...], w_ref[...]).astype(o_ref.dtype)


def _norm_matmul(x, g, w, tn):
    n, d = x.shape
    nout = w.shape[1]
    tm = min(ROW_TILE, n)
    return pl.pallas_call(
        _nmm_kernel,
        out_shape=jax.ShapeDtypeStruct((n, nout), F32),
        grid=(n // tm, nout // tn),
        in_specs=[
            pl.BlockSpec((tm, d), lambda i, j: (i, 0)),
            pl.BlockSpec((1, d), lambda i, j: (0, 0)),
            pl.BlockSpec((d, tn), lambda i, j: (0, j)),
        ],
        out_specs=pl.BlockSpec((tm, tn), lambda i, j: (i, j)),
        scratch_shapes=[pltpu.VMEM((tm, d), BF16)],
        compiler_params=_cparams("parallel", "arbitrary"),
        name="norm_matmul",
    )(x, g.reshape(1, d), w)


def _out2_kernel(res_ref, a1_ref, a2_ref, w1_ref, w2_ref, o_ref):
    o_ref[...] = res_ref[...] + _dot(a1_ref[...], w1_ref[...]) + _dot(a2_ref[...], w2_ref[...])


def _out2(res, a1, a2, w):
    n, d = res.shape
    k1 = a1.shape[1]
    tm = min(ROW_TILE, n)
    tn = min(COL_TILE, d)
    return pl.pallas_call(
        _out2_kernel,
        out_shape=jax.ShapeDtypeStruct((n, d), F32),
        grid=(n // tm, d // tn),
        in_specs=[
            pl.BlockSpec((tm, tn), lambda i, j: (i, j)),
            pl.BlockSpec((tm, k1), lambda i, j: (i, 0)),
            pl.BlockSpec((tm, k1), lambda i, j: (i, 0)),
            pl.BlockSpec((k1, tn), lambda i, j: (0, j)),
            pl.BlockSpec((k1, tn), lambda i, j: (1, j)),
        ],
        out_specs=pl.BlockSpec((tm, tn), lambda i, j: (i, j)),
        compiler_params=_cparams("parallel", "arbitrary"),
        name="mixer_out",
    )(res, a1, a2, w, w)


def _glu_kernel(res_ref, a_ref, wa_ref, wb_ref, o_ref):
    a = a_ref[...]
    za = _dot(a, wa_ref[...])
    zb = _dot(a, wb_ref[...])
    o_ref[...] = res_ref[...] + za * _sigmoid(zb)


def _glu_out(res, a, w):
    n, d = res.shape
    k = a.shape[1]
    tm = min(ROW_TILE, n)
    tn = min(COL_TILE, d)
    nb = d // tn
    return pl.pallas_call(
        _glu_kernel,
        out_shape=jax.ShapeDtypeStruct((n, d), F32),
        grid=(n // tm, nb),
        in_specs=[
            pl.BlockSpec((tm, tn), lambda i, j: (i, j)),
            pl.BlockSpec((tm, k), lambda i, j: (i, 0)),
            pl.BlockSpec((k, tn), lambda i, j: (0, j)),
            pl.BlockSpec((k, tn), lambda i, j: (0, j + nb)),
        ],
        out_specs=pl.BlockSpec((tm, tn), lambda i, j: (i, j)),
        compiler_params=_cparams("parallel", "arbitrary"),
        name="glu_out",
    )(res, a, w, w)


def _ple_kernel(h_ref, hres_ref, g_ref, p_ref, wg_ref, wp_ref, o_ref, n_scr):
    @pl.when(pl.program_id(1) == 0)
    def _():
        n_scr[...] = _rms(h_ref[...], g_ref[...]).astype(BF16)

    gate = _sigmoid(_dot(n_scr[...], wg_ref[...]))
    proj = _dot(p_ref[...].astype(BF16), wp_ref[...])
    o_ref[...] = hres_ref[...] + gate * proj


def _ple(h, g, p, wg, wp):
    n, d = h.shape
    dp = p.shape[1]
    tm = min(ROW_TILE, n)
    tn = min(COL_TILE, d)
    return pl.pallas_call(
        _ple_kernel,
        out_shape=jax.ShapeDtypeStruct((n, d), F32),
        grid=(n // tm, d // tn),
        in_specs=[
            pl.BlockSpec((tm, d), lambda i, j: (i, 0)),
            pl.BlockSpec((tm, tn), lambda i, j: (i, j)),
            pl.BlockSpec((1, d), lambda i, j: (0, 0)),
            pl.BlockSpec((tm, dp), lambda i, j: (i, 0)),
            pl.BlockSpec((d, tn), lambda i, j: (0, j)),
            pl.BlockSpec((dp, tn), lambda i, j: (0, j)),
        ],
        out_specs=pl.BlockSpec((tm, tn), lambda i, j: (i, j)),
        scratch_shapes=[pltpu.VMEM((tm, d), BF16)],
        compiler_params=_cparams("parallel", "arbitrary"),
        name="ple",
    )(h, h, g.reshape(1, d), p, wg, wp)


def _bias_kernel(tbl_ref, o_ref):
    tq = o_ref.shape[1]
    tk = o_ref.shape[2]
    n_rel = tbl_ref.shape[2]
    width = 2 * tk
    n_idx = lax.broadcasted_iota(jnp.int32, (n_rel, width), 1)
    c_idx = lax.broadcasted_iota(jnp.int32, (n_rel, width), 0)
    m = jnp.where(n_idx < tk, n_idx, n_idx - width)
    idx = jnp.clip(tq - m, -(CHUNK - 1), REL_CLIP) + (CHUNK - 1)
    onehot = (c_idx == idx).astype(F32)
    tbl = jnp.broadcast_to(tbl_ref[0], (8, n_rel))
    ext = jnp.dot(tbl, onehot, preferred_element_type=F32, precision=lax.Precision.HIGHEST)[0:1]
    x = jnp.broadcast_to(ext, (tq, width))
    rolled = pltpu.roll(x, 0, 1, stride=1, stride_axis=0)[:, :tk]
    qc = lax.broadcasted_iota(jnp.int32, (tq, tk), 0) // CHUNK
    kc = lax.broadcasted_iota(jnp.int32, (tq, tk), 1) // CHUNK
    ok = (kc >= qc) & (kc <= qc + N_LEFT_CHUNKS)
    o_ref[0] = jnp.where(ok, rolled, NEG)


def _bias_tiles(rel_bias):
    nh, n_rel = rel_bias.shape
    return pl.pallas_call(
        _bias_kernel,
        out_shape=jax.ShapeDtypeStruct((nh, ATT_TQ, 2 * ATT_TQ), F32),
        grid=(nh,),
        in_specs=[pl.BlockSpec((1, 1, n_rel), lambda h: (h, 0, 0))],
        out_specs=pl.BlockSpec((1, ATT_TQ, 2 * ATT_TQ), lambda h: (h, 0, 0)),
        compiler_params=_cparams("parallel"),
        name="rel_bias_tiles",
    )(rel_bias.reshape(nh, 1, n_rel))


def _attn_kernel(q_ref, kp_ref, kc_ref, vp_ref, vc_ref, bias_ref, qg_ref, kg_ref, o_ref):
    has_prev = pl.program_id(2) > 0
    tq = q_ref.shape[0]
    qg = qg_ref[...] * (HEAD_DIM ** -0.5)
    kg = kg_ref[...]
    outs = []
    for hh in range(2):
        sl = slice(hh * HEAD_DIM, (hh + 1) * HEAD_DIM)
        q = _rms(q_ref[:, sl], qg).astype(BF16)
        kp = _rms(kp_ref[:, sl], kg).astype(BF16)
        kc = _rms(kc_ref[:, sl], kg).astype(BF16)
        sp = _dot_nt(q, kp) + bias_ref[hh, :, :tq]
        sp = jnp.where(has_prev, sp, NEG)
        sc = _dot_nt(q, kc) + bias_ref[hh, :, tq:]
        mx = jnp.maximum(jnp.max(sp, axis=-1, keepdims=True), jnp.max(sc, axis=-1, keepdims=True))
        pp = jnp.exp(sp - mx)
        pc = jnp.exp(sc - mx)
        den = jnp.sum(pp, axis=-1, keepdims=True) + jnp.sum(pc, axis=-1, keepdims=True)
        o = _dot(pp.astype(BF16), vp_ref[:, sl].astype(BF16)) + _dot(pc.astype(BF16), vc_ref[:, sl].astype(BF16))
        outs.append(o / den)
    o_ref[...] = jnp.concatenate(outs, axis=1).astype(o_ref.dtype)


def _attention(proj, bias, q_gain, k_gain, bsz, seq, d_att):
    n = bsz * seq
    tq = ATT_TQ
    nqb = seq // tq
    npair = d_att // (2 * HEAD_DIM)
    w = 2 * HEAD_DIM

    def cur(col0):
        return pl.BlockSpec((tq, w), lambda hp, b, qb: (b * nqb + qb, col0 + hp))

    def prev(col0):
        return pl.BlockSpec((tq, w), lambda hp, b, qb: (b * nqb + jnp.maximum(qb - 1, 0), col0 + hp))

    return pl.pallas_call(
        _attn_kernel,
        out_shape=jax.ShapeDtypeStruct((n, d_att), BF16),
        grid=(npair, bsz, nqb),
        in_specs=[
            cur(0), prev(npair), cur(npair), prev(2 * npair), cur(2 * npair),
            pl.BlockSpec((2, tq, 2 * tq), lambda hp, b, qb: (hp, 0, 0)),
            pl.BlockSpec((1, HEAD_DIM), lambda hp, b, qb: (0, 0)),
            pl.BlockSpec((1, HEAD_DIM), lambda hp, b, qb: (0, 0)),
        ],
        out_specs=pl.BlockSpec((tq, w), lambda hp, b, qb: (b * nqb + qb, hp)),
        compiler_params=_cparams("parallel", "parallel", "arbitrary"),
        name="band_attention",
    )(proj, proj, proj, proj, proj, bias, q_gain.reshape(1, HEAD_DIM), k_gain.reshape(1, HEAD_DIM))


def _split3(x):
    hi = x.astype(BF16)
    r1 = x - hi.astype(F32)
    mid = r1.astype(BF16)
    lo = (r1 - mid.astype(F32)).astype(BF16)
    return hi, mid, lo


def _token_shift(z, carry_ref, mu):
    rows = lax.broadcasted_iota(jnp.int32, z.shape, 0)
    prev = jnp.where(rows == 0, carry_ref[...], pltpu.roll(z, 1, 0))
    carry_ref[...] = z[z.shape[0] - 1:, :]
    return z + (prev - z) * mu


def _rwkv_kernel(r_ref, k_ref, v_ref, l_ref, mur_ref, muk_ref, muv_ref, mul_ref,
                 w0_ref, a0_ref, kk_ref, ka_ref, rk_ref, lnw_ref, lnb_ref,
                 wup_ref, aup_ref, gup_ref, o_ref,
                 s_scr, cr_scr, ck_scr, cv_scr, cl_scr):
    first = pl.program_id(1) == 0
    L = r_ref.shape[0]
    d = r_ref.shape[1]
    nh = d // HEAD_DIM

    @pl.when(first)
    def _():
        s_scr[...] = jnp.zeros_like(s_scr)
        for c in (cr_scr, ck_scr, cv_scr, cl_scr):
            c[...] = jnp.zeros_like(c)

    r = _token_shift(r_ref[...], cr_scr, mur_ref[...])
    k = _token_shift(k_ref[...], ck_scr, muk_ref[...])
    v = _token_shift(v_ref[...], cv_scr, muv_ref[...])
    lo = _token_shift(l_ref[...], cl_scr, mul_ref[...])
    nw = wup_ref.shape[0]
    na = aup_ref.shape[0]
    xw, xa, xg = lo[:, :nw], lo[:, nw:nw + na], lo[:, nw + na:]

    wpre = w0_ref[...] + _bdot(jnp.tanh(xw), wup_ref[...])
    w_log = -(jnp.maximum(-wpre, 0.0) + jnp.log(1.0 + jnp.exp(-jnp.abs(wpre)))) - 0.5
    lw = -jnp.exp(w_log)
    a = _sigmoid(a0_ref[...] + _bdot(xa, aup_ref[...]))
    g = _bdot(_sigmoid(xg), gup_ref[...])

    kkf = k * kk_ref[...]
    k2 = k * (1.0 + (a - 1.0) * ka_ref[...])

    ti = lax.broadcasted_iota(jnp.int32, (L, L), 0)
    si = lax.broadcasted_iota(jnp.int32, (L, L), 1)
    incl = si <= ti
    strict = si < ti
    tri = jnp.where(incl, 1.0, 0.0).astype(BF16)
    h3 = _split3(lw)
    cum = _dot(tri, h3[0]) + _dot(tri, h3[1]) + _dot(tri, h3[2])
    e_pos = jnp.exp(cum)
    e_neg = jnp.exp(-cum)
    e_prev = jnp.exp(cum - lw)
    w_all = e_pos[L - 1:, :]

    rk = r * k2 * rk_ref[...]
    outs = []
    for h in range(nh):
        sl = slice(h * HEAD_DIM, (h + 1) * HEAD_DIM)
        kk = kkf[:, sl]
        nrm = jnp.sqrt(jnp.sum(kk * kk, axis=-1, keepdims=True))
        kk = kk / jnp.maximum(nrm, 1e-12)
        ah = -kk * e_prev[:, sl]
        bh = kk * a[:, sl] * e_neg[:, sl]
        kh = k2[:, sl] * e_neg[:, sl]
        rh = r[:, sl] * e_pos[:, sl]
        vh = v[:, sl]
        s0 = s_scr[h]
        ar = jnp.concatenate([ah, rh], axis=0)
        bk = jnp.concatenate([bh, kh], axis=0)
        gm = _bdot_nt(ar, bk)
        ps = _bdot_nt(ar, s0)
        nmat = jnp.where(strict, gm[:L, :L], 0.0)
        aak = jnp.where(strict, gm[:L, L:], 0.0)
        arb = jnp.where(incl, gm[L:, :L], 0.0)
        ark = jnp.where(incl, gm[L:, L:], 0.0)
        u = ps[:L] + _bdot(aak, vh)
        pw = nmat
        steps = max(1, (L - 1).bit_length())
        for i in range(steps):
            u = u + _bdot(pw, u)
            if i + 1 < steps:
                pw = _bdot(pw, pw)
        y = ps[L:] + _bdot(arb, u) + _bdot(ark, vh)
        s_scr[h] = (s0 + _bdot_tn(u, bh) + _bdot_tn(vh, kh)) * w_all[:, sl]
        mean = jnp.mean(y, axis=-1, keepdims=True)
        yc = y - mean
        var = jnp.mean(yc * yc, axis=-1, keepdims=True)
        yn = yc * lax.rsqrt(var + GN_EPS)
        bonus = jnp.sum(rk[:, sl], axis=-1, keepdims=True) * vh
        outs.append((yn, bonus))
    yn = jnp.concatenate([o[0] for o in outs], axis=1)
    bonus = jnp.concatenate([o[1] for o in outs], axis=1)
    o_ref[...] = ((yn * lnw_ref[...] + lnb_ref[...] + bonus) * g).astype(o_ref.dtype)


def _rwkv(proj, col0, bsz, seq, d, mu, w0, w_up, a0, a_up, g_up, k_k, k_a, r_k, lnx_w, lnx_b):
    n = bsz * seq
    L = RWKV_L
    nt = seq // L
    nl = mu.shape[0] - 3 * d
    row = lambda x: x.reshape(1, -1)
    cb = col0 // d
    lb = (col0 + 3 * d) // nl

    def zspec(width, blk):
        return pl.BlockSpec((L, width), lambda b, t: (b * nt + t, blk))

    def pspec(shape):
        return pl.BlockSpec(shape, lambda b, t: (0,) * len(shape))

    nh = d // HEAD_DIM
    return pl.pallas_call(
        _rwkv_kernel,
        out_shape=jax.ShapeDtypeStruct((n, d), BF16),
        grid=(bsz, nt),
        in_specs=[
            zspec(d, cb), zspec(d, cb + 1), zspec(d, cb + 2), zspec(nl, lb),
            pspec((1, d)), pspec((1, d)), pspec((1, d)), pspec((1, nl)),
            pspec((1, d)), pspec((1, d)), pspec((1, d)), pspec((1, d)), pspec((1, d)),
            pspec((1, d)), pspec((1, d)),
            pspec(w_up.shape), pspec(a_up.shape), pspec(g_up.shape),
        ],
        out_specs=pl.BlockSpec((L, d), lambda b, t: (b * nt + t, 0)),
        scratch_shapes=[
            pltpu.VMEM((nh, HEAD_DIM, HEAD_DIM), F32),
            pltpu.VMEM((1, d), F32), pltpu.VMEM((1, d), F32), pltpu.VMEM((1, d), F32),
            pltpu.VMEM((1, nl), F32),
        ],
        compiler_params=_cparams("parallel", "arbitrary"),
        name="rwkv7",
    )(proj, proj, proj, proj,
      row(mu[:d]), row(mu[d:2 * d]), row(mu[2 * d:3 * d]), row(mu[3 * d:]),
      row(w0), row(a0), row(k_k), row(k_a), row(r_k), row(lnx_w), row(lnx_b),
      w_up.astype(BF16), a_up.astype(BF16), g_up.astype(BF16))


def _s5_prep_kernel(lr_ref, li_ref, ldt_ref, bre_ref, bim_ref, cre_ref, cim_ref,
                    crt_ref, cit_ref, m_ref, r_ref, q_ref):
    L = SSM_L
    gs = SSM_GROUP
    w = L * gs
    lr = lr_ref[0]
    li = li_ref[0]
    dt = jnp.exp(ldt_ref[0])
    mag = jnp.exp(lr * dt)
    ab_re = mag * jnp.cos(li * dt)
    ab_im = mag * jnp.sin(li * dt)
    den = lr * lr + li * li
    z_re = ((ab_re - 1.0) * lr + ab_im * li) / den
    z_im = (ab_im * lr - (ab_re - 1.0) * li) / den
    bre = bre_ref[0]
    bim = bim_ref[0]
    bb_re = z_re * bre - z_im * bim
    bb_im = z_re * bim + z_im * bre
    lane = lax.broadcasted_iota(jnp.int32, (1, w), 1)
    step = (lane // gs).astype(F32)

    def apow(nsteps):
        mg = jnp.exp(lr * dt * nsteps)
        ang = li * dt * nsteps
        return mg * jnp.cos(ang), mg * jnp.sin(ang)

    pr, pi = apow((L - 1.0) - step)
    rr = pr * bb_re - pi * bb_im
    ri = pr * bb_im + pi * bb_re
    rrev = jnp.concatenate([rr, ri], axis=0)
    r_ref[0] = rrev.astype(r_ref.dtype)
    qr, qi = apow(step + 1.0)
    crt = crt_ref[0]
    cit = cit_ref[0]
    q_ref[0] = jnp.concatenate([crt * qr - cit * qi, -(crt * qi + cit * qr)], axis=0).astype(q_ref.dtype)
    cri = jnp.concatenate([cre_ref[0], -cim_ref[0]], axis=1)
    kp = jnp.dot(cri, rrev, preferred_element_type=F32, precision=lax.Precision.HIGHEST)
    for t in range(L):
        dsh = (L - 1 - t) * gs
        strip = pltpu.roll(kp, (w - dsh) % w, 1) if dsh else kp
        strip = jnp.where(lane < (t + 1) * gs, strip, 0.0)
        m_ref[0, t * gs:(t + 1) * gs, :] = strip.astype(m_ref.dtype)


def _s5_prep(lam_re, lam_im, log_dt, b_re, b_im, c_re, c_im):
    G, P = lam_re.shape
    gs = SSM_GROUP
    L = SSM_L
    w = L * gs
    tile = lambda x: jnp.tile(x, (1, 1, L))
    col = lambda x: x.reshape(G, P, 1)

    def spec(shape):
        return pl.BlockSpec((1,) + shape, lambda g: (g, 0, 0))

    return pl.pallas_call(
        _s5_prep_kernel,
        out_shape=[jax.ShapeDtypeStruct((G, w, w), BF16),
                   jax.ShapeDtypeStruct((G, 2 * P, w), BF16),
                   jax.ShapeDtypeStruct((G, 2 * P, w), BF16)],
        grid=(G,),
        in_specs=[spec((P, 1)), spec((P, 1)), spec((1, 1)), spec((P, w)), spec((P, w)),
                  spec((gs, P)), spec((gs, P)), spec((P, w)), spec((P, w))],
        out_specs=[spec((w, w)), spec((2 * P, w)), spec((2 * P, w))],
        compiler_params=_cparams("parallel"),
        name="s5_prep",
    )(col(lam_re), col(lam_im), log_dt.reshape(G, 1, 1), tile(b_re), tile(b_im), c_re, c_im,
      tile(jnp.swapaxes(c_re, 1, 2)), tile(jnp.swapaxes(c_im, 1, 2)))


def _s5_kernel(u_ref, m_ref, r_ref, q_ref, lr_ref, li_ref, ldt_ref, d_ref, o_ref,
               gre_scr, gim_scr, hre_scr, him_scr):
    P = SSM_STATE
    nc = u_ref.shape[2]
    u = u_ref[0, 0]
    ub = u.astype(BF16)
    y = _dot_nt(ub, m_ref[0])
    gall = _dot_nt(ub, r_ref[0])
    gre_scr[...] = gall[:, :P]
    gim_scr[...] = gall[:, P:]
    dtl = jnp.exp(ldt_ref[0]) * SSM_L
    lr = lr_ref[0]
    li = li_ref[0]
    mg = jnp.exp(lr * dtl)
    al_re = mg * jnp.cos(li * dtl)
    al_im = mg * jnp.sin(li * dtl)

    def body(c, carry):
        hre, him = carry
        hre_scr[pl.ds(c, 1), :] = hre
        him_scr[pl.ds(c, 1), :] = him
        gr = gre_scr[pl.ds(c, 1), :]
        gi = gim_scr[pl.ds(c, 1), :]
        return al_re * hre - al_im * him + gr, al_re * him + al_im * hre + gi

    zero = jnp.zeros((1, P), F32)
    lax.fori_loop(0, nc, body, (zero, zero))
    hp = jnp.concatenate([hre_scr[...], him_scr[...]], axis=1).astype(BF16)
    y = y + _dot(hp, q_ref[0]) + d_ref[0] * u
    o_ref[0, 0] = (0.5 * y * (1.0 + jnp.tanh(math.sqrt(2.0 / math.pi) * (y + 0.044715 * y * y * y)))).astype(o_ref.dtype)


def _s5(u, bsz, seq, mats, lam_re, lam_im, log_dt, d_skip):
    m, r, q = mats
    G, P = lam_re.shape
    gs = SSM_GROUP
    L = SSM_L
    w = L * gs
    nc = seq // L
    ug = u.reshape(bsz, nc, L, G, gs).transpose(3, 0, 1, 2, 4).reshape(G, bsz, nc, w)
    d_t = jnp.tile(d_skip.reshape(G, 1, gs), (1, 1, L))

    def gspec(shape):
        return pl.BlockSpec((1,) + shape, lambda g, b: (g, 0, 0))

    yg = pl.pallas_call(
        _s5_kernel,
        out_shape=jax.ShapeDtypeStruct((G, bsz, nc, w), BF16),
        grid=(G, bsz),
        in_specs=[
            pl.BlockSpec((1, 1, nc, w), lambda g, b: (g, b, 0, 0)),
            gspec((w, w)), gspec((2 * P, w)), gspec((2 * P, w)),
            gspec((1, P)), gspec((1, P)), gspec((1, 1)), gspec((1, w)),
        ],
        out_specs=pl.BlockSpec((1, 1, nc, w), lambda g, b: (g, b, 0, 0)),
        scratch_shapes=[pltpu.VMEM((nc, P), F32)] * 4,
        compiler_params=_cparams("parallel", "arbitrary"),
        name="s5_ssm",
    )(ug, m, r, q, lam_re.reshape(G, 1, P), lam_im.reshape(G, 1, P), log_dt.reshape(G, 1, 1), d_t)
    return yg.reshape(G, bsz, nc, L, gs).transpose(1, 2, 3, 0, 4).reshape(bsz * seq, G * gs)


def kernel(x, p, ffn1_norm, ffn1_w_gate, ffn1_w_up, ffn1_w_down, mix_norm, ffn2_norm, ffn2_w_gate, ffn2_w_up, ffn2_w_down, ple_norm, ple_w_gate, ple_w_proj, ab_w_in, att_q_gain, att_k_gain, att_rel_bias, rwkv_mu, rwkv_w0, rwkv_w_up, rwkv_a0, rwkv_a_up, rwkv_g_up, rwkv_k_k, rwkv_k_a, rwkv_r_k, rwkv_lnx_w, rwkv_lnx_b, ab_w_out, ssm_w_in, ssm_lambda_re, ssm_lambda_im, ssm_log_dt, ssm_b_re, ssm_b_im, ssm_c_re, ssm_c_im, ssm_d, ssm_w_out):
    bsz, seq, d = x.shape
    depth = p.shape[0]
    n = bsz * seq
    bf = lambda w: w.astype(BF16)
    h = x.reshape(n, d)
    for i in range(depth):
        j = i // 2
        h = _ffn(h, ffn1_norm[i], bf(ffn1_w_gate[i]), bf(ffn1_w_up[i]), bf(ffn1_w_down[i]))
        if i % 2 == 0:
            d_att = att_rel_bias.shape[1] * HEAD_DIM
            d_rw = rwkv_w0.shape[1]
            n_in = ab_w_in.shape[2]
            tn = 640 if n_in % 640 == 0 else 128
            proj = _norm_matmul(h, mix_norm[i], bf(ab_w_in[j]), tn)
            bias = _bias_tiles(att_rel_bias[j])
            att = _attention(proj, bias, att_q_gain[j], att_k_gain[j], bsz, seq, d_att)
            rw = _rwkv(proj, 3 * d_att, bsz, seq, d_rw, rwkv_mu[j], rwkv_w0[j], rwkv_w_up[j],
                       rwkv_a0[j], rwkv_a_up[j], rwkv_g_up[j], rwkv_k_k[j], rwkv_k_a[j],
                       rwkv_r_k[j].reshape(-1), rwkv_lnx_w[j], rwkv_lnx_b[j])
            h = _out2(h, att, rw, bf(ab_w_out[j]))
        else:
            d_ssm = ssm_w_in.shape[2]
            u = _norm_matmul(h, mix_norm[i], bf(ssm_w_in[j]), min(COL_TILE, d_ssm))
            mats = _s5_prep(ssm_lambda_re[j], ssm_lambda_im[j], ssm_log_dt[j], ssm_b_re[j],
                            ssm_b_im[j], ssm_c_re[j], ssm_c_im[j])
            y = _s5(u, bsz, seq, mats, ssm_lambda_re[j], ssm_lambda_im[j], ssm_log_dt[j], ssm_d[j])
            h = _glu_out(h, y, bf(ssm_w_out[j]))
        h = _ffn(h, ffn2_norm[i], bf(ffn2_w_gate[i]), bf(ffn2_w_up[i]), bf(ffn2_w_down[i]))
        h = _ple(h, ple_norm[i], p[i].reshape(n, -1), bf(ple_w_gate[i]), bf(ple_w_proj[i]))
    return h.reshape(bsz, seq, d)
```

```python
import functools
import math

import jax
import jax.numpy as jnp
from jax import lax
from jax.experimental import pallas as pl
from jax.experimental.pallas import tpu as pltpu

F32 = jnp.float32
BF16 = jnp.bfloat16

RMS_EPS = 1e-6
GN_EPS = 64e-5
CHUNK = 64
N_LEFT_CHUNKS = 8
REL_CLIP = 128
HEAD_DIM = 64
ATT_TQ = 512
RWKV_L = 64
SSM_GROUP = 16
SSM_STATE = 64
SSM_L = 16
NEG = -1e30

ROW_TILE = 512
COL_TILE = 512
VMEM_LIMIT = 52 * 2 ** 20


def _cparams(*sem):
    return pltpu.CompilerParams(dimension_semantics=sem, vmem_limit_bytes=VMEM_LIMIT)


def _dot(a, b):
    return jnp.dot(a, b, preferred_element_type=F32)


def _dot_nt(a, b):
    return lax.dot_general(a, b, (((1,), (1,)), ((), ())), preferred_element_type=F32)


def _dot_tn(a, b):
    return lax.dot_general(a, b, (((0,), (0,)), ((), ())), preferred_element_type=F32)


def _bdot(a, b):
    return _dot(a.astype(BF16), b.astype(BF16))


def _bdot_nt(a, b):
    return _dot_nt(a.astype(BF16), b.astype(BF16))


def _bdot_tn(a, b):
    return _dot_tn(a.astype(BF16), b.astype(BF16))


def _rms(x, g):
    return x * lax.rsqrt(jnp.mean(x * x, axis=-1, keepdims=True) + RMS_EPS) * g


def _sigmoid(x):
    return 1.0 / (1.0 + jnp.exp(-x))


def _ffn_kernel(h_ref, g_ref, wg_ref, wu_ref, wd_ref, o_ref, n_scr, acc_scr):
    f = pl.program_id(1)

    @pl.when(f == 0)
    def _():
        n_scr[...] = _rms(h_ref[...], g_ref[...]).astype(BF16)
        acc_scr[...] = jnp.zeros_like(acc_scr)

    n = n_scr[...]
    gate = _dot(n, wg_ref[...])
    up = _dot(n, wu_ref[...])
    act = (gate * _sigmoid(gate) * up).astype(BF16)
    acc_scr[...] += _dot(act, wd_ref[...])

    @pl.when(f == pl.num_programs(1) - 1)
    def _():
        o_ref[...] = h_ref[...] + 0.5 * acc_scr[...]


def _ffn(h, g, wg, wu, wd):
    n, d = h.shape
    dff = wg.shape[1]
    tm = min(ROW_TILE, n)
    tf = min(COL_TILE, dff)
    return pl.pallas_call(
        _ffn_kernel,
        out_shape=jax.ShapeDtypeStruct((n, d), F32),
        grid=(n // tm, dff // tf),
        in_specs=[
            pl.BlockSpec((tm, d), lambda i, f: (i, 0)),
            pl.BlockSpec((1, d), lambda i, f: (0, 0)),
            pl.BlockSpec((d, tf), lambda i, f: (0, f)),
            pl.BlockSpec((d, tf), lambda i, f: (0, f)),
            pl.BlockSpec((tf, d), lambda i, f: (f, 0)),
        ],
        out_specs=pl.BlockSpec((tm, d), lambda i, f: (i, 0)),
        scratch_shapes=[pltpu.VMEM((tm, d), BF16), pltpu.VMEM((tm, d), F32)],
        compiler_params=_cparams("parallel", "arbitrary"),
        name="ffn",
    )(h, g.reshape(1, d), wg, wu, wd)


def _nmm_kernel(x_ref, g_ref, w_ref, o_ref, n_scr):
    @pl.when(pl.program_id(1) == 0)
    def _():
        n_scr[...] = _rms(x_ref[...], g_ref[...]).astype(BF16)

    o_ref[...] = _dot(n_scr[...], w_ref[...]).astype(o_ref.dtype)


def _norm_matmul(x, g, w, tm, tn):
    n, d = x.shape
    nout = w.shape[1]
    tm = min(tm, n)
    return pl.pallas_call(
        _nmm_kernel,
        out_shape=jax.ShapeDtypeStruct((n, nout), F32),
        grid=(n // tm, nout // tn),
        in_specs=[
            pl.BlockSpec((tm, d), lambda i, j: (i, 0)),
            pl.BlockSpec((1, d), lambda i, j: (0, 0)),
            pl.BlockSpec((d, tn), lambda i, j: (0, j)),
        ],
        out_specs=pl.BlockSpec((tm, tn), lambda i, j: (i, j)),
        scratch_shapes=[pltpu.VMEM((tm, d), BF16)],
        compiler_params=_cparams("parallel", "arbitrary"),
        name="norm_matmul",
    )(x, g.reshape(1, d), w)


def _out2_kernel(res_ref, a1_ref, a2_ref, w1_ref, w2_ref, o_ref):
    o_ref[...] = res_ref[...] + _dot(a1_ref[...], w1_ref[...]) + _dot(a2_ref[...], w2_ref[...])


def _out2(res, a1, a2, w):
    n, d = res.shape
    k1 = a1.shape[1]
    tm = min(ROW_TILE, n)
    return pl.pallas_call(
        _out2_kernel,
        out_shape=jax.ShapeDtypeStruct((n, d), F32),
        grid=(n // tm,),
        in_specs=[
            pl.BlockSpec((tm, d), lambda i: (i, 0)),
            pl.BlockSpec((tm, k1), lambda i: (i, 0)),
            pl.BlockSpec((tm, k1), lambda i: (i, 0)),
            pl.BlockSpec((k1, d), lambda i: (0, 0)),
            pl.BlockSpec((k1, d), lambda i: (1, 0)),
        ],
        out_specs=pl.BlockSpec((tm, d), lambda i: (i, 0)),
        compiler_params=_cparams("parallel"),
        name="mixer_out",
    )(res, a1, a2, w, w)


def _glu_kernel(res_ref, a_ref, wa_ref, wb_ref, o_ref):
    a = a_ref[...]
    za = _dot(a, wa_ref[...])
    zb = _dot(a, wb_ref[...])
    o_ref[...] = res_ref[...] + za * _sigmoid(zb)


def _glu_out(res, a, w):
    n, d = res.shape
    k = a.shape[1]
    tm = min(ROW_TILE, n)
    return pl.pallas_call(
        _glu_kernel,
        out_shape=jax.ShapeDtypeStruct((n, d), F32),
        grid=(n // tm,),
        in_specs=[
            pl.BlockSpec((tm, d), lambda i: (i, 0)),
            pl.BlockSpec((tm, k), lambda i: (i, 0)),
            pl.BlockSpec((k, d), lambda i: (0, 0)),
            pl.BlockSpec((k, d), lambda i: (0, 1)),
        ],
        out_specs=pl.BlockSpec((tm, d), lambda i: (i, 0)),
        compiler_params=_cparams("parallel"),
        name="glu_out",
    )(res, a, w, w)


def _ple_kernel(h_ref, g_ref, p_ref, wg_ref, wp_ref, o_ref):
    h = h_ref[...]
    gate = _sigmoid(_dot(_rms(h, g_ref[...]).astype(BF16), wg_ref[...]))
    proj = _dot(p_ref[...].astype(BF16), wp_ref[...])
    o_ref[...] = h + gate * proj


def _ple(h, g, p, wg, wp):
    n, d = h.shape
    dp = p.shape[1]
    tm = min(ROW_TILE, n)
    return pl.pallas_call(
        _ple_kernel,
        out_shape=jax.ShapeDtypeStruct((n, d), F32),
        grid=(n // tm,),
        in_specs=[
            pl.BlockSpec((tm, d), lambda i: (i, 0)),
            pl.BlockSpec((1, d), lambda i: (0, 0)),
            pl.BlockSpec((tm, dp), lambda i: (i, 0)),
            pl.BlockSpec((d, d), lambda i: (0, 0)),
            pl.BlockSpec((dp, d), lambda i: (0, 0)),
        ],
        out_specs=pl.BlockSpec((tm, d), lambda i: (i, 0)),
        compiler_params=_cparams("parallel"),
        name="ple",
    )(h, g.reshape(1, d), p, wg, wp)


def _bias_kernel(tbl_ref, o_ref):
    tq = o_ref.shape[1]
    tk = o_ref.shape[2]
    n_rel = tbl_ref.shape[2]
    width = 2 * tk
    n_idx = lax.broadcasted_iota(jnp.int32, (n_rel, width), 1)
    c_idx = lax.broadcasted_iota(jnp.int32, (n_rel, width), 0)
    m = jnp.where(n_idx < tk, n_idx, n_idx - width)
    idx = jnp.clip(tq - m, -(CHUNK - 1), REL_CLIP) + (CHUNK - 1)
    onehot = (c_idx == idx).astype(F32)
    tbl = jnp.broadcast_to(tbl_ref[0], (8, n_rel))
    ext = jnp.dot(tbl, onehot, preferred_element_type=F32, precision=lax.Precision.HIGHEST)[0:1]
    x = jnp.broadcast_to(ext, (tq, width))
    rolled = pltpu.roll(x, 0, 1, stride=1, stride_axis=0)[:, :tk]
    qc = lax.broadcasted_iota(jnp.int32, (tq, tk), 0) // CHUNK
    kc = lax.broadcasted_iota(jnp.int32, (tq, tk), 1) // CHUNK
    ok = (kc >= qc) & (kc <= qc + N_LEFT_CHUNKS)
    o_ref[0] = jnp.where(ok, rolled, NEG)


def _bias_tiles(rel_bias):
    nh, n_rel = rel_bias.shape
    return pl.pallas_call(
        _bias_kernel,
        out_shape=jax.ShapeDtypeStruct((nh, ATT_TQ, 2 * ATT_TQ), F32),
        grid=(nh,),
        in_specs=[pl.BlockSpec((1, 1, n_rel), lambda h: (h, 0, 0))],
        out_specs=pl.BlockSpec((1, ATT_TQ, 2 * ATT_TQ), lambda h: (h, 0, 0)),
        compiler_params=_cparams("parallel"),
        name="rel_bias_tiles",
    )(rel_bias.reshape(nh, 1, n_rel))


def _attn_kernel(q_ref, kp_ref, kc_ref, vp_ref, vc_ref, bias_ref, qg_ref, kg_ref, o_ref):
    has_prev = pl.program_id(2) > 0
    tq = q_ref.shape[0]
    qg = qg_ref[...] * (HEAD_DIM ** -0.5)
    kg = kg_ref[...]
    outs = []
    for hh in range(2):
        sl = slice(hh * HEAD_DIM, (hh + 1) * HEAD_DIM)
        q = _rms(q_ref[:, sl], qg).astype(BF16)
        kp = _rms(kp_ref[:, sl], kg).astype(BF16)
        kc = _rms(kc_ref[:, sl], kg).astype(BF16)
        sp = _dot_nt(q, kp) + bias_ref[hh, :, :tq]
        sp = jnp.where(has_prev, sp, NEG)
        sc = _dot_nt(q, kc) + bias_ref[hh, :, tq:]
        mx = jnp.maximum(jnp.max(sp, axis=-1, keepdims=True), jnp.max(sc, axis=-1, keepdims=True))
        pp = jnp.exp(sp - mx)
        pc = jnp.exp(sc - mx)
        den = jnp.sum(pp, axis=-1, keepdims=True) + jnp.sum(pc, axis=-1, keepdims=True)
        o = _dot(pp.astype(BF16), vp_ref[:, sl].astype(BF16)) + _dot(pc.astype(BF16), vc_ref[:, sl].astype(BF16))
        outs.append(o / den)
    o_ref[...] = jnp.concatenate(outs, axis=1).astype(o_ref.dtype)


def _attention(proj, bias, q_gain, k_gain, bsz, seq, d_att):
    n = bsz * seq
    tq = ATT_TQ
    nqb = seq // tq
    npair = d_att // (2 * HEAD_DIM)
    w = 2 * HEAD_DIM

    def cur(col0):
        return pl.BlockSpec((tq, w), lambda hp, b, qb: (b * nqb + qb, col0 + hp))

    def prev(col0):
        return pl.BlockSpec((tq, w), lambda hp, b, qb: (b * nqb + jnp.maximum(qb - 1, 0), col0 + hp))

    return pl.pallas_call(
        _attn_kernel,
        out_shape=jax.ShapeDtypeStruct((n, d_att), BF16),
        grid=(npair, bsz, nqb),
        in_specs=[
            cur(0), prev(npair), cur(npair), prev(2 * npair), cur(2 * npair),
            pl.BlockSpec((2, tq, 2 * tq), lambda hp, b, qb: (hp, 0, 0)),
            pl.BlockSpec((1, HEAD_DIM), lambda hp, b, qb: (0, 0)),
            pl.BlockSpec((1, HEAD_DIM), lambda hp, b, qb: (0, 0)),
        ],
        out_specs=pl.BlockSpec((tq, w), lambda hp, b, qb: (b * nqb + qb, hp)),
        compiler_params=_cparams("parallel", "parallel", "arbitrary"),
        name="band_attention",
    )(proj, proj, proj, proj, proj, bias, q_gain.reshape(1, HEAD_DIM), k_gain.reshape(1, HEAD_DIM))


def _split3(x):
    hi = x.astype(BF16)
    r1 = x - hi.astype(F32)
    mid = r1.astype(BF16)
    lo = (r1 - mid.astype(F32)).astype(BF16)
    return hi, mid, lo


def _token_shift(z, carry_ref, mu):
    rows = lax.broadcasted_iota(jnp.int32, z.shape, 0)
    prev = jnp.where(rows == 0, carry_ref[...], pltpu.roll(z, 1, 0))
    carry_ref[...] = z[z.shape[0] - 1:, :]
    return z + (prev - z) * mu


def _rwkv_kernel(r_ref, k_ref, v_ref, l_ref, mur_ref, muk_ref, muv_ref, mul_ref,
                 w0_ref, a0_ref, kk_ref, ka_ref, rk_ref, lnw_ref, lnb_ref,
                 wup_ref, aup_ref, gup_ref, o_ref,
                 s_scr, cr_scr, ck_scr, cv_scr, cl_scr):
    first = pl.program_id(1) == 0
    L = r_ref.shape[0]
    d = r_ref.shape[1]
    nh = d // HEAD_DIM

    @pl.when(first)
    def _():
        s_scr[...] = jnp.zeros_like(s_scr)
        for c in (cr_scr, ck_scr, cv_scr, cl_scr):
            c[...] = jnp.zeros_like(c)

    r = _token_shift(r_ref[...], cr_scr, mur_ref[...])
    k = _token_shift(k_ref[...], ck_scr, muk_ref[...])
    v = _token_shift(v_ref[...], cv_scr, muv_ref[...])
    lo = _token_shift(l_ref[...], cl_scr, mul_ref[...])
    nw = wup_ref.shape[0]
    na = aup_ref.shape[0]
    xw, xa, xg = lo[:, :nw], lo[:, nw:nw + na], lo[:, nw + na:]

    wpre = w0_ref[...] + _bdot(jnp.tanh(xw), wup_ref[...])
    w_log = -(jnp.maximum(-wpre, 0.0) + jnp.log(1.0 + jnp.exp(-jnp.abs(wpre)))) - 0.5
    lw = -jnp.exp(w_log)
    a = _sigmoid(a0_ref[...] + _bdot(xa, aup_ref[...]))
    g = _bdot(_sigmoid(xg), gup_ref[...])

    kkf = k * kk_ref[...]
    k2 = k * (1.0 + (a - 1.0) * ka_ref[...])

    ti = lax.broadcasted_iota(jnp.int32, (L, L), 0)
    si = lax.broadcasted_iota(jnp.int32, (L, L), 1)
    incl = si <= ti
    strict = si < ti
    tri = jnp.where(incl, 1.0, 0.0).astype(BF16)
    h3 = _split3(lw)
    cum = _dot(tri, h3[0]) + _dot(tri, h3[1]) + _dot(tri, h3[2])
    e_pos = jnp.exp(cum)
    e_neg = jnp.exp(-cum)
    e_prev = jnp.exp(cum - lw)
    w_all = e_pos[L - 1:, :]

    rk = r * k2 * rk_ref[...]
    hs = range(nh)
    sls = [slice(h * HEAD_DIM, (h + 1) * HEAD_DIM) for h in hs]
    ti2 = lax.broadcasted_iota(jnp.int32, (L, 2 * L), 0)
    si2 = lax.broadcasted_iota(jnp.int32, (L, 2 * L), 1)
    incl2 = jnp.where(si2 < L, si2, si2 - L) <= ti2
    ar, bk, vb = [], [], []
    for sl in sls:
        kk = kkf[:, sl]
        nrm = jnp.sqrt(jnp.sum(kk * kk, axis=-1, keepdims=True))
        kk = kk / jnp.maximum(nrm, 1e-12)
        ah = -kk * e_prev[:, sl]
        bh = kk * a[:, sl] * e_neg[:, sl]
        kh = k2[:, sl] * e_neg[:, sl]
        rh = r[:, sl] * e_pos[:, sl]
        ar.append(jnp.concatenate([ah, rh], axis=0).astype(BF16))
        bk.append(jnp.concatenate([bh, kh], axis=0).astype(BF16))
        vb.append(v[:, sl].astype(BF16))
    s0 = [s_scr[h] for h in hs]
    gm = [_dot_nt(ar[h], bk[h]) for h in hs]
    ps = [_dot_nt(ar[h], s0[h].astype(BF16)) for h in hs]
    pw = [jnp.where(strict, gm[h][:L, :L], 0.0).astype(BF16) for h in hs]
    u = [ps[h][:L] + _dot(jnp.where(strict, gm[h][:L, L:], 0.0).astype(BF16), vb[h]) for h in hs]
    steps = max(1, (L - 1).bit_length())
    for i in range(steps):
        u = [u[h] + _dot(pw[h], u[h].astype(BF16)) for h in hs]
        if i + 1 < steps:
            pw = [_dot(pw[h], pw[h]).astype(BF16) for h in hs]
    uv = [jnp.concatenate([u[h].astype(BF16), vb[h]], axis=0) for h in hs]
    y = [ps[h][L:] + _dot(jnp.where(incl2, gm[h][L:], 0.0).astype(BF16), uv[h]) for h in hs]
    for h in hs:
        s_scr[h] = (s0[h] + _dot_tn(uv[h], bk[h])) * w_all[:, sls[h]]
    outs = []
    for h in hs:
        mean = jnp.mean(y[h], axis=-1, keepdims=True)
        yc = y[h] - mean
        var = jnp.mean(yc * yc, axis=-1, keepdims=True)
        yn = yc * lax.rsqrt(var + GN_EPS)
        bonus = jnp.sum(rk[:, sls[h]], axis=-1, keepdims=True) * v[:, sls[h]]
        outs.append((yn, bonus))
    yn = jnp.concatenate([o[0] for o in outs], axis=1)
    bonus = jnp.concatenate([o[1] for o in outs], axis=1)
    o_ref[...] = ((yn * lnw_ref[...] + lnb_ref[...] + bonus) * g).astype(o_ref.dtype)


def _rwkv(proj, col0, bsz, seq, d, mu, w0, w_up, a0, a_up, g_up, k_k, k_a, r_k, lnx_w, lnx_b):
    n = bsz * seq
    L = RWKV_L
    nt = seq // L
    nl = mu.shape[0] - 3 * d
    row = lambda x: x.reshape(1, -1)
    cb = col0 // d
    lb = (col0 + 3 * d) // nl

    def zspec(width, blk):
        return pl.BlockSpec((L, width), lambda b, t: (b * nt + t, blk))

    def pspec(shape):
        return pl.BlockSpec(shape, lambda b, t: (0,) * len(shape))

    nh = d // HEAD_DIM
    return pl.pallas_call(
        _rwkv_kernel,
        out_shape=jax.ShapeDtypeStruct((n, d), BF16),
        grid=(bsz, nt),
        in_specs=[
            zspec(d, cb), zspec(d, cb + 1), zspec(d, cb + 2), zspec(nl, lb),
            pspec((1, d)), pspec((1, d)), pspec((1, d)), pspec((1, nl)),
            pspec((1, d)), pspec((1, d)), pspec((1, d)), pspec((1, d)), pspec((1, d)),
            pspec((1, d)), pspec((1, d)),
            pspec(w_up.shape), pspec(a_up.shape), pspec(g_up.shape),
        ],
        out_specs=pl.BlockSpec((L, d), lambda b, t: (b * nt + t, 0)),
        scratch_shapes=[
            pltpu.VMEM((nh, HEAD_DIM, HEAD_DIM), F32),
            pltpu.VMEM((1, d), F32), pltpu.VMEM((1, d), F32), pltpu.VMEM((1, d), F32),
            pltpu.VMEM((1, nl), F32),
        ],
        compiler_params=_cparams("parallel", "arbitrary"),
        name="rwkv7",
    )(proj, proj, proj, proj,
      row(mu[:d]), row(mu[d:2 * d]), row(mu[2 * d:3 * d]), row(mu[3 * d:]),
      row(w0), row(a0), row(k_k), row(k_a), row(r_k), row(lnx_w), row(lnx_b),
      w_up.astype(BF16), a_up.astype(BF16), g_up.astype(BF16))


def _s5_prep_kernel(lr_ref, li_ref, ldt_ref, bre_ref, bim_ref, cre_ref, cim_ref,
                    crt_ref, cit_ref, m_ref, r_ref, q_ref):
    L = SSM_L
    gs = SSM_GROUP
    w = L * gs
    lr = lr_ref[0]
    li = li_ref[0]
    dt = jnp.exp(ldt_ref[0])
    mag = jnp.exp(lr * dt)
    ab_re = mag * jnp.cos(li * dt)
    ab_im = mag * jnp.sin(li * dt)
    den = lr * lr + li * li
    z_re = ((ab_re - 1.0) * lr + ab_im * li) / den
    z_im = (ab_im * lr - (ab_re - 1.0) * li) / den
    bre = bre_ref[0]
    bim = bim_ref[0]
    bb_re = z_re * bre - z_im * bim
    bb_im = z_re * bim + z_im * bre
    lane = lax.broadcasted_iota(jnp.int32, (1, w), 1)
    step = (lane // gs).astype(F32)

    def apow(nsteps):
        mg = jnp.exp(lr * dt * nsteps)
        ang = li * dt * nsteps
        return mg * jnp.cos(ang), mg * jnp.sin(ang)

    pr, pi = apow((L - 1.0) - step)
    rr = pr * bb_re - pi * bb_im
    ri = pr * bb_im + pi * bb_re
    rrev = jnp.concatenate([rr, ri], axis=0)
    r_ref[0] = rrev.astype(r_ref.dtype)
    qr, qi = apow(step + 1.0)
    crt = crt_ref[0]
    cit = cit_ref[0]
    q_ref[0] = jnp.concatenate([crt * qr - cit * qi, -(crt * qi + cit * qr)], axis=0).astype(q_ref.dtype)
    cri = jnp.concatenate([cre_ref[0], -cim_ref[0]], axis=1)
    kp = jnp.dot(cri, rrev, preferred_element_type=F32, precision=lax.Precision.HIGHEST)
    for t in range(L):
        dsh = (L - 1 - t) * gs
        strip = pltpu.roll(kp, (w - dsh) % w, 1) if dsh else kp
        strip = jnp.where(lane < (t + 1) * gs, strip, 0.0)
        m_ref[0, t * gs:(t + 1) * gs, :] = strip.astype(m_ref.dtype)


def _s5_prep(lam_re, lam_im, log_dt, b_re, b_im, c_re, c_im):
    G, P = lam_re.shape
    gs = SSM_GROUP
    L = SSM_L
    w = L * gs
    tile = lambda x: jnp.tile(x, (1, 1, L))
    col = lambda x: x.reshape(G, P, 1)

    def spec(shape):
        return pl.BlockSpec((1,) + shape, lambda g: (g, 0, 0))

    return pl.pallas_call(
        _s5_prep_kernel,
        out_shape=[jax.ShapeDtypeStruct((G, w, w), BF16),
                   jax.ShapeDtypeStruct((G, 2 * P, w), BF16),
                   jax.ShapeDtypeStruct((G, 2 * P, w), BF16)],
        grid=(G,),
        in_specs=[spec((P, 1)), spec((P, 1)), spec((1, 1)), spec((P, w)), spec((P, w)),
                  spec((gs, P)), spec((gs, P)), spec((P, w)), spec((P, w))],
        out_specs=[spec((w, w)), spec((2 * P, w)), spec((2 * P, w))],
        compiler_params=_cparams("parallel"),
        name="s5_prep",
    )(col(lam_re), col(lam_im), log_dt.reshape(G, 1, 1), tile(b_re), tile(b_im), c_re, c_im,
      tile(jnp.swapaxes(c_re, 1, 2)), tile(jnp.swapaxes(c_im, 1, 2)))


def _s5_kernel(u_ref, m_ref, r_ref, q_ref, lr_ref, li_ref, ldt_ref, d_ref, o_ref,
               gre_scr, gim_scr, hre_scr, him_scr):
    P = SSM_STATE
    nc = u_ref.shape[2]
    u = u_ref[0, 0]
    ub = u.astype(BF16)
    y = _dot_nt(ub, m_ref[0])
    gall = _dot_nt(ub, r_ref[0])
    gre_scr[...] = gall[:, :P]
    gim_scr[...] = gall[:, P:]
    dtl = jnp.exp(ldt_ref[0]) * SSM_L
    lr = lr_ref[0]
    li = li_ref[0]
    mg = jnp.exp(lr * dtl)
    al_re = mg * jnp.cos(li * dtl)
    al_im = mg * jnp.sin(li * dtl)

    def body(c, carry):
        hre, him = carry
        hre_scr[pl.ds(c, 1), :] = hre
        him_scr[pl.ds(c, 1), :] = him
        gr = gre_scr[pl.ds(c, 1), :]
        gi = gim_scr[pl.ds(c, 1), :]
        return al_re * hre - al_im * him + gr, al_re * him + al_im * hre + gi

    zero = jnp.zeros((1, P), F32)
    lax.fori_loop(0, nc, body, (zero, zero))
    hp = jnp.concatenate([hre_scr[...], him_scr[...]], axis=1).astype(BF16)
    y = y + _dot(hp, q_ref[0]) + d_ref[0] * u
    o_ref[0, 0] = (0.5 * y * (1.0 + jnp.tanh(math.sqrt(2.0 / math.pi) * (y + 0.044715 * y * y * y)))).astype(o_ref.dtype)


def _s5(u, bsz, seq, mats, lam_re, lam_im, log_dt, d_skip):
    m, r, q = mats
    G, P = lam_re.shape
    gs = SSM_GROUP
    L = SSM_L
    w = L * gs
    nc = seq // L
    ug = u.reshape(bsz, nc, L, G, gs).transpose(3, 0, 1, 2, 4).reshape(G, bsz, nc, w)
    d_t = jnp.tile(d_skip.reshape(G, 1, gs), (1, 1, L))

    def gspec(shape):
        return pl.BlockSpec((1,) + shape, lambda g, b: (g, 0, 0))

    yg = pl.pallas_call(
        _s5_kernel,
        out_shape=jax.ShapeDtypeStruct((G, bsz, nc, w), BF16),
        grid=(G, bsz),
        in_specs=[
            pl.BlockSpec((1, 1, nc, w), lambda g, b: (g, b, 0, 0)),
            gspec((w, w)), gspec((2 * P, w)), gspec((2 * P, w)),
            gspec((1, P)), gspec((1, P)), gspec((1, 1)), gspec((1, w)),
        ],
        out_specs=pl.BlockSpec((1, 1, nc, w), lambda g, b: (g, b, 0, 0)),
        scratch_shapes=[pltpu.VMEM((nc, P), F32)] * 4,
        compiler_params=_cparams("parallel", "arbitrary"),
        name="s5_ssm",
    )(ug, m, r, q, lam_re.reshape(G, 1, P), lam_im.reshape(G, 1, P), log_dt.reshape(G, 1, 1), d_t)
    return yg.reshape(G, bsz, nc, L, gs).transpose(1, 2, 3, 0, 4).reshape(bsz * seq, G * gs)


def kernel(x, p, ffn1_norm, ffn1_w_gate, ffn1_w_up, ffn1_w_down, mix_norm, ffn2_norm, ffn2_w_gate, ffn2_w_up, ffn2_w_down, ple_norm, ple_w_gate, ple_w_proj, ab_w_in, att_q_gain, att_k_gain, att_rel_bias, rwkv_mu, rwkv_w0, rwkv_w_up, rwkv_a0, rwkv_a_up, rwkv_g_up, rwkv_k_k, rwkv_k_a, rwkv_r_k, rwkv_lnx_w, rwkv_lnx_b, ab_w_out, ssm_w_in, ssm_lambda_re, ssm_lambda_im, ssm_log_dt, ssm_b_re, ssm_b_im, ssm_c_re, ssm_c_im, ssm_d, ssm_w_out):
    bsz, seq, d = x.shape
    depth = p.shape[0]
    n = bsz * seq
    bf = lambda w: w.astype(BF16)
    h = x.reshape(n, d)
    for i in range(depth):
        j = i // 2
        h = _ffn(h, ffn1_norm[i], bf(ffn1_w_gate[i]), bf(ffn1_w_up[i]), bf(ffn1_w_down[i]))
        if i % 2 == 0:
            d_att = att_rel_bias.shape[1] * HEAD_DIM
            d_rw = rwkv_w0.shape[1]
            n_in = ab_w_in.shape[2]
            tn = 640 if n_in % 640 == 0 else 128
            proj = _norm_matmul(h, mix_norm[i], bf(ab_w_in[j]), 2 * ROW_TILE, tn)
            bias = _bias_tiles(att_rel_bias[j])
            att = _attention(proj, bias, att_q_gain[j], att_k_gain[j], bsz, seq, d_att)
            rw = _rwkv(proj, 3 * d_att, bsz, seq, d_rw, rwkv_mu[j], rwkv_w0[j], rwkv_w_up[j],
                       rwkv_a0[j], rwkv_a_up[j], rwkv_g_up[j], rwkv_k_k[j], rwkv_k_a[j],
                       rwkv_r_k[j].reshape(-1), rwkv_lnx_w[j], rwkv_lnx_b[j])
            h = _out2(h, att, rw, bf(ab_w_out[j]))
        else:
            d_ssm = ssm_w_in.shape[2]
            u = _norm_matmul(h, mix_norm[i], bf(ssm_w_in[j]), ROW_TILE, d_ssm)
            mats = _s5_prep(ssm_lambda_re[j], ssm_lambda_im[j], ssm_log_dt[j], ssm_b_re[j],
                            ssm_b_im[j], ssm_c_re[j], ssm_c_im[j])
            y = _s5(u, bsz, seq, mats, ssm_lambda_re[j], ssm_lambda_im[j], ssm_log_dt[j], ssm_d[j])
            h = _glu_out(h, y, bf(ssm_w_out[j]))
        h = _ffn(h, ffn2_norm[i], bf(ffn2_w_gate[i]), bf(ffn2_w_up[i]), bf(ffn2_w_down[i]))
        h = _ple(h, ple_norm[i], p[i].reshape(n, -1), bf(ple_w_gate[i]), bf(ple_w_proj[i]))
    return h.reshape(bsz, seq, d)
```

```python
import functools
import math

import jax
import jax.numpy as jnp
from jax import lax
from jax.experimental import pallas as pl
from jax.experimental.pallas import tpu as pltpu

F32 = jnp.float32
BF16 = jnp.bfloat16

RMS_EPS = 1e-6
GN_EPS = 64e-5
CHUNK = 64
N_LEFT_CHUNKS = 8
REL_CLIP = 128
HEAD_DIM = 64
ATT_LEFT = N_LEFT_CHUNKS * CHUNK
ATT_TQ = ATT_LEFT
ATT_SUB = 128
RWKV_L = 64
SSM_GROUP = 16
SSM_STATE = 64
SSM_L = 16
NEG = -1e30

ROW_TILE = 512
COL_TILE = 512
VMEM_LIMIT = 52 * 2 ** 20


def _cparams(*sem):
    return pltpu.CompilerParams(dimension_semantics=sem, vmem_limit_bytes=VMEM_LIMIT)


def _dot(a, b):
    return jnp.dot(a, b, preferred_element_type=F32)


def _dot_nt(a, b):
    return lax.dot_general(a, b, (((1,), (1,)), ((), ())), preferred_element_type=F32)


def _dot_tn(a, b):
    return lax.dot_general(a, b, (((0,), (0,)), ((), ())), preferred_element_type=F32)


def _bdot(a, b):
    return _dot(a.astype(BF16), b.astype(BF16))


def _bdot_nt(a, b):
    return _dot_nt(a.astype(BF16), b.astype(BF16))


def _bdot_tn(a, b):
    return _dot_tn(a.astype(BF16), b.astype(BF16))


def _rms(x, g):
    return x * lax.rsqrt(jnp.mean(x * x, axis=-1, keepdims=True) + RMS_EPS) * g


def _sigmoid(x):
    return 1.0 / (1.0 + jnp.exp(-x))


def _ffn_kernel(h_ref, g_ref, wg_ref, wu_ref, wd_ref, o_ref, n_scr, acc_scr):
    f = pl.program_id(1)

    @pl.when(f == 0)
    def _():
        n_scr[...] = _rms(h_ref[...], g_ref[...]).astype(BF16)
        acc_scr[...] = jnp.zeros_like(acc_scr)

    n = n_scr[...]
    gate = _dot(n, wg_ref[...])
    up = _dot(n, wu_ref[...])
    act = (gate * _sigmoid(gate) * up).astype(BF16)
    acc_scr[...] += _dot(act, wd_ref[...])

    @pl.when(f == pl.num_programs(1) - 1)
    def _():
        o_ref[...] = h_ref[...] + 0.5 * acc_scr[...]


def _ffn(h, g, wg, wu, wd):
    n, d = h.shape
    dff = wg.shape[1]
    tm = min(ROW_TILE, n)
    tf = min(COL_TILE, dff)
    return pl.pallas_call(
        _ffn_kernel,
        out_shape=jax.ShapeDtypeStruct((n, d), F32),
        grid=(n // tm, dff // tf),
        in_specs=[
            pl.BlockSpec((tm, d), lambda i, f: (i, 0)),
            pl.BlockSpec((1, d), lambda i, f: (0, 0)),
            pl.BlockSpec((d, tf), lambda i, f: (0, f)),
            pl.BlockSpec((d, tf), lambda i, f: (0, f)),
            pl.BlockSpec((tf, d), lambda i, f: (f, 0)),
        ],
        out_specs=pl.BlockSpec((tm, d), lambda i, f: (i, 0)),
        scratch_shapes=[pltpu.VMEM((tm, d), BF16), pltpu.VMEM((tm, d), F32)],
        compiler_params=_cparams("parallel", "arbitrary"),
        name="ffn",
    )(h, g.reshape(1, d), wg, wu, wd)


def _nmm_kernel(x_ref, g_ref, w_ref, o_ref, n_scr):
    @pl.when(pl.program_id(1) == 0)
    def _():
        n_scr[...] = _rms(x_ref[...], g_ref[...]).astype(BF16)

    o_ref[...] = _dot(n_scr[...], w_ref[...]).astype(o_ref.dtype)


def _norm_matmul(x, g, w, tm, tn):
    n, d = x.shape
    nout = w.shape[1]
    tm = min(tm, n)
    return pl.pallas_call(
        _nmm_kernel,
        out_shape=jax.ShapeDtypeStruct((n, nout), F32),
        grid=(n // tm, nout // tn),
        in_specs=[
            pl.BlockSpec((tm, d), lambda i, j: (i, 0)),
            pl.BlockSpec((1, d), lambda i, j: (0, 0)),
            pl.BlockSpec((d, tn), lambda i, j: (0, j)),
        ],
        out_specs=pl.BlockSpec((tm, tn), lambda i, j: (i, j)),
        scratch_shapes=[pltpu.VMEM((tm, d), BF16)],
        compiler_params=_cparams("parallel", "arbitrary"),
        name="norm_matmul",
    )(x, g.reshape(1, d), w)


def _out2_kernel(res_ref, a1_ref, a2_ref, w1_ref, w2_ref, o_ref):
    o_ref[...] = res_ref[...] + _dot(a1_ref[...], w1_ref[...]) + _dot(a2_ref[...], w2_ref[...])


def _out2(res, a1, a2, w):
    n, d = res.shape
    k1 = a1.shape[1]
    tm = min(ROW_TILE, n)
    return pl.pallas_call(
        _out2_kernel,
        out_shape=jax.ShapeDtypeStruct((n, d), F32),
        grid=(n // tm,),
        in_specs=[
            pl.BlockSpec((tm, d), lambda i: (i, 0)),
            pl.BlockSpec((tm, k1), lambda i: (i, 0)),
            pl.BlockSpec((tm, k1), lambda i: (i, 0)),
            pl.BlockSpec((k1, d), lambda i: (0, 0)),
            pl.BlockSpec((k1, d), lambda i: (1, 0)),
        ],
        out_specs=pl.BlockSpec((tm, d), lambda i: (i, 0)),
        compiler_params=_cparams("parallel"),
        name="mixer_out",
    )(res, a1, a2, w, w)


def _glu_kernel(res_ref, a_ref, wa_ref, wb_ref, o_ref):
    a = a_ref[...]
    za = _dot(a, wa_ref[...])
    zb = _dot(a, wb_ref[...])
    o_ref[...] = res_ref[...] + za * _sigmoid(zb)


def _glu_out(res, a, w):
    n, d = res.shape
    k = a.shape[1]
    tm = min(ROW_TILE, n)
    return pl.pallas_call(
        _glu_kernel,
        out_shape=jax.ShapeDtypeStruct((n, d), F32),
        grid=(n // tm,),
        in_specs=[
            pl.BlockSpec((tm, d), lambda i: (i, 0)),
            pl.BlockSpec((tm, k), lambda i: (i, 0)),
            pl.BlockSpec((k, d), lambda i: (0, 0)),
            pl.BlockSpec((k, d), lambda i: (0, 1)),
        ],
        out_specs=pl.BlockSpec((tm, d), lambda i: (i, 0)),
        compiler_params=_cparams("parallel"),
        name="glu_out",
    )(res, a, w, w)


def _ple_kernel(h_ref, g_ref, p_ref, wg_ref, wp_ref, o_ref):
    h = h_ref[...]
    gate = _sigmoid(_dot(_rms(h, g_ref[...]).astype(BF16), wg_ref[...]))
    proj = _dot(p_ref[...].astype(BF16), wp_ref[...])
    o_ref[...] = h + gate * proj


def _ple(h, g, p, wg, wp):
    n, d = h.shape
    dp = p.shape[1]
    tm = min(ROW_TILE, n)
    return pl.pallas_call(
        _ple_kernel,
        out_shape=jax.ShapeDtypeStruct((n, d), F32),
        grid=(n // tm,),
        in_specs=[
            pl.BlockSpec((tm, d), lambda i: (i, 0)),
            pl.BlockSpec((1, d), lambda i: (0, 0)),
            pl.BlockSpec((tm, dp), lambda i: (i, 0)),
            pl.BlockSpec((d, d), lambda i: (0, 0)),
            pl.BlockSpec((dp, d), lambda i: (0, 0)),
        ],
        out_specs=pl.BlockSpec((tm, d), lambda i: (i, 0)),
        compiler_params=_cparams("parallel"),
        name="ple",
    )(h, g.reshape(1, d), p, wg, wp)


def _bias_kernel(tbl_ref, o_ref):
    tq = o_ref.shape[1]
    tk = o_ref.shape[2]
    n_rel = tbl_ref.shape[2]
    width = pl.next_power_of_2(tq + tk)
    n_idx = lax.broadcasted_iota(jnp.int32, (n_rel, width), 1)
    c_idx = lax.broadcasted_iota(jnp.int32, (n_rel, width), 0)
    m = jnp.where(n_idx < tk, n_idx, n_idx - width)
    idx = jnp.clip(ATT_LEFT - m, -(CHUNK - 1), REL_CLIP) + (CHUNK - 1)
    onehot = (c_idx == idx).astype(F32)
    tbl = jnp.broadcast_to(tbl_ref[0], (8, n_rel))
    ext = jnp.dot(tbl, onehot, preferred_element_type=F32, precision=lax.Precision.HIGHEST)[0:1]
    x = jnp.broadcast_to(ext, (tq, width))
    rolled = pltpu.roll(x, 0, 1, stride=1, stride_axis=0)[:, :tk]
    qc = lax.broadcasted_iota(jnp.int32, (tq, tk), 0) // CHUNK
    kc = lax.broadcasted_iota(jnp.int32, (tq, tk), 1) // CHUNK
    ok = (kc >= qc) & (kc <= qc + N_LEFT_CHUNKS)
    o_ref[0] = jnp.where(ok, rolled, NEG)


def _bias_tiles(rel_bias):
    nh, n_rel = rel_bias.shape
    tk = ATT_SUB + ATT_LEFT
    return pl.pallas_call(
        _bias_kernel,
        out_shape=jax.ShapeDtypeStruct((nh, ATT_SUB, tk), F32),
        grid=(nh,),
        in_specs=[pl.BlockSpec((1, 1, n_rel), lambda h: (h, 0, 0))],
        out_specs=pl.BlockSpec((1, ATT_SUB, tk), lambda h: (h, 0, 0)),
        compiler_params=_cparams("parallel"),
        name="rel_bias_tiles",
    )(rel_bias.reshape(nh, 1, n_rel))


def _attn_kernel(q_ref, kp_ref, kc_ref, vp_ref, vc_ref, bias_ref, qg_ref, kg_ref, o_ref):
    has_prev = pl.program_id(2) > 0
    tq, w = q_ref.shape
    tk = bias_ref.shape[2]
    col = lax.broadcasted_iota(jnp.int32, (ATT_SUB, tk), 1)
    head_of_lane = lax.broadcasted_iota(jnp.int32, (1, w), 1) // HEAD_DIM
    seg = (lax.broadcasted_iota(jnp.int32, (w, w), 0) // HEAD_DIM
           == lax.broadcasted_iota(jnp.int32, (w, w), 1) // HEAD_DIM)
    seg = jnp.where(seg, 1.0, 0.0).astype(BF16)

    def rms2(x, gain):
        sq = x * x
        hi = sq.astype(BF16)
        lo = (sq - hi.astype(F32)).astype(BF16)
        ss = _dot(hi, seg) + _dot(lo, seg)
        return x * lax.rsqrt(ss * (1.0 / HEAD_DIM) + RMS_EPS) * gain

    q = rms2(q_ref[...], qg_ref[...] * (HEAD_DIM ** -0.5))
    k = jnp.concatenate([rms2(kp_ref[...], kg_ref[...]), rms2(kc_ref[...], kg_ref[...])], axis=0).astype(BF16)
    v = jnp.concatenate([vp_ref[...], vc_ref[...]], axis=0).astype(BF16)
    nsub = tq // ATT_SUB
    heads = range(w // HEAD_DIM)
    qh = [jnp.where(head_of_lane == hh, q, 0.0).astype(BF16) for hh in heads]
    jobs = [(hh, i * ATT_SUB) for hh in heads for i in range(nsub)]
    sc = [_dot_nt(qh[hh][r0:r0 + ATT_SUB], k[r0:r0 + tk]) + bias_ref[hh] for hh, r0 in jobs]
    sc = [jnp.where(has_prev | (col >= ATT_LEFT - r0), s, NEG) if r0 < ATT_LEFT else s
          for s, (hh, r0) in zip(sc, jobs)]
    pr = [jnp.exp(s - jnp.max(s, axis=-1, keepdims=True)) for s in sc]
    den = [jnp.sum(p, axis=-1, keepdims=True) for p in pr]
    ob = [_dot(p.astype(BF16), v[r0:r0 + tk]) / d for p, d, (hh, r0) in zip(pr, den, jobs)]
    outs = [jnp.concatenate(ob[hh * nsub:(hh + 1) * nsub], axis=0) for hh in heads]
    o = outs[0]
    for hh in range(1, len(outs)):
        o = jnp.where(head_of_lane == hh, outs[hh], o)
    o_ref[...] = o.astype(o_ref.dtype)


def _attention(proj, bias, q_gain, k_gain, bsz, seq, d_att):
    n = bsz * seq
    tq = ATT_TQ
    nqb = seq // tq
    npair = d_att // (2 * HEAD_DIM)
    w = 2 * HEAD_DIM

    def cur(col0):
        return pl.BlockSpec((tq, w), lambda hp, b, qb: (b * nqb + qb, col0 + hp))

    def prev(col0):
        return pl.BlockSpec((tq, w), lambda hp, b, qb: (b * nqb + jnp.maximum(qb - 1, 0), col0 + hp))

    return pl.pallas_call(
        _attn_kernel,
        out_shape=jax.ShapeDtypeStruct((n, d_att), BF16),
        grid=(npair, bsz, nqb),
        in_specs=[
            cur(0), prev(npair), cur(npair), prev(2 * npair), cur(2 * npair),
            pl.BlockSpec((2, ATT_SUB, ATT_SUB + ATT_LEFT), lambda hp, b, qb: (hp, 0, 0)),
            pl.BlockSpec((1, w), lambda hp, b, qb: (0, 0)),
            pl.BlockSpec((1, w), lambda hp, b, qb: (0, 0)),
        ],
        out_specs=pl.BlockSpec((tq, w), lambda hp, b, qb: (b * nqb + qb, hp)),
        compiler_params=_cparams("parallel", "parallel", "arbitrary"),
        name="band_attention",
    )(proj, proj, proj, proj, proj, bias, jnp.tile(q_gain, 2).reshape(1, w), jnp.tile(k_gain, 2).reshape(1, w))


def _split3(x):
    hi = x.astype(BF16)
    r1 = x - hi.astype(F32)
    mid = r1.astype(BF16)
    lo = (r1 - mid.astype(F32)).astype(BF16)
    return hi, mid, lo


def _token_shift(z, carry_ref, mu):
    rows = lax.broadcasted_iota(jnp.int32, z.shape, 0)
    prev = jnp.where(rows == 0, carry_ref[...], pltpu.roll(z, 1, 0))
    carry_ref[...] = z[z.shape[0] - 1:, :]
    return z + (prev - z) * mu


def _rwkv_kernel(r_ref, k_ref, v_ref, l_ref, mur_ref, muk_ref, muv_ref, mul_ref,
                 w0_ref, a0_ref, kk_ref, ka_ref, rk_ref, lnw_ref, lnb_ref,
                 wup_ref, aup_ref, gup_ref, o_ref,
                 s_scr, cr_scr, ck_scr, cv_scr, cl_scr):
    first = pl.program_id(1) == 0
    L = r_ref.shape[0]
    d = r_ref.shape[1]
    nh = d // HEAD_DIM

    @pl.when(first)
    def _():
        s_scr[...] = jnp.zeros_like(s_scr)
        for c in (cr_scr, ck_scr, cv_scr, cl_scr):
            c[...] = jnp.zeros_like(c)

    r = _token_shift(r_ref[...], cr_scr, mur_ref[...])
    k = _token_shift(k_ref[...], ck_scr, muk_ref[...])
    v = _token_shift(v_ref[...], cv_scr, muv_ref[...])
    lo = _token_shift(l_ref[...], cl_scr, mul_ref[...])
    nw = wup_ref.shape[0]
    na = aup_ref.shape[0]
    xw, xa, xg = lo[:, :nw], lo[:, nw:nw + na], lo[:, nw + na:]

    wpre = w0_ref[...] + _bdot(jnp.tanh(xw), wup_ref[...])
    w_log = -(jnp.maximum(-wpre, 0.0) + jnp.log(1.0 + jnp.exp(-jnp.abs(wpre)))) - 0.5
    lw = -jnp.exp(w_log)
    a = _sigmoid(a0_ref[...] + _bdot(xa, aup_ref[...]))
    g = _bdot(_sigmoid(xg), gup_ref[...])

    kkf = k * kk_ref[...]
    k2 = k * (1.0 + (a - 1.0) * ka_ref[...])

    ti = lax.broadcasted_iota(jnp.int32, (L, L), 0)
    si = lax.broadcasted_iota(jnp.int32, (L, L), 1)
    incl = si <= ti
    strict = si < ti
    tri = jnp.where(incl, 1.0, 0.0).astype(BF16)
    h3 = _split3(lw)
    cum = _dot(tri, h3[0]) + _dot(tri, h3[1]) + _dot(tri, h3[2])
    e_pos = jnp.exp(cum)
    e_neg = jnp.exp(-cum)
    e_prev = jnp.exp(cum - lw)
    w_all = e_pos[L - 1:, :]

    rk = r * k2 * rk_ref[...]
    hs = range(nh)
    sls = [slice(h * HEAD_DIM, (h + 1) * HEAD_DIM) for h in hs]
    ti2 = lax.broadcasted_iota(jnp.int32, (L, 2 * L), 0)
    si2 = lax.broadcasted_iota(jnp.int32, (L, 2 * L), 1)
    incl2 = jnp.where(si2 < L, si2, si2 - L) <= ti2
    ar, bk, vb = [], [], []
    for sl in sls:
        kk = kkf[:, sl]
        nrm = jnp.sqrt(jnp.sum(kk * kk, axis=-1, keepdims=True))
        kk = kk / jnp.maximum(nrm, 1e-12)
        ah = -kk * e_prev[:, sl]
        bh = kk * a[:, sl] * e_neg[:, sl]
        kh = k2[:, sl] * e_neg[:, sl]
        rh = r[:, sl] * e_pos[:, sl]
        ar.append(jnp.concatenate([ah, rh], axis=0).astype(BF16))
        bk.append(jnp.concatenate([bh, kh], axis=0).astype(BF16))
        vb.append(v[:, sl].astype(BF16))
    s0 = [s_scr[h] for h in hs]
    gm = [_dot_nt(ar[h], bk[h]) for h in hs]
    ps = [_dot_nt(ar[h], s0[h].astype(BF16)) for h in hs]
    pw = [jnp.where(strict, gm[h][:L, :L], 0.0).astype(BF16) for h in hs]
    u = [ps[h][:L] + _dot(jnp.where(strict, gm[h][:L, L:], 0.0).astype(BF16), vb[h]) for h in hs]
    steps = max(1, (L - 1).bit_length())
    for i in range(steps):
        u = [u[h] + _dot(pw[h], u[h].astype(BF16)) for h in hs]
        if i + 1 < steps:
            pw = [_dot(pw[h], pw[h]).astype(BF16) for h in hs]
    uv = [jnp.concatenate([u[h].astype(BF16), vb[h]], axis=0) for h in hs]
    y = [ps[h][L:] + _dot(jnp.where(incl2, gm[h][L:], 0.0).astype(BF16), uv[h]) for h in hs]
    for h in hs:
        s_scr[h] = (s0[h] + _dot_tn(uv[h], bk[h])) * w_all[:, sls[h]]
    outs = []
    for h in hs:
        mean = jnp.mean(y[h], axis=-1, keepdims=True)
        yc = y[h] - mean
        var = jnp.mean(yc * yc, axis=-1, keepdims=True)
        yn = yc * lax.rsqrt(var + GN_EPS)
        bonus = jnp.sum(rk[:, sls[h]], axis=-1, keepdims=True) * v[:, sls[h]]
        outs.append((yn, bonus))
    yn = jnp.concatenate([o[0] for o in outs], axis=1)
    bonus = jnp.concatenate([o[1] for o in outs], axis=1)
    o_ref[...] = ((yn * lnw_ref[...] + lnb_ref[...] + bonus) * g).astype(o_ref.dtype)


def _rwkv(proj, col0, bsz, seq, d, mu, w0, w_up, a0, a_up, g_up, k_k, k_a, r_k, lnx_w, lnx_b):
    n = bsz * seq
    L = RWKV_L
    nt = seq // L
    nl = mu.shape[0] - 3 * d
    row = lambda x: x.reshape(1, -1)
    cb = col0 // d
    lb = (col0 + 3 * d) // nl

    def zspec(width, blk):
        return pl.BlockSpec((L, width), lambda b, t: (b * nt + t, blk))

    def pspec(shape):
        return pl.BlockSpec(shape, lambda b, t: (0,) * len(shape))

    nh = d // HEAD_DIM
    return pl.pallas_call(
        _rwkv_kernel,
        out_shape=jax.ShapeDtypeStruct((n, d), BF16),
        grid=(bsz, nt),
        in_specs=[
            zspec(d, cb), zspec(d, cb + 1), zspec(d, cb + 2), zspec(nl, lb),
            pspec((1, d)), pspec((1, d)), pspec((1, d)), pspec((1, nl)),
            pspec((1, d)), pspec((1, d)), pspec((1, d)), pspec((1, d)), pspec((1, d)),
            pspec((1, d)), pspec((1, d)),
            pspec(w_up.shape), pspec(a_up.shape), pspec(g_up.shape),
        ],
        out_specs=pl.BlockSpec((L, d), lambda b, t: (b * nt + t, 0)),
        scratch_shapes=[
            pltpu.VMEM((nh, HEAD_DIM, HEAD_DIM), F32),
            pltpu.VMEM((1, d), F32), pltpu.VMEM((1, d), F32), pltpu.VMEM((1, d), F32),
            pltpu.VMEM((1, nl), F32),
        ],
        compiler_params=_cparams("parallel", "arbitrary"),
        name="rwkv7",
    )(proj, proj, proj, proj,
      row(mu[:d]), row(mu[d:2 * d]), row(mu[2 * d:3 * d]), row(mu[3 * d:]),
      row(w0), row(a0), row(k_k), row(k_a), row(r_k), row(lnx_w), row(lnx_b),
      w_up.astype(BF16), a_up.astype(BF16), g_up.astype(BF16))


SSM_GB = 8


def _cmul(ar, ai, br, bi):
    return ar * br - ai * bi, ar * bi + ai * br


def _s5_abar(lr, li, ldt):
    dt = jnp.exp(ldt)
    mag = jnp.exp(lr * dt)
    return mag * jnp.cos(li * dt), mag * jnp.sin(li * dt)


def _s5_prep_kernel(lrr_ref, lir_ref, ldtr_ref, lrc_ref, lic_ref, ldtc_ref,
                    bre_ref, bim_ref, cre_ref, cim_ref, k_ref, p_ref, q_ref):
    L = SSM_L
    w = k_ref.shape[2]
    lr, li = lrr_ref[0], lir_ref[0]
    a_re, a_im = _s5_abar(lr, li, ldtr_ref[0])
    den = lr * lr + li * li
    z_re = ((a_re - 1.0) * lr + a_im * li) / den
    z_im = (a_im * lr - (a_re - 1.0) * li) / den
    bb_re, bb_im = _cmul(z_re, z_im, bre_ref[0], bim_ref[0])
    cre, cim = cre_ref[0], cim_ref[0]
    ccat = jnp.concatenate([cre, -cim], axis=0)
    ac_re, ac_im = _s5_abar(lrc_ref[0], lic_ref[0], ldtc_ref[0])
    pr, pi = jnp.ones_like(a_re), jnp.zeros_like(a_im)
    qr, qi = ac_re, ac_im
    for tau in range(L):
        rb_re, rb_im = _cmul(pr, pi, bb_re, bb_im)
        ptau = jnp.concatenate([rb_re, rb_im], axis=1)
        p_ref[0, L - 1 - tau] = ptau.astype(p_ref.dtype)
        ktau = jnp.dot(ptau, ccat, preferred_element_type=F32, precision=lax.Precision.HIGHEST)
        k_ref[0, (L - 1 - tau) * w:(L - tau) * w, :] = ktau.astype(k_ref.dtype)
        q_ref[0, tau] = jnp.concatenate([cre * qr - cim * qi, -(cre * qi + cim * qr)], axis=0).astype(q_ref.dtype)
        pr, pi = _cmul(pr, pi, a_re, a_im)
        qr, qi = _cmul(qr, qi, ac_re, ac_im)


def _s5_prep(lam_re, lam_im, log_dt, b_re, b_im, c_re, c_im):
    G, P = lam_re.shape
    gs = SSM_GROUP
    gb = SSM_GB
    J = G // gb
    L = SSM_L
    eye = jnp.eye(gb, dtype=F32)
    ldt = jnp.repeat(log_dt, P)

    def bdiag_b(b):
        bt = jnp.swapaxes(b, 1, 2).reshape(J, gb, gs, P)
        return (bt[:, :, :, None, :] * eye[None, :, None, :, None]).reshape(J, gb * gs, gb * P)

    def bdiag_c(c):
        ct = jnp.swapaxes(c, 1, 2).reshape(J, gb, P, gs)
        return (ct[:, :, :, None, :] * eye[None, :, None, :, None]).reshape(J, gb * P, gb * gs)

    row = lambda x: x.reshape(J, 1, gb * P)
    col = lambda x: x.reshape(J, gb * P, 1)

    def spec(*shape):
        return pl.BlockSpec((1,) + shape, lambda j: (j,) + (0,) * len(shape))

    w, sw = gb * gs, gb * P
    return pl.pallas_call(
        _s5_prep_kernel,
        out_shape=[jax.ShapeDtypeStruct((J, L * w, w), BF16),
                   jax.ShapeDtypeStruct((J, L, w, 2 * sw), BF16),
                   jax.ShapeDtypeStruct((J, L, 2 * sw, w), BF16)],
        grid=(J,),
        in_specs=[spec(1, sw), spec(1, sw), spec(1, sw), spec(sw, 1), spec(sw, 1), spec(sw, 1),
                  spec(w, sw), spec(w, sw), spec(sw, w), spec(sw, w)],
        out_specs=[spec(L * w, w), spec(L, w, 2 * sw), spec(L, 2 * sw, w)],
        compiler_params=_cparams("parallel"),
        name="s5_prep",
    )(row(lam_re), row(lam_im), row(ldt), col(lam_re), col(lam_im), col(ldt),
      bdiag_b(b_re), bdiag_b(b_im), bdiag_c(c_re), bdiag_c(c_im))


def _s5_kernel(u_ref, k_ref, p_ref, q_ref, lr_ref, li_ref, ldt_ref, d_ref, o_ref,
               gre_scr, gim_scr, hre_scr, him_scr, y_scr):
    L = SSM_L
    nc = u_ref.shape[0] // L
    sw = lr_ref.shape[2]
    w = u_ref.shape[1]
    us = [u_ref[pl.ds(s, nc, stride=L), :].astype(BF16) for s in range(L)]
    gall = _dot(us[0], p_ref[0, 0])
    for s in range(1, L):
        gall = gall + _dot(us[s], p_ref[0, s])
    gre_scr[...] = gall[:, :sw]
    gim_scr[...] = gall[:, sw:]
    a_re, a_im = _s5_abar(lr_ref[0], li_ref[0], ldt_ref[0])
    al_re, al_im = a_re, a_im
    for _ in range(L.bit_length() - 1):
        al_re, al_im = _cmul(al_re, al_im, al_re, al_im)

    def body(c, carry):
        hre, him = carry
        hre_scr[pl.ds(c, 1), :] = hre
        him_scr[pl.ds(c, 1), :] = him
        gr = gre_scr[pl.ds(c, 1), :]
        gi = gim_scr[pl.ds(c, 1), :]
        return al_re * hre - al_im * him + gr, al_re * him + al_im * hre + gi

    zero = jnp.zeros((1, sw), F32)
    lax.fori_loop(0, nc, body, (zero, zero))
    hp = jnp.concatenate([hre_scr[...], him_scr[...]], axis=1).astype(BF16)
    ucat = jnp.concatenate(us, axis=1)
    kall = k_ref[0]
    for t in range(L):
        yt = _dot(hp, q_ref[0, t]) + _dot(ucat[:, :(t + 1) * w], kall[(L - 1 - t) * w:, :])
        y_scr[pl.ds(t, nc, stride=L), :] = yt
    y = y_scr[...] + d_ref[0] * u_ref[...]
    o_ref[...] = (0.5 * y * (1.0 + jnp.tanh(math.sqrt(2.0 / math.pi) * (y + 0.044715 * y * y * y)))).astype(o_ref.dtype)


def _s5(u, bsz, seq, mats, lam_re, lam_im, log_dt, d_skip):
    kb, pb, qb = mats
    G, P = lam_re.shape
    gb = SSM_GB
    J = G // gb
    L = SSM_L
    w = gb * SSM_GROUP
    sw = gb * P
    nc = seq // L
    row = lambda x: x.reshape(J, 1, sw)

    def jspec(*shape):
        return pl.BlockSpec((1,) + shape, lambda j, b: (j,) + (0,) * len(shape))

    return pl.pallas_call(
        _s5_kernel,
        out_shape=jax.ShapeDtypeStruct((bsz * seq, G * SSM_GROUP), BF16),
        grid=(J, bsz),
        in_specs=[
            pl.BlockSpec((seq, w), lambda j, b: (b, j)),
            jspec(L * w, w), jspec(L, w, 2 * sw), jspec(L, 2 * sw, w),
            jspec(1, sw), jspec(1, sw), jspec(1, sw), jspec(1, w),
        ],
        out_specs=pl.BlockSpec((seq, w), lambda j, b: (b, j)),
        scratch_shapes=[pltpu.VMEM((nc, sw), F32)] * 4 + [pltpu.VMEM((seq, w), F32)],
        compiler_params=_cparams("parallel", "arbitrary"),
        name="s5_ssm",
    )(u, kb, pb, qb, row(lam_re), row(lam_im), row(jnp.repeat(log_dt, P)), d_skip.reshape(J, 1, w))


def kernel(x, p, ffn1_norm, ffn1_w_gate, ffn1_w_up, ffn1_w_down, mix_norm, ffn2_norm, ffn2_w_gate, ffn2_w_up, ffn2_w_down, ple_norm, ple_w_gate, ple_w_proj, ab_w_in, att_q_gain, att_k_gain, att_rel_bias, rwkv_mu, rwkv_w0, rwkv_w_up, rwkv_a0, rwkv_a_up, rwkv_g_up, rwkv_k_k, rwkv_k_a, rwkv_r_k, rwkv_lnx_w, rwkv_lnx_b, ab_w_out, ssm_w_in, ssm_lambda_re, ssm_lambda_im, ssm_log_dt, ssm_b_re, ssm_b_im, ssm_c_re, ssm_c_im, ssm_d, ssm_w_out):
    bsz, seq, d = x.shape
    depth = p.shape[0]
    n = bsz * seq
    bf = lambda w: w.astype(BF16)
    h = x.reshape(n, d)
    for i in range(depth):
        j = i // 2
        h = _ffn(h, ffn1_norm[i], bf(ffn1_w_gate[i]), bf(ffn1_w_up[i]), bf(ffn1_w_down[i]))
        if i % 2 == 0:
            d_att = att_rel_bias.shape[1] * HEAD_DIM
            d_rw = rwkv_w0.shape[1]
            n_in = ab_w_in.shape[2]
            tn = 640 if n_in % 640 == 0 else 128
            proj = _norm_matmul(h, mix_norm[i], bf(ab_w_in[j]), 2 * ROW_TILE, tn)
            bias = _bias_tiles(att_rel_bias[j])
            att = _attention(proj, bias, att_q_gain[j], att_k_gain[j], bsz, seq, d_att)
            rw = _rwkv(proj, 3 * d_att, bsz, seq, d_rw, rwkv_mu[j], rwkv_w0[j], rwkv_w_up[j],
                       rwkv_a0[j], rwkv_a_up[j], rwkv_g_up[j], rwkv_k_k[j], rwkv_k_a[j],
                       rwkv_r_k[j].reshape(-1), rwkv_lnx_w[j], rwkv_lnx_b[j])
            h = _out2(h, att, rw, bf(ab_w_out[j]))
        else:
            d_ssm = ssm_w_in.shape[2]
            u = _norm_matmul(h, mix_norm[i], bf(ssm_w_in[j]), ROW_TILE, d_ssm)
            mats = _s5_prep(ssm_lambda_re[j], ssm_lambda_im[j], ssm_log_dt[j], ssm_b_re[j],
                            ssm_b_im[j], ssm_c_re[j], ssm_c_im[j])
            y = _s5(u, bsz, seq, mats, ssm_lambda_re[j], ssm_lambda_im[j], ssm_log_dt[j], ssm_d[j])
            h = _glu_out(h, y, bf(ssm_w_out[j]))
        h = _ffn(h, ffn2_norm[i], bf(ffn2_w_gate[i]), bf(ffn2_w_up[i]), bf(ffn2_w_down[i]))
        h = _ple(h, ple_norm[i], p[i].reshape(n, -1), bf(ple_w_gate[i]), bf(ple_w_proj[i]))
    return h.reshape(bsz, seq, d)
```

```python
import functools
import math

import jax
import jax.numpy as jnp
from jax import lax
from jax.experimental import pallas as pl
from jax.experimental.pallas import tpu as pltpu

F32 = jnp.float32
BF16 = jnp.bfloat16

RMS_EPS = 1e-6
GN_EPS = 64e-5
CHUNK = 64
N_LEFT_CHUNKS = 8
REL_CLIP = 128
HEAD_DIM = 64
ATT_LEFT = N_LEFT_CHUNKS * CHUNK
ATT_TQ = ATT_LEFT
ATT_SUB = 128
RWKV_L = 64
SSM_GROUP = 16
SSM_STATE = 64
SSM_L = 16
NEG = -1e30

ROW_TILE = 512
COL_TILE = 512
VMEM_LIMIT = 52 * 2 ** 20


def _cparams(*sem):
    return pltpu.CompilerParams(dimension_semantics=sem, vmem_limit_bytes=VMEM_LIMIT)


def _dot(a, b):
    return jnp.dot(a, b, preferred_element_type=F32)


def _dot_nt(a, b):
    return lax.dot_general(a, b, (((1,), (1,)), ((), ())), preferred_element_type=F32)


def _dot_tn(a, b):
    return lax.dot_general(a, b, (((0,), (0,)), ((), ())), preferred_element_type=F32)


def _bdot(a, b):
    return _dot(a.astype(BF16), b.astype(BF16))


def _bdot_nt(a, b):
    return _dot_nt(a.astype(BF16), b.astype(BF16))


def _bdot_tn(a, b):
    return _dot_tn(a.astype(BF16), b.astype(BF16))


def _rms(x, g):
    return x * lax.rsqrt(jnp.mean(x * x, axis=-1, keepdims=True) + RMS_EPS) * g


def _sigmoid(x):
    return 1.0 / (1.0 + jnp.exp(-x))


def _ffn_kernel(h_ref, g_ref, wg_ref, wu_ref, wd_ref, o_ref, n_scr, acc_scr):
    f = pl.program_id(1)

    @pl.when(f == 0)
    def _():
        n_scr[...] = _rms(h_ref[...], g_ref[...]).astype(BF16)
        acc_scr[...] = jnp.zeros_like(acc_scr)

    n = n_scr[...]
    gate = _dot(n, wg_ref[...])
    up = _dot(n, wu_ref[...])
    act = (gate * _sigmoid(gate) * up).astype(BF16)
    acc_scr[...] += _dot(act, wd_ref[...])

    @pl.when(f == pl.num_programs(1) - 1)
    def _():
        o_ref[...] = h_ref[...] + 0.5 * acc_scr[...]


def _ffn(h, g, wg, wu, wd, layer):
    n, d = h.shape
    dff = wg.shape[2]
    tm = min(ROW_TILE, n)
    tf = min(COL_TILE, dff)
    return pl.pallas_call(
        _ffn_kernel,
        out_shape=jax.ShapeDtypeStruct((n, d), F32),
        grid=(n // tm, dff // tf),
        in_specs=[
            pl.BlockSpec((tm, d), lambda i, f: (i, 0)),
            pl.BlockSpec((1, d), lambda i, f: (0, 0)),
            pl.BlockSpec((None, d, tf), lambda i, f: (layer, 0, f)),
            pl.BlockSpec((None, d, tf), lambda i, f: (layer, 0, f)),
            pl.BlockSpec((None, tf, d), lambda i, f: (layer, f, 0)),
        ],
        out_specs=pl.BlockSpec((tm, d), lambda i, f: (i, 0)),
        scratch_shapes=[pltpu.VMEM((tm, d), BF16), pltpu.VMEM((tm, d), F32)],
        compiler_params=_cparams("parallel", "arbitrary"),
        name="ffn",
    )(h, g.reshape(1, d), wg, wu, wd)


def _nmm_kernel(x_ref, g_ref, w_ref, o_ref, n_scr):
    @pl.when(pl.program_id(1) == 0)
    def _():
        n_scr[...] = _rms(x_ref[...], g_ref[...]).astype(BF16)

    o_ref[...] = _dot(n_scr[...], w_ref[...]).astype(o_ref.dtype)


def _norm_matmul(x, g, w, tm, tn):
    n, d = x.shape
    nout = w.shape[1]
    tm = min(tm, n)
    return pl.pallas_call(
        _nmm_kernel,
        out_shape=jax.ShapeDtypeStruct((n, nout), F32),
        grid=(n // tm, nout // tn),
        in_specs=[
            pl.BlockSpec((tm, d), lambda i, j: (i, 0)),
            pl.BlockSpec((1, d), lambda i, j: (0, 0)),
            pl.BlockSpec((d, tn), lambda i, j: (0, j)),
        ],
        out_specs=pl.BlockSpec((tm, tn), lambda i, j: (i, j)),
        scratch_shapes=[pltpu.VMEM((tm, d), BF16)],
        compiler_params=_cparams("parallel", "arbitrary"),
        name="norm_matmul",
    )(x, g.reshape(1, d), w)


def _out2_kernel(res_ref, a1_ref, a2_ref, w1_ref, w2_ref, o_ref):
    o_ref[...] = res_ref[...] + _dot(a1_ref[...], w1_ref[...]) + _dot(a2_ref[...], w2_ref[...])


def _out2(res, a1, a2, w):
    n, d = res.shape
    k1 = a1.shape[1]
    tm = min(ROW_TILE, n)
    return pl.pallas_call(
        _out2_kernel,
        out_shape=jax.ShapeDtypeStruct((n, d), F32),
        grid=(n // tm,),
        in_specs=[
            pl.BlockSpec((tm, d), lambda i: (i, 0)),
            pl.BlockSpec((tm, k1), lambda i: (i, 0)),
            pl.BlockSpec((tm, k1), lambda i: (i, 0)),
            pl.BlockSpec((k1, d), lambda i: (0, 0)),
            pl.BlockSpec((k1, d), lambda i: (1, 0)),
        ],
        out_specs=pl.BlockSpec((tm, d), lambda i: (i, 0)),
        compiler_params=_cparams("parallel"),
        name="mixer_out",
    )(res, a1, a2, w, w)


def _glu_kernel(res_ref, a_ref, wa_ref, wb_ref, o_ref):
    a = a_ref[...]
    za = _dot(a, wa_ref[...])
    zb = _dot(a, wb_ref[...])
    o_ref[...] = res_ref[...] + za * _sigmoid(zb)


def _glu_out(res, a, w):
    n, d = res.shape
    k = a.shape[1]
    tm = min(ROW_TILE, n)
    return pl.pallas_call(
        _glu_kernel,
        out_shape=jax.ShapeDtypeStruct((n, d), F32),
        grid=(n // tm,),
        in_specs=[
            pl.BlockSpec((tm, d), lambda i: (i, 0)),
            pl.BlockSpec((tm, k), lambda i: (i, 0)),
            pl.BlockSpec((k, d), lambda i: (0, 0)),
            pl.BlockSpec((k, d), lambda i: (0, 1)),
        ],
        out_specs=pl.BlockSpec((tm, d), lambda i: (i, 0)),
        compiler_params=_cparams("parallel"),
        name="glu_out",
    )(res, a, w, w)


def _ple_kernel(h_ref, g_ref, p_ref, wg_ref, wp_ref, o_ref):
    h = h_ref[...]
    gate = _sigmoid(_dot(_rms(h, g_ref[...]).astype(BF16), wg_ref[...]))
    proj = _dot(p_ref[...].astype(BF16), wp_ref[...])
    o_ref[...] = h + gate * proj


def _ple(h, g, p, wg, wp, layer):
    n, d = h.shape
    dp = p.shape[2]
    tm = min(ROW_TILE, n)
    return pl.pallas_call(
        _ple_kernel,
        out_shape=jax.ShapeDtypeStruct((n, d), F32),
        grid=(n // tm,),
        in_specs=[
            pl.BlockSpec((tm, d), lambda i: (i, 0)),
            pl.BlockSpec((1, d), lambda i: (0, 0)),
            pl.BlockSpec((None, tm, dp), lambda i: (layer, i, 0)),
            pl.BlockSpec((None, d, d), lambda i: (layer, 0, 0)),
            pl.BlockSpec((None, dp, d), lambda i: (layer, 0, 0)),
        ],
        out_specs=pl.BlockSpec((tm, d), lambda i: (i, 0)),
        compiler_params=_cparams("parallel"),
        name="ple",
    )(h, g.reshape(1, d), p, wg, wp)


def _bias_kernel(tbl_ref, o_ref):
    tq = o_ref.shape[1]
    tk = o_ref.shape[2]
    n_rel = tbl_ref.shape[2]
    width = pl.next_power_of_2(tq + tk)
    n_idx = lax.broadcasted_iota(jnp.int32, (n_rel, width), 1)
    c_idx = lax.broadcasted_iota(jnp.int32, (n_rel, width), 0)
    m = jnp.where(n_idx < tk, n_idx, n_idx - width)
    idx = jnp.clip(ATT_LEFT - m, -(CHUNK - 1), REL_CLIP) + (CHUNK - 1)
    onehot = (c_idx == idx).astype(F32)
    tbl = jnp.broadcast_to(tbl_ref[0], (8, n_rel))
    ext = jnp.dot(tbl, onehot, preferred_element_type=F32, precision=lax.Precision.HIGHEST)[0:1]
    x = jnp.broadcast_to(ext, (tq, width))
    rolled = pltpu.roll(x, 0, 1, stride=1, stride_axis=0)[:, :tk]
    qc = lax.broadcasted_iota(jnp.int32, (tq, tk), 0) // CHUNK
    kc = lax.broadcasted_iota(jnp.int32, (tq, tk), 1) // CHUNK
    ok = (kc >= qc) & (kc <= qc + N_LEFT_CHUNKS)
    o_ref[0] = jnp.where(ok, rolled, NEG)


def _bias_tiles(rel_bias):
    nh, n_rel = rel_bias.shape
    tk = ATT_SUB + ATT_LEFT
    return pl.pallas_call(
        _bias_kernel,
        out_shape=jax.ShapeDtypeStruct((nh, ATT_SUB, tk), F32),
        grid=(nh,),
        in_specs=[pl.BlockSpec((1, 1, n_rel), lambda h: (h, 0, 0))],
        out_specs=pl.BlockSpec((1, ATT_SUB, tk), lambda h: (h, 0, 0)),
        compiler_params=_cparams("parallel"),
        name="rel_bias_tiles",
    )(rel_bias.reshape(nh, 1, n_rel))


def _attn_kernel(q_ref, kp_ref, kc_ref, vp_ref, vc_ref, bias_ref, qg_ref, kg_ref, o_ref):
    has_prev = pl.program_id(2) > 0
    tq, w = q_ref.shape
    tk = bias_ref.shape[2]
    col = lax.broadcasted_iota(jnp.int32, (ATT_SUB, tk), 1)
    head_of_lane = lax.broadcasted_iota(jnp.int32, (1, w), 1) // HEAD_DIM
    seg = (lax.broadcasted_iota(jnp.int32, (w, w), 0) // HEAD_DIM
           == lax.broadcasted_iota(jnp.int32, (w, w), 1) // HEAD_DIM)
    seg = jnp.where(seg, 1.0, 0.0).astype(BF16)

    def rms2(x, gain):
        sq = x * x
        hi = sq.astype(BF16)
        lo = (sq - hi.astype(F32)).astype(BF16)
        ss = _dot(hi, seg) + _dot(lo, seg)
        return x * lax.rsqrt(ss * (1.0 / HEAD_DIM) + RMS_EPS) * gain

    q = rms2(q_ref[...], qg_ref[...] * (HEAD_DIM ** -0.5))
    k = jnp.concatenate([rms2(kp_ref[...], kg_ref[...]), rms2(kc_ref[...], kg_ref[...])], axis=0).astype(BF16)
    v = jnp.concatenate([vp_ref[...], vc_ref[...]], axis=0).astype(BF16)
    nsub = tq // ATT_SUB
    heads = range(w // HEAD_DIM)
    qh = [jnp.where(head_of_lane == hh, q, 0.0).astype(BF16) for hh in heads]
    jobs = [(hh, i * ATT_SUB) for hh in heads for i in range(nsub)]
    sc = [_dot_nt(qh[hh][r0:r0 + ATT_SUB], k[r0:r0 + tk]) + bias_ref[hh] for hh, r0 in jobs]
    sc = [jnp.where(has_prev | (col >= ATT_LEFT - r0), s, NEG) if r0 < ATT_LEFT else s
          for s, (hh, r0) in zip(sc, jobs)]
    pr = [jnp.exp(s - jnp.max(s, axis=-1, keepdims=True)) for s in sc]
    den = [jnp.sum(p, axis=-1, keepdims=True) for p in pr]
    ob = [_dot(p.astype(BF16), v[r0:r0 + tk]) / d for p, d, (hh, r0) in zip(pr, den, jobs)]
    outs = [jnp.concatenate(ob[hh * nsub:(hh + 1) * nsub], axis=0) for hh in heads]
    o = outs[0]
    for hh in range(1, len(outs)):
        o = jnp.where(head_of_lane == hh, outs[hh], o)
    o_ref[...] = o.astype(o_ref.dtype)


def _attention(proj, bias, q_gain, k_gain, bsz, seq, d_att):
    n = bsz * seq
    tq = ATT_TQ
    nqb = seq // tq
    npair = d_att // (2 * HEAD_DIM)
    w = 2 * HEAD_DIM

    def cur(col0):
        return pl.BlockSpec((tq, w), lambda hp, b, qb: (b * nqb + qb, col0 + hp))

    def prev(col0):
        return pl.BlockSpec((tq, w), lambda hp, b, qb: (b * nqb + jnp.maximum(qb - 1, 0), col0 + hp))

    return pl.pallas_call(
        _attn_kernel,
        out_shape=jax.ShapeDtypeStruct((n, d_att), BF16),
        grid=(npair, bsz, nqb),
        in_specs=[
            cur(0), prev(npair), cur(npair), prev(2 * npair), cur(2 * npair),
            pl.BlockSpec((2, ATT_SUB, ATT_SUB + ATT_LEFT), lambda hp, b, qb: (hp, 0, 0)),
            pl.BlockSpec((1, w), lambda hp, b, qb: (0, 0)),
            pl.BlockSpec((1, w), lambda hp, b, qb: (0, 0)),
        ],
        out_specs=pl.BlockSpec((tq, w), lambda hp, b, qb: (b * nqb + qb, hp)),
        compiler_params=_cparams("parallel", "parallel", "arbitrary"),
        name="band_attention",
    )(proj, proj, proj, proj, proj, bias, jnp.tile(q_gain, 2).reshape(1, w), jnp.tile(k_gain, 2).reshape(1, w))


def _split3(x):
    hi = x.astype(BF16)
    r1 = x - hi.astype(F32)
    mid = r1.astype(BF16)
    lo = (r1 - mid.astype(F32)).astype(BF16)
    return hi, mid, lo


def _token_shift(z, carry_ref, mu):
    rows = lax.broadcasted_iota(jnp.int32, z.shape, 0)
    prev = jnp.where(rows == 0, carry_ref[...], pltpu.roll(z, 1, 0))
    carry_ref[...] = z[z.shape[0] - 1:, :]
    return z + (prev - z) * mu


def _rwkv_kernel(r_ref, k_ref, v_ref, l_ref, mur_ref, muk_ref, muv_ref, mul_ref,
                 w0_ref, a0_ref, kk_ref, ka_ref, rk_ref, lnw_ref, lnb_ref,
                 wup_ref, aup_ref, gup_ref, o_ref,
                 s_scr, cr_scr, ck_scr, cv_scr, cl_scr):
    first = pl.program_id(1) == 0
    L = r_ref.shape[0]
    d = r_ref.shape[1]
    nh = d // HEAD_DIM

    @pl.when(first)
    def _():
        s_scr[...] = jnp.zeros_like(s_scr)
        for c in (cr_scr, ck_scr, cv_scr, cl_scr):
            c[...] = jnp.zeros_like(c)

    r = _token_shift(r_ref[...], cr_scr, mur_ref[...])
    k = _token_shift(k_ref[...], ck_scr, muk_ref[...])
    v = _token_shift(v_ref[...], cv_scr, muv_ref[...])
    lo = _token_shift(l_ref[...], cl_scr, mul_ref[...])
    nw = wup_ref.shape[0]
    na = aup_ref.shape[0]
    xw, xa, xg = lo[:, :nw], lo[:, nw:nw + na], lo[:, nw + na:]

    wpre = w0_ref[...] + _bdot(jnp.tanh(xw), wup_ref[...])
    w_log = -(jnp.maximum(-wpre, 0.0) + jnp.log(1.0 + jnp.exp(-jnp.abs(wpre)))) - 0.5
    lw = -jnp.exp(w_log)
    a = _sigmoid(a0_ref[...] + _bdot(xa, aup_ref[...]))
    g = _bdot(_sigmoid(xg), gup_ref[...])

    kkf = k * kk_ref[...]
    k2 = k * (1.0 + (a - 1.0) * ka_ref[...])

    ti = lax.broadcasted_iota(jnp.int32, (L, L), 0)
    si = lax.broadcasted_iota(jnp.int32, (L, L), 1)
    tri = jnp.where(si <= ti, 1.0, 0.0).astype(BF16)
    h3 = _split3(lw)
    cum = _dot(tri, h3[0]) + _dot(tri, h3[1]) + _dot(tri, h3[2])
    e_pos = jnp.exp(cum)
    e_neg = jnp.exp(-cum)
    e_prev = jnp.exp(cum - lw)
    w_all = e_pos[L - 1:, :]

    rk = r * k2 * rk_ref[...]
    w2 = 2 * HEAD_DIM
    pairs = range(d // w2)
    psl = [slice(j * w2, (j + 1) * w2) for j in pairs]
    lane = lax.broadcasted_iota(jnp.int32, (1, w2), 1)
    in_a = lane < HEAD_DIM
    row_l = lax.broadcasted_iota(jnp.int32, (L, w2), 0)
    idx_l = lax.broadcasted_iota(jnp.int32, (L, w2), 1) % HEAD_DIM
    strict = idx_l < row_l
    incl = idx_l <= row_l
    same_head = (lax.broadcasted_iota(jnp.int32, (w2, w2), 0) // HEAD_DIM
                 == lax.broadcasted_iota(jnp.int32, (w2, w2), 1) // HEAD_DIM)
    seg = jnp.where(same_head, 1.0, 0.0).astype(BF16)

    def segsum(xs):
        x = jnp.concatenate(xs, axis=0)
        hi = x.astype(BF16)
        lo = (x - hi.astype(F32)).astype(BF16)
        tot = _dot(hi, seg) + _dot(lo, seg)
        return [tot[i * L:(i + 1) * L] for i in range(len(xs))]

    def bdiag(x):
        return jnp.concatenate([jnp.where(in_a, x, 0.0), jnp.where(in_a, 0.0, x)], axis=0)

    ar, bkc, rb, vb, vbd = [], [], [], [], []
    kss = segsum([kkf[:, sl] * kkf[:, sl] for sl in psl])
    for sl, ks in zip(psl, kss):
        kk = kkf[:, sl] / jnp.maximum(jnp.sqrt(ks), 1e-12)
        ah = -kk * e_prev[:, sl]
        bh = kk * a[:, sl] * e_neg[:, sl]
        kh = k2[:, sl] * e_neg[:, sl]
        rh = r[:, sl] * e_pos[:, sl]
        ar.append(jnp.concatenate([ah, rh], axis=0).astype(BF16))
        bkc.append(jnp.concatenate([bh, kh], axis=0).astype(BF16))
        rb.append(jnp.concatenate([bdiag(bh), bdiag(kh)], axis=0).astype(BF16))
        vb.append(v[:, sl].astype(BF16))
        vbd.append(bdiag(v[:, sl]).astype(BF16))
    s0 = [s_scr[j] for j in pairs]
    gm = [_dot_nt(ar[j], rb[j]) for j in pairs]
    ps = [_dot_nt(ar[j], s0[j].astype(BF16)) for j in pairs]
    pw = [jnp.where(strict, gm[j][:L, :w2], 0.0).astype(BF16) for j in pairs]
    u = [ps[j][:L] + _dot(jnp.where(strict, gm[j][:L, w2:], 0.0).astype(BF16), vbd[j]) for j in pairs]
    steps = max(1, (L - 1).bit_length())
    for i in range(steps):
        u = [u[j] + _dot(pw[j], bdiag(u[j]).astype(BF16)) for j in pairs]
        if i + 1 < steps:
            pw = [_dot(pw[j], bdiag(pw[j])).astype(BF16) for j in pairs]
    y = []
    for j in pairs:
        low = jnp.concatenate([jnp.where(incl, gm[j][L:, :w2], 0.0), jnp.where(incl, gm[j][L:, w2:], 0.0)], axis=1)
        uvd = jnp.concatenate([bdiag(u[j]).astype(BF16), vbd[j]], axis=0)
        y.append(ps[j][L:] + _dot(low.astype(BF16), uvd))
    for j in pairs:
        uv = jnp.concatenate([u[j].astype(BF16), vb[j]], axis=0)
        s_scr[j] = jnp.where(same_head, s0[j] + _dot_tn(uv, bkc[j]), 0.0) * w_all[:, psl[j]]
    yc = [y[j] - m * (1.0 / HEAD_DIM) for j, m in zip(pairs, segsum(y))]
    var = segsum([c * c for c in yc])
    yn = jnp.concatenate([c * lax.rsqrt(vr * (1.0 / HEAD_DIM) + GN_EPS) for c, vr in zip(yc, var)], axis=1)
    bonus = jnp.concatenate(segsum([rk[:, sl] for sl in psl]), axis=1) * v
    o_ref[...] = ((yn * lnw_ref[...] + lnb_ref[...] + bonus) * g).astype(o_ref.dtype)


def _rwkv(proj, col0, bsz, seq, d, mu, w0, w_up, a0, a_up, g_up, k_k, k_a, r_k, lnx_w, lnx_b):
    n = bsz * seq
    L = RWKV_L
    nt = seq // L
    nl = mu.shape[0] - 3 * d
    row = lambda x: x.reshape(1, -1)
    cb = col0 // d
    lb = (col0 + 3 * d) // nl

    def zspec(width, blk):
        return pl.BlockSpec((L, width), lambda b, t: (b * nt + t, blk))

    def pspec(shape):
        return pl.BlockSpec(shape, lambda b, t: (0,) * len(shape))

    nh = d // HEAD_DIM
    return pl.pallas_call(
        _rwkv_kernel,
        out_shape=jax.ShapeDtypeStruct((n, d), BF16),
        grid=(bsz, nt),
        in_specs=[
            zspec(d, cb), zspec(d, cb + 1), zspec(d, cb + 2), zspec(nl, lb),
            pspec((1, d)), pspec((1, d)), pspec((1, d)), pspec((1, nl)),
            pspec((1, d)), pspec((1, d)), pspec((1, d)), pspec((1, d)), pspec((1, d)),
            pspec((1, d)), pspec((1, d)),
            pspec(w_up.shape), pspec(a_up.shape), pspec(g_up.shape),
        ],
        out_specs=pl.BlockSpec((L, d), lambda b, t: (b * nt + t, 0)),
        scratch_shapes=[
            pltpu.VMEM((nh // 2, 2 * HEAD_DIM, 2 * HEAD_DIM), F32),
            pltpu.VMEM((1, d), F32), pltpu.VMEM((1, d), F32), pltpu.VMEM((1, d), F32),
            pltpu.VMEM((1, nl), F32),
        ],
        compiler_params=_cparams("parallel", "arbitrary"),
        name="rwkv7",
    )(proj, proj, proj, proj,
      row(mu[:d]), row(mu[d:2 * d]), row(mu[2 * d:3 * d]), row(mu[3 * d:]),
      row(w0), row(a0), row(k_k), row(k_a), row(r_k), row(lnx_w), row(lnx_b),
      w_up.astype(BF16), a_up.astype(BF16), g_up.astype(BF16))


SSM_GB = 8


def _cmul(ar, ai, br, bi):
    return ar * br - ai * bi, ar * bi + ai * br


def _s5_abar(lr, li, ldt):
    dt = jnp.exp(ldt)
    mag = jnp.exp(lr * dt)
    return mag * jnp.cos(li * dt), mag * jnp.sin(li * dt)


def _s5_prep_kernel(lrr_ref, lir_ref, ldtr_ref, lrc_ref, lic_ref, ldtc_ref,
                    bre_ref, bim_ref, cre_ref, cim_ref, k_ref, p_ref, q_ref):
    L = SSM_L
    w = k_ref.shape[2]
    lr, li = lrr_ref[0], lir_ref[0]
    a_re, a_im = _s5_abar(lr, li, ldtr_ref[0])
    den = lr * lr + li * li
    z_re = ((a_re - 1.0) * lr + a_im * li) / den
    z_im = (a_im * lr - (a_re - 1.0) * li) / den
    bb_re, bb_im = _cmul(z_re, z_im, bre_ref[0], bim_ref[0])
    cre, cim = cre_ref[0], cim_ref[0]
    ccat = jnp.concatenate([cre, -cim], axis=0)
    ac_re, ac_im = _s5_abar(lrc_ref[0], lic_ref[0], ldtc_ref[0])
    pr, pi = jnp.ones_like(a_re), jnp.zeros_like(a_im)
    qr, qi = ac_re, ac_im
    for tau in range(L):
        rb_re, rb_im = _cmul(pr, pi, bb_re, bb_im)
        ptau = jnp.concatenate([rb_re, rb_im], axis=1)
        p_ref[0, L - 1 - tau] = ptau.astype(p_ref.dtype)
        ktau = jnp.dot(ptau, ccat, preferred_element_type=F32, precision=lax.Precision.HIGHEST)
        k_ref[0, (L - 1 - tau) * w:(L - tau) * w, :] = ktau.astype(k_ref.dtype)
        q_ref[0, tau] = jnp.concatenate([cre * qr - cim * qi, -(cre * qi + cim * qr)], axis=0).astype(q_ref.dtype)
        pr, pi = _cmul(pr, pi, a_re, a_im)
        qr, qi = _cmul(qr, qi, ac_re, ac_im)


def _s5_prep(lam_re, lam_im, log_dt, b_re, b_im, c_re, c_im):
    G, P = lam_re.shape
    gs = SSM_GROUP
    gb = SSM_GB
    J = G // gb
    L = SSM_L
    eye = jnp.eye(gb, dtype=F32)
    ldt = jnp.repeat(log_dt, P)

    def bdiag_b(b):
        bt = jnp.swapaxes(b, 1, 2).reshape(J, gb, gs, P)
        return (bt[:, :, :, None, :] * eye[None, :, None, :, None]).reshape(J, gb * gs, gb * P)

    def bdiag_c(c):
        ct = jnp.swapaxes(c, 1, 2).reshape(J, gb, P, gs)
        return (ct[:, :, :, None, :] * eye[None, :, None, :, None]).reshape(J, gb * P, gb * gs)

    row = lambda x: x.reshape(J, 1, gb * P)
    col = lambda x: x.reshape(J, gb * P, 1)

    def spec(*shape):
        return pl.BlockSpec((1,) + shape, lambda j: (j,) + (0,) * len(shape))

    w, sw = gb * gs, gb * P
    return pl.pallas_call(
        _s5_prep_kernel,
        out_shape=[jax.ShapeDtypeStruct((J, L * w, w), BF16),
                   jax.ShapeDtypeStruct((J, L, w, 2 * sw), BF16),
                   jax.ShapeDtypeStruct((J, L, 2 * sw, w), BF16)],
        grid=(J,),
        in_specs=[spec(1, sw), spec(1, sw), spec(1, sw), spec(sw, 1), spec(sw, 1), spec(sw, 1),
                  spec(w, sw), spec(w, sw), spec(sw, w), spec(sw, w)],
        out_specs=[spec(L * w, w), spec(L, w, 2 * sw), spec(L, 2 * sw, w)],
        compiler_params=_cparams("parallel"),
        name="s5_prep",
    )(row(lam_re), row(lam_im), row(ldt), col(lam_re), col(lam_im), col(ldt),
      bdiag_b(b_re), bdiag_b(b_im), bdiag_c(c_re), bdiag_c(c_im))


def _s5_kernel(u_ref, k_ref, p_ref, q_ref, lr_ref, li_ref, ldt_ref, d_ref, o_ref,
               gre_scr, gim_scr, hre_scr, him_scr, y_scr):
    L = SSM_L
    nc = u_ref.shape[0] // L
    sw = lr_ref.shape[2]
    w = u_ref.shape[1]
    us = [u_ref[pl.ds(s, nc, stride=L), :].astype(BF16) for s in range(L)]
    gall = _dot(us[0], p_ref[0, 0])
    for s in range(1, L):
        gall = gall + _dot(us[s], p_ref[0, s])
    gre_scr[...] = gall[:, :sw]
    gim_scr[...] = gall[:, sw:]
    a_re, a_im = _s5_abar(lr_ref[0], li_ref[0], ldt_ref[0])
    al_re, al_im = a_re, a_im
    for _ in range(L.bit_length() - 1):
        al_re, al_im = _cmul(al_re, al_im, al_re, al_im)

    def body(c, carry):
        hre, him = carry
        hre_scr[pl.ds(c, 1), :] = hre
        him_scr[pl.ds(c, 1), :] = him
        gr = gre_scr[pl.ds(c, 1), :]
        gi = gim_scr[pl.ds(c, 1), :]
        return al_re * hre - al_im * him + gr, al_re * him + al_im * hre + gi

    zero = jnp.zeros((1, sw), F32)
    lax.fori_loop(0, nc, body, (zero, zero))
    hp = jnp.concatenate([hre_scr[...], him_scr[...]], axis=1).astype(BF16)
    ucat = jnp.concatenate(us, axis=1)
    kall = k_ref[0]
    for t in range(L):
        yt = _dot(hp, q_ref[0, t]) + _dot(ucat[:, :(t + 1) * w], kall[(L - 1 - t) * w:, :])
        y_scr[pl.ds(t, nc, stride=L), :] = yt
    y = y_scr[...] + d_ref[0] * u_ref[...]
    o_ref[...] = (0.5 * y * (1.0 + jnp.tanh(math.sqrt(2.0 / math.pi) * (y + 0.044715 * y * y * y)))).astype(o_ref.dtype)


def _s5(u, bsz, seq, mats, lam_re, lam_im, log_dt, d_skip):
    kb, pb, qb = mats
    G, P = lam_re.shape
    gb = SSM_GB
    J = G // gb
    L = SSM_L
    w = gb * SSM_GROUP
    sw = gb * P
    nc = seq // L
    row = lambda x: x.reshape(J, 1, sw)

    def jspec(*shape):
        return pl.BlockSpec((1,) + shape, lambda j, b: (j,) + (0,) * len(shape))

    return pl.pallas_call(
        _s5_kernel,
        out_shape=jax.ShapeDtypeStruct((bsz * seq, G * SSM_GROUP), BF16),
        grid=(J, bsz),
        in_specs=[
            pl.BlockSpec((seq, w), lambda j, b: (b, j)),
            jspec(L * w, w), jspec(L, w, 2 * sw), jspec(L, 2 * sw, w),
            jspec(1, sw), jspec(1, sw), jspec(1, sw), jspec(1, w),
        ],
        out_specs=pl.BlockSpec((seq, w), lambda j, b: (b, j)),
        scratch_shapes=[pltpu.VMEM((nc, sw), F32)] * 4 + [pltpu.VMEM((seq, w), F32)],
        compiler_params=_cparams("parallel", "arbitrary"),
        name="s5_ssm",
    )(u, kb, pb, qb, row(lam_re), row(lam_im), row(jnp.repeat(log_dt, P)), d_skip.reshape(J, 1, w))


def kernel(x, p, ffn1_norm, ffn1_w_gate, ffn1_w_up, ffn1_w_down, mix_norm, ffn2_norm, ffn2_w_gate, ffn2_w_up, ffn2_w_down, ple_norm, ple_w_gate, ple_w_proj, ab_w_in, att_q_gain, att_k_gain, att_rel_bias, rwkv_mu, rwkv_w0, rwkv_w_up, rwkv_a0, rwkv_a_up, rwkv_g_up, rwkv_k_k, rwkv_k_a, rwkv_r_k, rwkv_lnx_w, rwkv_lnx_b, ab_w_out, ssm_w_in, ssm_lambda_re, ssm_lambda_im, ssm_log_dt, ssm_b_re, ssm_b_im, ssm_c_re, ssm_c_im, ssm_d, ssm_w_out):
    bsz, seq, d = x.shape
    depth = p.shape[0]
    n = bsz * seq
    bf = lambda w: w.astype(BF16)
    h = x.reshape(n, d)
    w1 = [bf(w) for w in (ffn1_w_gate, ffn1_w_up, ffn1_w_down)]
    w2 = [bf(w) for w in (ffn2_w_gate, ffn2_w_up, ffn2_w_down)]
    wpl = [bf(w) for w in (ple_w_gate, ple_w_proj)]
    pe = p.reshape(depth, n, -1)
    for i in range(depth):
        j = i // 2
        h = _ffn(h, ffn1_norm[i], w1[0], w1[1], w1[2], i)
        if i % 2 == 0:
            d_att = att_rel_bias.shape[1] * HEAD_DIM
            d_rw = rwkv_w0.shape[1]
            n_in = ab_w_in.shape[2]
            tn = 1280 if n_in % 1280 == 0 else 128
            proj = _norm_matmul(h, mix_norm[i], bf(ab_w_in[j]), 2 * ROW_TILE, tn)
            bias = _bias_tiles(att_rel_bias[j])
            att = _attention(proj, bias, att_q_gain[j], att_k_gain[j], bsz, seq, d_att)
            rw = _rwkv(proj, 3 * d_att, bsz, seq, d_rw, rwkv_mu[j], rwkv_w0[j], rwkv_w_up[j],
                       rwkv_a0[j], rwkv_a_up[j], rwkv_g_up[j], rwkv_k_k[j], rwkv_k_a[j],
                       rwkv_r_k[j].reshape(-1), rwkv_lnx_w[j], rwkv_lnx_b[j])
            h = _out2(h, att, rw, bf(ab_w_out[j]))
        else:
            d_ssm = ssm_w_in.shape[2]
            u = _norm_matmul(h, mix_norm[i], bf(ssm_w_in[j]), ROW_TILE, d_ssm)
            mats = _s5_prep(ssm_lambda_re[j], ssm_lambda_im[j], ssm_log_dt[j], ssm_b_re[j],
                            ssm_b_im[j], ssm_c_re[j], ssm_c_im[j])
            y = _s5(u, bsz, seq, mats, ssm_lambda_re[j], ssm_lambda_im[j], ssm_log_dt[j], ssm_d[j])
            h = _glu_out(h, y, bf(ssm_w_out[j]))
        h = _ffn(h, ffn2_norm[i], w2[0], w2[1], w2[2], i)
        h = _ple(h, ple_norm[i], pe, wpl[0], wpl[1], i)
    return h.reshape(bsz, seq, d)
```

```python
import functools
import math

import jax
import jax.numpy as jnp
from jax import lax
from jax.experimental import pallas as pl
from jax.experimental.pallas import tpu as pltpu

F32 = jnp.float32
BF16 = jnp.bfloat16

RMS_EPS = 1e-6
GN_EPS = 64e-5
CHUNK = 64
N_LEFT_CHUNKS = 8
REL_CLIP = 128
HEAD_DIM = 64
ATT_LEFT = N_LEFT_CHUNKS * CHUNK
ATT_TQ = ATT_LEFT
ATT_SUB = 128
RWKV_L = 64
SSM_GROUP = 16
SSM_STATE = 64
SSM_L = 16
NEG = -1e30

ROW_TILE = 512
FFN_ROW_TILE = 1024
FFN_VMEM_LIMIT = 58 * 2 ** 20
COL_TILE = 512
VMEM_LIMIT = 52 * 2 ** 20


def _cparams(*sem):
    return pltpu.CompilerParams(dimension_semantics=sem, vmem_limit_bytes=VMEM_LIMIT)


def _dot(a, b):
    return jnp.dot(a, b, preferred_element_type=F32)


def _dot_nt(a, b):
    return lax.dot_general(a, b, (((1,), (1,)), ((), ())), preferred_element_type=F32)


def _dot_tn(a, b):
    return lax.dot_general(a, b, (((0,), (0,)), ((), ())), preferred_element_type=F32)


def _bdot(a, b):
    return _dot(a.astype(BF16), b.astype(BF16))


def _bdot_nt(a, b):
    return _dot_nt(a.astype(BF16), b.astype(BF16))


def _bdot_tn(a, b):
    return _dot_tn(a.astype(BF16), b.astype(BF16))


def _rms(x, g):
    return x * lax.rsqrt(jnp.mean(x * x, axis=-1, keepdims=True) + RMS_EPS) * g


def _sigmoid(x):
    return 1.0 / (1.0 + jnp.exp(-x))


def _ffn_kernel(h_ref, g_ref, wg_ref, wu_ref, wd_ref, o_ref, n_scr):
    @pl.when(pl.program_id(1) == 0)
    def _():
        h = h_ref[...]
        n_scr[...] = _rms(h, g_ref[...]).astype(BF16)
        o_ref[...] = h

    n = n_scr[...]
    gate = _dot(n, wg_ref[...])
    up = _dot(n, wu_ref[...])
    act = (0.5 * gate * _sigmoid(gate) * up).astype(BF16)
    o_ref[...] += _dot(act, wd_ref[...])


def _ffn(h, g, wg, wu, wd, layer):
    n, d = h.shape
    dff = wg.shape[2]
    tm = min(FFN_ROW_TILE, n)
    tf = min(COL_TILE, dff)
    return pl.pallas_call(
        _ffn_kernel,
        out_shape=jax.ShapeDtypeStruct((n, d), F32),
        grid=(n // tm, dff // tf),
        in_specs=[
            pl.BlockSpec((tm, d), lambda i, f: (i, 0)),
            pl.BlockSpec((1, d), lambda i, f: (0, 0)),
            pl.BlockSpec((None, d, tf), lambda i, f: (layer, 0, f)),
            pl.BlockSpec((None, d, tf), lambda i, f: (layer, 0, f)),
            pl.BlockSpec((None, tf, d), lambda i, f: (layer, f, 0)),
        ],
        out_specs=pl.BlockSpec((tm, d), lambda i, f: (i, 0)),
        scratch_shapes=[pltpu.VMEM((tm, d), BF16)],
        compiler_params=pltpu.CompilerParams(dimension_semantics=("parallel", "arbitrary"),
                                             vmem_limit_bytes=FFN_VMEM_LIMIT),
        name="ffn",
    )(h, g.reshape(1, d), wg, wu, wd)


def _nmm_kernel(x_ref, g_ref, w_ref, o_ref, n_scr):
    @pl.when(pl.program_id(1) == 0)
    def _():
        n_scr[...] = _rms(x_ref[...], g_ref[...]).astype(BF16)

    o_ref[...] = _dot(n_scr[...], w_ref[...]).astype(o_ref.dtype)


def _norm_matmul(x, g, w, tm, tn):
    n, d = x.shape
    nout = w.shape[1]
    tm = min(tm, n)
    return pl.pallas_call(
        _nmm_kernel,
        out_shape=jax.ShapeDtypeStruct((n, nout), F32),
        grid=(n // tm, nout // tn),
        in_specs=[
            pl.BlockSpec((tm, d), lambda i, j: (i, 0)),
            pl.BlockSpec((1, d), lambda i, j: (0, 0)),
            pl.BlockSpec((d, tn), lambda i, j: (0, j)),
        ],
        out_specs=pl.BlockSpec((tm, tn), lambda i, j: (i, j)),
        scratch_shapes=[pltpu.VMEM((tm, d), BF16)],
        compiler_params=_cparams("parallel", "arbitrary"),
        name="norm_matmul",
    )(x, g.reshape(1, d), w)


def _out2_kernel(res_ref, a1_ref, a2_ref, w1_ref, w2_ref, o_ref):
    o_ref[...] = res_ref[...] + _dot(a1_ref[...], w1_ref[...]) + _dot(a2_ref[...], w2_ref[...])


def _out2(res, a1, a2, w):
    n, d = res.shape
    k1 = a1.shape[1]
    tm = min(ROW_TILE, n)
    return pl.pallas_call(
        _out2_kernel,
        out_shape=jax.ShapeDtypeStruct((n, d), F32),
        grid=(n // tm,),
        in_specs=[
            pl.BlockSpec((tm, d), lambda i: (i, 0)),
            pl.BlockSpec((tm, k1), lambda i: (i, 0)),
            pl.BlockSpec((tm, k1), lambda i: (i, 0)),
            pl.BlockSpec((k1, d), lambda i: (0, 0)),
            pl.BlockSpec((k1, d), lambda i: (1, 0)),
        ],
        out_specs=pl.BlockSpec((tm, d), lambda i: (i, 0)),
        compiler_params=_cparams("parallel"),
        name="mixer_out",
    )(res, a1, a2, w, w)


def _glu_kernel(res_ref, a_ref, wa_ref, wb_ref, o_ref):
    a = a_ref[...]
    za = _dot(a, wa_ref[...])
    zb = _dot(a, wb_ref[...])
    o_ref[...] = res_ref[...] + za * _sigmoid(zb)


def _glu_out(res, a, w):
    n, d = res.shape
    k = a.shape[1]
    tm = min(ROW_TILE, n)
    return pl.pallas_call(
        _glu_kernel,
        out_shape=jax.ShapeDtypeStruct((n, d), F32),
        grid=(n // tm,),
        in_specs=[
            pl.BlockSpec((tm, d), lambda i: (i, 0)),
            pl.BlockSpec((tm, k), lambda i: (i, 0)),
            pl.BlockSpec((k, d), lambda i: (0, 0)),
            pl.BlockSpec((k, d), lambda i: (0, 1)),
        ],
        out_specs=pl.BlockSpec((tm, d), lambda i: (i, 0)),
        compiler_params=_cparams("parallel"),
        name="glu_out",
    )(res, a, w, w)


def _ple_kernel(h_ref, g_ref, p_ref, wg_ref, wp_ref, o_ref):
    h = h_ref[...]
    gate = _sigmoid(_dot(_rms(h, g_ref[...]).astype(BF16), wg_ref[...]))
    proj = _dot(p_ref[...].astype(BF16), wp_ref[...])
    o_ref[...] = h + gate * proj


def _ple(h, g, p, wg, wp, layer):
    n, d = h.shape
    dp = p.shape[2]
    tm = min(ROW_TILE, n)
    return pl.pallas_call(
        _ple_kernel,
        out_shape=jax.ShapeDtypeStruct((n, d), F32),
        grid=(n // tm,),
        in_specs=[
            pl.BlockSpec((tm, d), lambda i: (i, 0)),
            pl.BlockSpec((1, d), lambda i: (0, 0)),
            pl.BlockSpec((None, tm, dp), lambda i: (layer, i, 0)),
            pl.BlockSpec((None, d, d), lambda i: (layer, 0, 0)),
            pl.BlockSpec((None, dp, d), lambda i: (layer, 0, 0)),
        ],
        out_specs=pl.BlockSpec((tm, d), lambda i: (i, 0)),
        compiler_params=_cparams("parallel"),
        name="ple",
    )(h, g.reshape(1, d), p, wg, wp)


def _bias_kernel(tbl_ref, o_ref):
    tq = o_ref.shape[1]
    tk = o_ref.shape[2]
    n_rel = tbl_ref.shape[2]
    width = pl.next_power_of_2(tq + tk)
    n_idx = lax.broadcasted_iota(jnp.int32, (n_rel, width), 1)
    c_idx = lax.broadcasted_iota(jnp.int32, (n_rel, width), 0)
    m = jnp.where(n_idx < tk, n_idx, n_idx - width)
    idx = jnp.clip(ATT_LEFT - m, -(CHUNK - 1), REL_CLIP) + (CHUNK - 1)
    onehot = (c_idx == idx).astype(F32)
    tbl = jnp.broadcast_to(tbl_ref[0], (8, n_rel))
    ext = jnp.dot(tbl, onehot, preferred_element_type=F32, precision=lax.Precision.HIGHEST)[0:1]
    x = jnp.broadcast_to(ext, (tq, width))
    rolled = pltpu.roll(x, 0, 1, stride=1, stride_axis=0)[:, :tk]
    qc = lax.broadcasted_iota(jnp.int32, (tq, tk), 0) // CHUNK
    kc = lax.broadcasted_iota(jnp.int32, (tq, tk), 1) // CHUNK
    ok = (kc >= qc) & (kc <= qc + N_LEFT_CHUNKS)
    o_ref[0] = jnp.where(ok, rolled, NEG)


def _bias_tiles(rel_bias):
    nh, n_rel = rel_bias.shape
    tk = ATT_SUB + ATT_LEFT
    return pl.pallas_call(
        _bias_kernel,
        out_shape=jax.ShapeDtypeStruct((nh, ATT_SUB, tk), F32),
        grid=(nh,),
        in_specs=[pl.BlockSpec((1, 1, n_rel), lambda h: (h, 0, 0))],
        out_specs=pl.BlockSpec((1, ATT_SUB, tk), lambda h: (h, 0, 0)),
        compiler_params=_cparams("parallel"),
        name="rel_bias_tiles",
    )(rel_bias.reshape(nh, 1, n_rel))


def _attn_kernel(q_ref, kp_ref, kc_ref, vp_ref, vc_ref, bias_ref, qg_ref, kg_ref, o_ref):
    has_prev = pl.program_id(2) > 0
    tq, w = q_ref.shape
    tk = bias_ref.shape[2]
    col = lax.broadcasted_iota(jnp.int32, (ATT_SUB, tk), 1)
    head_of_lane = lax.broadcasted_iota(jnp.int32, (1, w), 1) // HEAD_DIM
    seg = (lax.broadcasted_iota(jnp.int32, (w, w), 0) // HEAD_DIM
           == lax.broadcasted_iota(jnp.int32, (w, w), 1) // HEAD_DIM)
    seg = jnp.where(seg, 1.0, 0.0).astype(BF16)

    def rms2(x, gain):
        sq = x * x
        hi = sq.astype(BF16)
        lo = (sq - hi.astype(F32)).astype(BF16)
        ss = _dot(hi, seg) + _dot(lo, seg)
        return x * lax.rsqrt(ss * (1.0 / HEAD_DIM) + RMS_EPS) * gain

    q = rms2(q_ref[...], qg_ref[...] * (HEAD_DIM ** -0.5))
    k = jnp.concatenate([rms2(kp_ref[...], kg_ref[...]), rms2(kc_ref[...], kg_ref[...])], axis=0).astype(BF16)
    v = jnp.concatenate([vp_ref[...], vc_ref[...]], axis=0).astype(BF16)
    nsub = tq // ATT_SUB
    heads = range(w // HEAD_DIM)
    qh = [jnp.where(head_of_lane == hh, q, 0.0).astype(BF16) for hh in heads]
    jobs = [(hh, i * ATT_SUB) for hh in heads for i in range(nsub)]
    sc = [_dot_nt(qh[hh][r0:r0 + ATT_SUB], k[r0:r0 + tk]) + bias_ref[hh] for hh, r0 in jobs]
    sc = [jnp.where(has_prev | (col >= ATT_LEFT - r0), s, NEG) if r0 < ATT_LEFT else s
          for s, (hh, r0) in zip(sc, jobs)]
    pr = [jnp.exp(s - jnp.max(s, axis=-1, keepdims=True)) for s in sc]
    den = [jnp.sum(p, axis=-1, keepdims=True) for p in pr]
    ob = [_dot(p.astype(BF16), v[r0:r0 + tk]) / d for p, d, (hh, r0) in zip(pr, den, jobs)]
    outs = [jnp.concatenate(ob[hh * nsub:(hh + 1) * nsub], axis=0) for hh in heads]
    o = outs[0]
    for hh in range(1, len(outs)):
        o = jnp.where(head_of_lane == hh, outs[hh], o)
    o_ref[...] = o.astype(o_ref.dtype)


def _attention(proj, bias, q_gain, k_gain, bsz, seq, d_att):
    n = bsz * seq
    tq = ATT_TQ
    nqb = seq // tq
    npair = d_att // (2 * HEAD_DIM)
    w = 2 * HEAD_DIM

    def cur(col0):
        return pl.BlockSpec((tq, w), lambda hp, b, qb: (b * nqb + qb, col0 + hp))

    def prev(col0):
        return pl.BlockSpec((tq, w), lambda hp, b, qb: (b * nqb + jnp.maximum(qb - 1, 0), col0 + hp))

    return pl.pallas_call(
        _attn_kernel,
        out_shape=jax.ShapeDtypeStruct((n, d_att), BF16),
        grid=(npair, bsz, nqb),
        in_specs=[
            cur(0), prev(npair), cur(npair), prev(2 * npair), cur(2 * npair),
            pl.BlockSpec((2, ATT_SUB, ATT_SUB + ATT_LEFT), lambda hp, b, qb: (hp, 0, 0)),
            pl.BlockSpec((1, w), lambda hp, b, qb: (0, 0)),
            pl.BlockSpec((1, w), lambda hp, b, qb: (0, 0)),
        ],
        out_specs=pl.BlockSpec((tq, w), lambda hp, b, qb: (b * nqb + qb, hp)),
        compiler_params=_cparams("parallel", "parallel", "arbitrary"),
        name="band_attention",
    )(proj, proj, proj, proj, proj, bias, jnp.tile(q_gain, 2).reshape(1, w), jnp.tile(k_gain, 2).reshape(1, w))


def _split3(x):
    hi = x.astype(BF16)
    r1 = x - hi.astype(F32)
    mid = r1.astype(BF16)
    lo = (r1 - mid.astype(F32)).astype(BF16)
    return hi, mid, lo


def _token_shift(z, carry_ref, mu):
    rows = lax.broadcasted_iota(jnp.int32, z.shape, 0)
    prev = jnp.where(rows == 0, carry_ref[...], pltpu.roll(z, 1, 0))
    carry_ref[...] = z[z.shape[0] - 1:, :]
    return z + (prev - z) * mu


def _rwkv_kernel(r_ref, k_ref, v_ref, l_ref, mur_ref, muk_ref, muv_ref, mul_ref,
                 w0_ref, a0_ref, kk_ref, ka_ref, rk_ref, lnw_ref, lnb_ref,
                 wup_ref, aup_ref, gup_ref, o_ref,
                 s_scr, cr_scr, ck_scr, cv_scr, cl_scr):
    first = pl.program_id(1) == 0
    L = r_ref.shape[0]
    d = r_ref.shape[1]
    nh = d // HEAD_DIM

    @pl.when(first)
    def _():
        s_scr[...] = jnp.zeros_like(s_scr)
        for c in (cr_scr, ck_scr, cv_scr, cl_scr):
            c[...] = jnp.zeros_like(c)

    r = _token_shift(r_ref[...], cr_scr, mur_ref[...])
    k = _token_shift(k_ref[...], ck_scr, muk_ref[...])
    v = _token_shift(v_ref[...], cv_scr, muv_ref[...])
    lo = _token_shift(l_ref[...], cl_scr, mul_ref[...])
    nw = wup_ref.shape[0]
    na = aup_ref.shape[0]
    xw, xa, xg = lo[:, :nw], lo[:, nw:nw + na], lo[:, nw + na:]

    wpre = w0_ref[...] + _bdot(jnp.tanh(xw), wup_ref[...])
    w_log = -(jnp.maximum(-wpre, 0.0) + jnp.log(1.0 + jnp.exp(-jnp.abs(wpre)))) - 0.5
    lw = -jnp.exp(w_log)
    a = _sigmoid(a0_ref[...] + _bdot(xa, aup_ref[...]))
    g = _bdot(_sigmoid(xg), gup_ref[...])

    kkf = k * kk_ref[...]
    k2 = k * (1.0 + (a - 1.0) * ka_ref[...])

    ti = lax.broadcasted_iota(jnp.int32, (L, L), 0)
    si = lax.broadcasted_iota(jnp.int32, (L, L), 1)
    tri = jnp.where(si <= ti, 1.0, 0.0).astype(BF16)
    h3 = _split3(lw)
    cum = _dot(tri, h3[0]) + _dot(tri, h3[1]) + _dot(tri, h3[2])
    e_pos = jnp.exp(cum)
    e_neg = jnp.exp(-cum)
    e_prev = jnp.exp(cum - lw)
    w_all = e_pos[L - 1:, :]

    rk = r * k2 * rk_ref[...]
    w2 = 2 * HEAD_DIM
    pairs = range(d // w2)
    psl = [slice(j * w2, (j + 1) * w2) for j in pairs]
    lane = lax.broadcasted_iota(jnp.int32, (1, w2), 1)
    in_a = lane < HEAD_DIM
    row_l = lax.broadcasted_iota(jnp.int32, (L, w2), 0)
    idx_l = lax.broadcasted_iota(jnp.int32, (L, w2), 1) % HEAD_DIM
    strict = idx_l < row_l
    incl = idx_l <= row_l
    same_head = (lax.broadcasted_iota(jnp.int32, (w2, w2), 0) // HEAD_DIM
                 == lax.broadcasted_iota(jnp.int32, (w2, w2), 1) // HEAD_DIM)
    seg = jnp.where(same_head, 1.0, 0.0).astype(BF16)

    def segsum(xs):
        x = jnp.concatenate(xs, axis=0)
        hi = x.astype(BF16)
        lo = (x - hi.astype(F32)).astype(BF16)
        tot = _dot(hi, seg) + _dot(lo, seg)
        return [tot[i * L:(i + 1) * L] for i in range(len(xs))]

    def bdiag(x):
        return jnp.concatenate([jnp.where(in_a, x, 0.0), jnp.where(in_a, 0.0, x)], axis=0)

    ar, bkc, rb, vb, vbd = [], [], [], [], []
    kss = segsum([kkf[:, sl] * kkf[:, sl] for sl in psl])
    for sl, ks in zip(psl, kss):
        kk = kkf[:, sl] / jnp.maximum(jnp.sqrt(ks), 1e-12)
        ah = -kk * e_prev[:, sl]
        bh = kk * a[:, sl] * e_neg[:, sl]
        kh = k2[:, sl] * e_neg[:, sl]
        rh = r[:, sl] * e_pos[:, sl]
        ar.append(jnp.concatenate([ah, rh], axis=0).astype(BF16))
        bkc.append(jnp.concatenate([bh, kh], axis=0).astype(BF16))
        rb.append(jnp.concatenate([bdiag(bh), bdiag(kh)], axis=0).astype(BF16))
        vb.append(v[:, sl].astype(BF16))
        vbd.append(bdiag(v[:, sl]).astype(BF16))
    s0 = [s_scr[j] for j in pairs]
    gm = [_dot_nt(ar[j], rb[j]) for j in pairs]
    ps = [_dot_nt(ar[j], s0[j].astype(BF16)) for j in pairs]
    pw = [jnp.where(strict, gm[j][:L, :w2], 0.0).astype(BF16) for j in pairs]
    u = [ps[j][:L] + _dot(jnp.where(strict, gm[j][:L, w2:], 0.0).astype(BF16), vbd[j]) for j in pairs]
    steps = max(1, (L - 1).bit_length())
    for i in range(steps):
        u = [u[j] + _dot(pw[j], bdiag(u[j]).astype(BF16)) for j in pairs]
        if i + 1 < steps:
            pw = [_dot(pw[j], bdiag(pw[j])).astype(BF16) for j in pairs]
    y = []
    for j in pairs:
        low = jnp.concatenate([jnp.where(incl, gm[j][L:, :w2], 0.0), jnp.where(incl, gm[j][L:, w2:], 0.0)], axis=1)
        uvd = jnp.concatenate([bdiag(u[j]).astype(BF16), vbd[j]], axis=0)
        y.append(ps[j][L:] + _dot(low.astype(BF16), uvd))
    for j in pairs:
        uv = jnp.concatenate([u[j].astype(BF16), vb[j]], axis=0)
        s_scr[j] = jnp.where(same_head, s0[j] + _dot_tn(uv, bkc[j]), 0.0) * w_all[:, psl[j]]
    yc = [y[j] - m * (1.0 / HEAD_DIM) for j, m in zip(pairs, segsum(y))]
    var = segsum([c * c for c in yc])
    yn = jnp.concatenate([c * lax.rsqrt(vr * (1.0 / HEAD_DIM) + GN_EPS) for c, vr in zip(yc, var)], axis=1)
    bonus = jnp.concatenate(segsum([rk[:, sl] for sl in psl]), axis=1) * v
    o_ref[...] = ((yn * lnw_ref[...] + lnb_ref[...] + bonus) * g).astype(o_ref.dtype)


def _rwkv(proj, col0, bsz, seq, d, mu, w0, w_up, a0, a_up, g_up, k_k, k_a, r_k, lnx_w, lnx_b):
    n = bsz * seq
    L = RWKV_L
    nt = seq // L
    nl = mu.shape[0] - 3 * d
    row = lambda x: x.reshape(1, -1)
    cb = col0 // d
    lb = (col0 + 3 * d) // nl

    def zspec(width, blk):
        return pl.BlockSpec((L, width), lambda b, t: (b * nt + t, blk))

    def pspec(shape):
        return pl.BlockSpec(shape, lambda b, t: (0,) * len(shape))

    nh = d // HEAD_DIM
    return pl.pallas_call(
        _rwkv_kernel,
        out_shape=jax.ShapeDtypeStruct((n, d), BF16),
        grid=(bsz, nt),
        in_specs=[
            zspec(d, cb), zspec(d, cb + 1), zspec(d, cb + 2), zspec(nl, lb),
            pspec((1, d)), pspec((1, d)), pspec((1, d)), pspec((1, nl)),
            pspec((1, d)), pspec((1, d)), pspec((1, d)), pspec((1, d)), pspec((1, d)),
            pspec((1, d)), pspec((1, d)),
            pspec(w_up.shape), pspec(a_up.shape), pspec(g_up.shape),
        ],
        out_specs=pl.BlockSpec((L, d), lambda b, t: (b * nt + t, 0)),
        scratch_shapes=[
            pltpu.VMEM((nh // 2, 2 * HEAD_DIM, 2 * HEAD_DIM), F32),
            pltpu.VMEM((1, d), F32), pltpu.VMEM((1, d), F32), pltpu.VMEM((1, d), F32),
            pltpu.VMEM((1, nl), F32),
        ],
        compiler_params=_cparams("parallel", "arbitrary"),
        name="rwkv7",
    )(proj, proj, proj, proj,
      row(mu[:d]), row(mu[d:2 * d]), row(mu[2 * d:3 * d]), row(mu[3 * d:]),
      row(w0), row(a0), row(k_k), row(k_a), row(r_k), row(lnx_w), row(lnx_b),
      w_up.astype(BF16), a_up.astype(BF16), g_up.astype(BF16))


SSM_GB = 8


def _cmul(ar, ai, br, bi):
    return ar * br - ai * bi, ar * bi + ai * br


def _s5_abar(lr, li, ldt):
    dt = jnp.exp(ldt)
    mag = jnp.exp(lr * dt)
    return mag * jnp.cos(li * dt), mag * jnp.sin(li * dt)


def _s5_prep_kernel(lrr_ref, lir_ref, ldtr_ref, lrc_ref, lic_ref, ldtc_ref,
                    bre_ref, bim_ref, cre_ref, cim_ref, k_ref, p_ref, q_ref):
    L = SSM_L
    w = k_ref.shape[2]
    lr, li = lrr_ref[0], lir_ref[0]
    a_re, a_im = _s5_abar(lr, li, ldtr_ref[0])
    den = lr * lr + li * li
    z_re = ((a_re - 1.0) * lr + a_im * li) / den
    z_im = (a_im * lr - (a_re - 1.0) * li) / den
    bb_re, bb_im = _cmul(z_re, z_im, bre_ref[0], bim_ref[0])
    cre, cim = cre_ref[0], cim_ref[0]
    ccat = jnp.concatenate([cre, -cim], axis=0)
    ac_re, ac_im = _s5_abar(lrc_ref[0], lic_ref[0], ldtc_ref[0])
    pr, pi = jnp.ones_like(a_re), jnp.zeros_like(a_im)
    qr, qi = ac_re, ac_im
    for tau in range(L):
        rb_re, rb_im = _cmul(pr, pi, bb_re, bb_im)
        ptau = jnp.concatenate([rb_re, rb_im], axis=1)
        p_ref[0, L - 1 - tau] = ptau.astype(p_ref.dtype)
        ktau = jnp.dot(ptau, ccat, preferred_element_type=F32, precision=lax.Precision.HIGHEST)
        k_ref[0, (L - 1 - tau) * w:(L - tau) * w, :] = ktau.astype(k_ref.dtype)
        q_ref[0, tau] = jnp.concatenate([cre * qr - cim * qi, -(cre * qi + cim * qr)], axis=0).astype(q_ref.dtype)
        pr, pi = _cmul(pr, pi, a_re, a_im)
        qr, qi = _cmul(qr, qi, ac_re, ac_im)


def _s5_prep(lam_re, lam_im, log_dt, b_re, b_im, c_re, c_im):
    G, P = lam_re.shape
    gs = SSM_GROUP
    gb = SSM_GB
    J = G // gb
    L = SSM_L
    eye = jnp.eye(gb, dtype=F32)
    ldt = jnp.repeat(log_dt, P)

    def bdiag_b(b):
        bt = jnp.swapaxes(b, 1, 2).reshape(J, gb, gs, P)
        return (bt[:, :, :, None, :] * eye[None, :, None, :, None]).reshape(J, gb * gs, gb * P)

    def bdiag_c(c):
        ct = jnp.swapaxes(c, 1, 2).reshape(J, gb, P, gs)
        return (ct[:, :, :, None, :] * eye[None, :, None, :, None]).reshape(J, gb * P, gb * gs)

    row = lambda x: x.reshape(J, 1, gb * P)
    col = lambda x: x.reshape(J, gb * P, 1)

    def spec(*shape):
        return pl.BlockSpec((1,) + shape, lambda j: (j,) + (0,) * len(shape))

    w, sw = gb * gs, gb * P
    return pl.pallas_call(
        _s5_prep_kernel,
        out_shape=[jax.ShapeDtypeStruct((J, L * w, w), BF16),
                   jax.ShapeDtypeStruct((J, L, w, 2 * sw), BF16),
                   jax.ShapeDtypeStruct((J, L, 2 * sw, w), BF16)],
        grid=(J,),
        in_specs=[spec(1, sw), spec(1, sw), spec(1, sw), spec(sw, 1), spec(sw, 1), spec(sw, 1),
                  spec(w, sw), spec(w, sw), spec(sw, w), spec(sw, w)],
        out_specs=[spec(L * w, w), spec(L, w, 2 * sw), spec(L, 2 * sw, w)],
        compiler_params=_cparams("parallel"),
        name="s5_prep",
    )(row(lam_re), row(lam_im), row(ldt), col(lam_re), col(lam_im), col(ldt),
      bdiag_b(b_re), bdiag_b(b_im), bdiag_c(c_re), bdiag_c(c_im))


def _s5_kernel(u_ref, k_ref, p_ref, q_ref, lr_ref, li_ref, ldt_ref, d_ref, o_ref,
               gre_scr, gim_scr, hre_scr, him_scr, y_scr):
    L = SSM_L
    nc = u_ref.shape[0] // L
    sw = lr_ref.shape[2]
    w = u_ref.shape[1]
    us = [u_ref[pl.ds(s, nc, stride=L), :].astype(BF16) for s in range(L)]
    gall = _dot(us[0], p_ref[0, 0])
    for s in range(1, L):
        gall = gall + _dot(us[s], p_ref[0, s])
    gre_scr[...] = gall[:, :sw]
    gim_scr[...] = gall[:, sw:]
    a_re, a_im = _s5_abar(lr_ref[0], li_ref[0], ldt_ref[0])
    al_re, al_im = a_re, a_im
    for _ in range(L.bit_length() - 1):
        al_re, al_im = _cmul(al_re, al_im, al_re, al_im)

    def body(c, carry):
        hre, him = carry
        hre_scr[pl.ds(c, 1), :] = hre
        him_scr[pl.ds(c, 1), :] = him
        gr = gre_scr[pl.ds(c, 1), :]
        gi = gim_scr[pl.ds(c, 1), :]
        return al_re * hre - al_im * him + gr, al_re * him + al_im * hre + gi

    zero = jnp.zeros((1, sw), F32)
    lax.fori_loop(0, nc, body, (zero, zero))
    hp = jnp.concatenate([hre_scr[...], him_scr[...]], axis=1).astype(BF16)
    ucat = jnp.concatenate(us, axis=1)
    kall = k_ref[0]
    for t in range(L):
        yt = _dot(hp, q_ref[0, t]) + _dot(ucat[:, :(t + 1) * w], kall[(L - 1 - t) * w:, :])
        y_scr[pl.ds(t, nc, stride=L), :] = yt
    y = y_scr[...] + d_ref[0] * u_ref[...]
    o_ref[...] = (0.5 * y * (1.0 + jnp.tanh(math.sqrt(2.0 / math.pi) * (y + 0.044715 * y * y * y)))).astype(o_ref.dtype)


def _s5(u, bsz, seq, mats, lam_re, lam_im, log_dt, d_skip):
    kb, pb, qb = mats
    G, P = lam_re.shape
    gb = SSM_GB
    J = G // gb
    L = SSM_L
    w = gb * SSM_GROUP
    sw = gb * P
    nc = seq // L
    row = lambda x: x.reshape(J, 1, sw)

    def jspec(*shape):
        return pl.BlockSpec((1,) + shape, lambda j, b: (j,) + (0,) * len(shape))

    return pl.pallas_call(
        _s5_kernel,
        out_shape=jax.ShapeDtypeStruct((bsz * seq, G * SSM_GROUP), BF16),
        grid=(J, bsz),
        in_specs=[
            pl.BlockSpec((seq, w), lambda j, b: (b, j)),
            jspec(L * w, w), jspec(L, w, 2 * sw), jspec(L, 2 * sw, w),
            jspec(1, sw), jspec(1, sw), jspec(1, sw), jspec(1, w),
        ],
        out_specs=pl.BlockSpec((seq, w), lambda j, b: (b, j)),
        scratch_shapes=[pltpu.VMEM((nc, sw), F32)] * 4 + [pltpu.VMEM((seq, w), F32)],
        compiler_params=_cparams("parallel", "arbitrary"),
        name="s5_ssm",
    )(u, kb, pb, qb, row(lam_re), row(lam_im), row(jnp.repeat(log_dt, P)), d_skip.reshape(J, 1, w))


def kernel(x, p, ffn1_norm, ffn1_w_gate, ffn1_w_up, ffn1_w_down, mix_norm, ffn2_norm, ffn2_w_gate, ffn2_w_up, ffn2_w_down, ple_norm, ple_w_gate, ple_w_proj, ab_w_in, att_q_gain, att_k_gain, att_rel_bias, rwkv_mu, rwkv_w0, rwkv_w_up, rwkv_a0, rwkv_a_up, rwkv_g_up, rwkv_k_k, rwkv_k_a, rwkv_r_k, rwkv_lnx_w, rwkv_lnx_b, ab_w_out, ssm_w_in, ssm_lambda_re, ssm_lambda_im, ssm_log_dt, ssm_b_re, ssm_b_im, ssm_c_re, ssm_c_im, ssm_d, ssm_w_out):
    bsz, seq, d = x.shape
    depth = p.shape[0]
    n = bsz * seq
    bf = lambda w: w.astype(BF16)
    h = x.reshape(n, d)
    w1 = [bf(w) for w in (ffn1_w_gate, ffn1_w_up, ffn1_w_down)]
    w2 = [bf(w) for w in (ffn2_w_gate, ffn2_w_up, ffn2_w_down)]
    wpl = [bf(w) for w in (ple_w_gate, ple_w_proj)]
    pe = p.reshape(depth, n, -1)
    for i in range(depth):
        j = i // 2
        h = _ffn(h, ffn1_norm[i], w1[0], w1[1], w1[2], i)
        if i % 2 == 0:
            d_att = att_rel_bias.shape[1] * HEAD_DIM
            d_rw = rwkv_w0.shape[1]
            n_in = ab_w_in.shape[2]
            tn = 1280 if n_in % 1280 == 0 else 128
            proj = _norm_matmul(h, mix_norm[i], bf(ab_w_in[j]), 2 * ROW_TILE, tn)
            bias = _bias_tiles(att_rel_bias[j])
            att = _attention(proj, bias, att_q_gain[j], att_k_gain[j], bsz, seq, d_att)
            rw = _rwkv(proj, 3 * d_att, bsz, seq, d_rw, rwkv_mu[j], rwkv_w0[j], rwkv_w_up[j],
                       rwkv_a0[j], rwkv_a_up[j], rwkv_g_up[j], rwkv_k_k[j], rwkv_k_a[j],
                       rwkv_r_k[j].reshape(-1), rwkv_lnx_w[j], rwkv_lnx_b[j])
            h = _out2(h, att, rw, bf(ab_w_out[j]))
        else:
            d_ssm = ssm_w_in.shape[2]
            u = _norm_matmul(h, mix_norm[i], bf(ssm_w_in[j]), ROW_TILE, d_ssm)
            mats = _s5_prep(ssm_lambda_re[j], ssm_lambda_im[j], ssm_log_dt[j], ssm_b_re[j],
                            ssm_b_im[j], ssm_c_re[j], ssm_c_im[j])
            y = _s5(u, bsz, seq, mats, ssm_lambda_re[j], ssm_lambda_im[j], ssm_log_dt[j], ssm_d[j])
            h = _glu_out(h, y, bf(ssm_w_out[j]))
        h = _ffn(h, ffn2_norm[i], w2[0], w2[1], w2[2], i)
        h = _ple(h, ple_norm[i], pe, wpl[0], wpl[1], i)
    return h.reshape(bsz, seq, d)
```

```python
import functools
import math

import jax
import jax.numpy as jnp
from jax import lax
from jax.experimental import pallas as pl
from jax.experimental.pallas import tpu as pltpu

F32 = jnp.float32
BF16 = jnp.bfloat16

RMS_EPS = 1e-6
GN_EPS = 64e-5
CHUNK = 64
N_LEFT_CHUNKS = 8
REL_CLIP = 128
HEAD_DIM = 64
ATT_LEFT = N_LEFT_CHUNKS * CHUNK
ATT_TQ = ATT_LEFT
ATT_SUB = 128
RWKV_L = 64
SSM_GROUP = 16
SSM_STATE = 64
SSM_L = 16
NEG = -1e30

ROW_TILE = 512
FFN_ROW_TILE = 1024
FFN_VMEM_LIMIT = 58 * 2 ** 20
COL_TILE = 512
VMEM_LIMIT = 52 * 2 ** 20


def _cparams(*sem):
    return pltpu.CompilerParams(dimension_semantics=sem, vmem_limit_bytes=VMEM_LIMIT)


def _dot(a, b):
    return jnp.dot(a, b, preferred_element_type=F32)


def _dot_nt(a, b):
    return lax.dot_general(a, b, (((1,), (1,)), ((), ())), preferred_element_type=F32)


def _dot_tn(a, b):
    return lax.dot_general(a, b, (((0,), (0,)), ((), ())), preferred_element_type=F32)


def _bdot(a, b):
    return _dot(a.astype(BF16), b.astype(BF16))


def _bdot_nt(a, b):
    return _dot_nt(a.astype(BF16), b.astype(BF16))


def _bdot_tn(a, b):
    return _dot_tn(a.astype(BF16), b.astype(BF16))


def _rms(x, g):
    return x * lax.rsqrt(jnp.mean(x * x, axis=-1, keepdims=True) + RMS_EPS) * g


def _sigmoid(x):
    return 1.0 / (1.0 + jnp.exp(-x))


def _ffn_kernel(*refs, n_cast):
    h_ref, g_ref, wg_ref, wu_ref, wd_ref = refs[:5]
    cast_in = refs[5:5 + n_cast]
    o_ref = refs[5 + n_cast]
    cast_out = refs[6 + n_cast:6 + 2 * n_cast]
    n_scr = refs[6 + 2 * n_cast]

    @pl.when(pl.program_id(1) == 0)
    def _():
        h = h_ref[...]
        n_scr[...] = _rms(h, g_ref[...]).astype(BF16)
        o_ref[...] = h

    n = n_scr[...]
    gate = _dot(n, wg_ref[...])
    up = _dot(n, wu_ref[...])
    act = (0.5 * gate * _sigmoid(gate) * up).astype(BF16)
    o_ref[...] += _dot(act, wd_ref[...])
    for src, dst in zip(cast_in, cast_out):
        dst[...] = src[...].astype(BF16)


def _ffn(h, g, w, layer, nxt=None, nxt_layer=0):
    wg, wu, wd = w
    n, d = h.shape
    dff = wg.shape[2]
    tm = min(FFN_ROW_TILE, n)
    tf = min(COL_TILE, dff)
    ni, nf = n // tm, dff // tf
    in_specs = [
        pl.BlockSpec((tm, d), lambda i, f: (i, 0)),
        pl.BlockSpec((1, d), lambda i, f: (0, 0)),
        pl.BlockSpec((None, d, tf), lambda i, f: (layer, 0, f)),
        pl.BlockSpec((None, d, tf), lambda i, f: (layer, 0, f)),
        pl.BlockSpec((None, tf, d), lambda i, f: (layer, f, 0)),
    ]
    out_shape = [jax.ShapeDtypeStruct((n, d), F32)]
    out_specs = [pl.BlockSpec((tm, d), lambda i, f: (i, 0))]
    args = [h, g.reshape(1, d), wg, wu, wd]
    if nxt is not None:
        rd, rf = d // ni, tf // ni
        up_blk, down_blk = (None, rd, tf), (None, rf, d)
        in_specs += [pl.BlockSpec(up_blk, lambda i, f: (nxt_layer, i, f)),
                     pl.BlockSpec(up_blk, lambda i, f: (nxt_layer, i, f)),
                     pl.BlockSpec(down_blk, lambda i, f: (nxt_layer, f * ni + i, 0))]
        out_specs += [pl.BlockSpec(up_blk, lambda i, f: (0, i, f)),
                      pl.BlockSpec(up_blk, lambda i, f: (0, i, f)),
                      pl.BlockSpec(down_blk, lambda i, f: (0, f * ni + i, 0))]
        out_shape += [jax.ShapeDtypeStruct((1, d, dff), BF16), jax.ShapeDtypeStruct((1, d, dff), BF16),
                      jax.ShapeDtypeStruct((1, dff, d), BF16)]
        args += list(nxt)
    outs = pl.pallas_call(
        functools.partial(_ffn_kernel, n_cast=0 if nxt is None else 3),
        out_shape=out_shape,
        grid=(ni, nf),
        in_specs=in_specs,
        out_specs=out_specs,
        scratch_shapes=[pltpu.VMEM((tm, d), BF16)],
        compiler_params=pltpu.CompilerParams(dimension_semantics=("parallel", "arbitrary"),
                                             vmem_limit_bytes=FFN_VMEM_LIMIT),
        name="ffn",
    )(*args)
    return outs[0], tuple(outs[1:])


def _nmm_kernel(x_ref, g_ref, w_ref, o_ref, n_scr):
    @pl.when(pl.program_id(1) == 0)
    def _():
        n_scr[...] = _rms(x_ref[...], g_ref[...]).astype(BF16)

    o_ref[...] = _dot(n_scr[...], w_ref[...]).astype(o_ref.dtype)


def _norm_matmul(x, g, w, tm, tn):
    n, d = x.shape
    nout = w.shape[1]
    tm = min(tm, n)
    return pl.pallas_call(
        _nmm_kernel,
        out_shape=jax.ShapeDtypeStruct((n, nout), F32),
        grid=(n // tm, nout // tn),
        in_specs=[
            pl.BlockSpec((tm, d), lambda i, j: (i, 0)),
            pl.BlockSpec((1, d), lambda i, j: (0, 0)),
            pl.BlockSpec((d, tn), lambda i, j: (0, j)),
        ],
        out_specs=pl.BlockSpec((tm, tn), lambda i, j: (i, j)),
        scratch_shapes=[pltpu.VMEM((tm, d), BF16)],
        compiler_params=_cparams("parallel", "arbitrary"),
        name="norm_matmul",
    )(x, g.reshape(1, d), w)


def _out2_kernel(res_ref, a1_ref, a2_ref, w1_ref, w2_ref, o_ref):
    o_ref[...] = res_ref[...] + _dot(a1_ref[...], w1_ref[...]) + _dot(a2_ref[...], w2_ref[...])


def _out2(res, a1, a2, w):
    n, d = res.shape
    k1 = a1.shape[1]
    tm = min(ROW_TILE, n)
    return pl.pallas_call(
        _out2_kernel,
        out_shape=jax.ShapeDtypeStruct((n, d), F32),
        grid=(n // tm,),
        in_specs=[
            pl.BlockSpec((tm, d), lambda i: (i, 0)),
            pl.BlockSpec((tm, k1), lambda i: (i, 0)),
            pl.BlockSpec((tm, k1), lambda i: (i, 0)),
            pl.BlockSpec((k1, d), lambda i: (0, 0)),
            pl.BlockSpec((k1, d), lambda i: (1, 0)),
        ],
        out_specs=pl.BlockSpec((tm, d), lambda i: (i, 0)),
        compiler_params=_cparams("parallel"),
        name="mixer_out",
    )(res, a1, a2, w, w)


def _glu_kernel(res_ref, a_ref, wa_ref, wb_ref, o_ref):
    a = a_ref[...]
    za = _dot(a, wa_ref[...])
    zb = _dot(a, wb_ref[...])
    o_ref[...] = res_ref[...] + za * _sigmoid(zb)


def _glu_out(res, a, w):
    n, d = res.shape
    k = a.shape[1]
    tm = min(ROW_TILE, n)
    return pl.pallas_call(
        _glu_kernel,
        out_shape=jax.ShapeDtypeStruct((n, d), F32),
        grid=(n // tm,),
        in_specs=[
            pl.BlockSpec((tm, d), lambda i: (i, 0)),
            pl.BlockSpec((tm, k), lambda i: (i, 0)),
            pl.BlockSpec((k, d), lambda i: (0, 0)),
            pl.BlockSpec((k, d), lambda i: (0, 1)),
        ],
        out_specs=pl.BlockSpec((tm, d), lambda i: (i, 0)),
        compiler_params=_cparams("parallel"),
        name="glu_out",
    )(res, a, w, w)


def _ple_kernel(h_ref, g_ref, p_ref, wg_ref, wp_ref, o_ref):
    h = h_ref[...]
    gate = _sigmoid(_dot(_rms(h, g_ref[...]).astype(BF16), wg_ref[...]))
    proj = _dot(p_ref[...].astype(BF16), wp_ref[...])
    o_ref[...] = h + gate * proj


def _ple(h, g, p, wg, wp, layer):
    n, d = h.shape
    dp = p.shape[2]
    tm = min(ROW_TILE, n)
    return pl.pallas_call(
        _ple_kernel,
        out_shape=jax.ShapeDtypeStruct((n, d), F32),
        grid=(n // tm,),
        in_specs=[
            pl.BlockSpec((tm, d), lambda i: (i, 0)),
            pl.BlockSpec((1, d), lambda i: (0, 0)),
            pl.BlockSpec((None, tm, dp), lambda i: (layer, i, 0)),
            pl.BlockSpec((None, d, d), lambda i: (layer, 0, 0)),
            pl.BlockSpec((None, dp, d), lambda i: (layer, 0, 0)),
        ],
        out_specs=pl.BlockSpec((tm, d), lambda i: (i, 0)),
        compiler_params=_cparams("parallel"),
        name="ple",
    )(h, g.reshape(1, d), p, wg, wp)


def _bias_kernel(tbl_ref, o_ref):
    tq = o_ref.shape[1]
    tk = o_ref.shape[2]
    n_rel = tbl_ref.shape[2]
    width = pl.next_power_of_2(tq + tk)
    n_idx = lax.broadcasted_iota(jnp.int32, (n_rel, width), 1)
    c_idx = lax.broadcasted_iota(jnp.int32, (n_rel, width), 0)
    m = jnp.where(n_idx < tk, n_idx, n_idx - width)
    idx = jnp.clip(ATT_LEFT - m, -(CHUNK - 1), REL_CLIP) + (CHUNK - 1)
    onehot = (c_idx == idx).astype(F32)
    tbl = jnp.broadcast_to(tbl_ref[0], (8, n_rel))
    ext = jnp.dot(tbl, onehot, preferred_element_type=F32, precision=lax.Precision.HIGHEST)[0:1]
    x = jnp.broadcast_to(ext, (tq, width))
    rolled = pltpu.roll(x, 0, 1, stride=1, stride_axis=0)[:, :tk]
    qc = lax.broadcasted_iota(jnp.int32, (tq, tk), 0) // CHUNK
    kc = lax.broadcasted_iota(jnp.int32, (tq, tk), 1) // CHUNK
    ok = (kc >= qc) & (kc <= qc + N_LEFT_CHUNKS)
    o_ref[0] = jnp.where(ok, rolled, NEG)


def _bias_tiles(rel_bias):
    nh, n_rel = rel_bias.shape
    tk = ATT_SUB + ATT_LEFT
    return pl.pallas_call(
        _bias_kernel,
        out_shape=jax.ShapeDtypeStruct((nh, ATT_SUB, tk), F32),
        grid=(nh,),
        in_specs=[pl.BlockSpec((1, 1, n_rel), lambda h: (h, 0, 0))],
        out_specs=pl.BlockSpec((1, ATT_SUB, tk), lambda h: (h, 0, 0)),
        compiler_params=_cparams("parallel"),
        name="rel_bias_tiles",
    )(rel_bias.reshape(nh, 1, n_rel))


def _attn_kernel(q_ref, kp_ref, kc_ref, vp_ref, vc_ref, bias_ref, qg_ref, kg_ref, o_ref):
    has_prev = pl.program_id(2) > 0
    tq, w = q_ref.shape
    tk = bias_ref.shape[2]
    col = lax.broadcasted_iota(jnp.int32, (ATT_SUB, tk), 1)
    head_of_lane = lax.broadcasted_iota(jnp.int32, (1, w), 1) // HEAD_DIM
    seg = (lax.broadcasted_iota(jnp.int32, (w, w), 0) // HEAD_DIM
           == lax.broadcasted_iota(jnp.int32, (w, w), 1) // HEAD_DIM)
    seg = jnp.where(seg, 1.0, 0.0).astype(BF16)

    def rms2(x, gain):
        sq = x * x
        hi = sq.astype(BF16)
        lo = (sq - hi.astype(F32)).astype(BF16)
        ss = _dot(hi, seg) + _dot(lo, seg)
        return x * lax.rsqrt(ss * (1.0 / HEAD_DIM) + RMS_EPS) * gain

    q = rms2(q_ref[...], qg_ref[...] * (HEAD_DIM ** -0.5))
    k = jnp.concatenate([rms2(kp_ref[...], kg_ref[...]), rms2(kc_ref[...], kg_ref[...])], axis=0).astype(BF16)
    v = jnp.concatenate([vp_ref[...], vc_ref[...]], axis=0).astype(BF16)
    nsub = tq // ATT_SUB
    heads = range(w // HEAD_DIM)
    qh = [jnp.where(head_of_lane == hh, q, 0.0).astype(BF16) for hh in heads]
    jobs = [(hh, i * ATT_SUB) for hh in heads for i in range(nsub)]
    sc = [_dot_nt(qh[hh][r0:r0 + ATT_SUB], k[r0:r0 + tk]) + bias_ref[hh] for hh, r0 in jobs]
    sc = [jnp.where(has_prev | (col >= ATT_LEFT - r0), s, NEG) if r0 < ATT_LEFT else s
          for s, (hh, r0) in zip(sc, jobs)]
    pr = [jnp.exp(s - jnp.max(s, axis=-1, keepdims=True)) for s in sc]
    den = [jnp.sum(p, axis=-1, keepdims=True) for p in pr]
    ob = [_dot(p.astype(BF16), v[r0:r0 + tk]) / d for p, d, (hh, r0) in zip(pr, den, jobs)]
    outs = [jnp.concatenate(ob[hh * nsub:(hh + 1) * nsub], axis=0) for hh in heads]
    o = outs[0]
    for hh in range(1, len(outs)):
        o = jnp.where(head_of_lane == hh, outs[hh], o)
    o_ref[...] = o.astype(o_ref.dtype)


def _attention(proj, bias, q_gain, k_gain, bsz, seq, d_att):
    n = bsz * seq
    tq = ATT_TQ
    nqb = seq // tq
    npair = d_att // (2 * HEAD_DIM)
    w = 2 * HEAD_DIM

    def cur(col0):
        return pl.BlockSpec((tq, w), lambda hp, b, qb: (b * nqb + qb, col0 + hp))

    def prev(col0):
        return pl.BlockSpec((tq, w), lambda hp, b, qb: (b * nqb + jnp.maximum(qb - 1, 0), col0 + hp))

    return pl.pallas_call(
        _attn_kernel,
        out_shape=jax.ShapeDtypeStruct((n, d_att), BF16),
        grid=(npair, bsz, nqb),
        in_specs=[
            cur(0), prev(npair), cur(npair), prev(2 * npair), cur(2 * npair),
            pl.BlockSpec((2, ATT_SUB, ATT_SUB + ATT_LEFT), lambda hp, b, qb: (hp, 0, 0)),
            pl.BlockSpec((1, w), lambda hp, b, qb: (0, 0)),
            pl.BlockSpec((1, w), lambda hp, b, qb: (0, 0)),
        ],
        out_specs=pl.BlockSpec((tq, w), lambda hp, b, qb: (b * nqb + qb, hp)),
        compiler_params=_cparams("parallel", "parallel", "arbitrary"),
        name="band_attention",
    )(proj, proj, proj, proj, proj, bias, jnp.tile(q_gain, 2).reshape(1, w), jnp.tile(k_gain, 2).reshape(1, w))


def _split3(x):
    hi = x.astype(BF16)
    r1 = x - hi.astype(F32)
    mid = r1.astype(BF16)
    lo = (r1 - mid.astype(F32)).astype(BF16)
    return hi, mid, lo


def _token_shift(z, carry_ref, mu):
    rows = lax.broadcasted_iota(jnp.int32, z.shape, 0)
    prev = jnp.where(rows == 0, carry_ref[...], pltpu.roll(z, 1, 0))
    carry_ref[...] = z[z.shape[0] - 1:, :]
    return z + (prev - z) * mu


def _rwkv_kernel(r_ref, k_ref, v_ref, l_ref, mur_ref, muk_ref, muv_ref, mul_ref,
                 w0_ref, a0_ref, kk_ref, ka_ref, rk_ref, lnw_ref, lnb_ref,
                 wup_ref, aup_ref, gup_ref, o_ref,
                 s_scr, cr_scr, ck_scr, cv_scr, cl_scr):
    first = pl.program_id(1) == 0
    L = r_ref.shape[0]
    d = r_ref.shape[1]
    nh = d // HEAD_DIM

    @pl.when(first)
    def _():
        s_scr[...] = jnp.zeros_like(s_scr)
        for c in (cr_scr, ck_scr, cv_scr, cl_scr):
            c[...] = jnp.zeros_like(c)

    r = _token_shift(r_ref[...], cr_scr, mur_ref[...])
    k = _token_shift(k_ref[...], ck_scr, muk_ref[...])
    v = _token_shift(v_ref[...], cv_scr, muv_ref[...])
    lo = _token_shift(l_ref[...], cl_scr, mul_ref[...])
    nw = wup_ref.shape[0]
    na = aup_ref.shape[0]
    xw, xa, xg = lo[:, :nw], lo[:, nw:nw + na], lo[:, nw + na:]

    wpre = w0_ref[...] + _bdot(jnp.tanh(xw), wup_ref[...])
    w_log = -(jnp.maximum(-wpre, 0.0) + jnp.log(1.0 + jnp.exp(-jnp.abs(wpre)))) - 0.5
    lw = -jnp.exp(w_log)
    a = _sigmoid(a0_ref[...] + _bdot(xa, aup_ref[...]))
    g = _bdot(_sigmoid(xg), gup_ref[...])

    kkf = k * kk_ref[...]
    k2 = k * (1.0 + (a - 1.0) * ka_ref[...])

    ti = lax.broadcasted_iota(jnp.int32, (L, L), 0)
    si = lax.broadcasted_iota(jnp.int32, (L, L), 1)
    tri = jnp.where(si <= ti, 1.0, 0.0).astype(BF16)
    h3 = _split3(lw)
    cum = _dot(tri, h3[0]) + _dot(tri, h3[1]) + _dot(tri, h3[2])
    e_pos = jnp.exp(cum)
    e_neg = jnp.exp(-cum)
    e_prev = jnp.exp(cum - lw)
    w_all = e_pos[L - 1:, :]

    rk = r * k2 * rk_ref[...]
    w2 = 2 * HEAD_DIM
    pairs = range(d // w2)
    psl = [slice(j * w2, (j + 1) * w2) for j in pairs]
    lane = lax.broadcasted_iota(jnp.int32, (1, w2), 1)
    in_a = lane < HEAD_DIM
    row_l = lax.broadcasted_iota(jnp.int32, (L, w2), 0)
    idx_l = lax.broadcasted_iota(jnp.int32, (L, w2), 1) % HEAD_DIM
    strict = idx_l < row_l
    incl = idx_l <= row_l
    same_head = (lax.broadcasted_iota(jnp.int32, (w2, w2), 0) // HEAD_DIM
                 == lax.broadcasted_iota(jnp.int32, (w2, w2), 1) // HEAD_DIM)
    seg = jnp.where(same_head, 1.0, 0.0).astype(BF16)

    def segsum(xs):
        x = jnp.concatenate(xs, axis=0)
        hi = x.astype(BF16)
        lo = (x - hi.astype(F32)).astype(BF16)
        tot = _dot(hi, seg) + _dot(lo, seg)
        return [tot[i * L:(i + 1) * L] for i in range(len(xs))]

    def bdiag(x):
        return jnp.concatenate([jnp.where(in_a, x, 0.0), jnp.where(in_a, 0.0, x)], axis=0)

    ar, bkc, rb, vb, vbd = [], [], [], [], []
    kss = segsum([kkf[:, sl] * kkf[:, sl] for sl in psl])
    for sl, ks in zip(psl, kss):
        kk = kkf[:, sl] / jnp.maximum(jnp.sqrt(ks), 1e-12)
        ah = -kk * e_prev[:, sl]
        bh = kk * a[:, sl] * e_neg[:, sl]
        kh = k2[:, sl] * e_neg[:, sl]
        rh = r[:, sl] * e_pos[:, sl]
        ar.append(jnp.concatenate([ah, rh], axis=0).astype(BF16))
        bkc.append(jnp.concatenate([bh, kh], axis=0).astype(BF16))
        rb.append(jnp.concatenate([bdiag(bh), bdiag(kh)], axis=0).astype(BF16))
        vb.append(v[:, sl].astype(BF16))
        vbd.append(bdiag(v[:, sl]).astype(BF16))
    s0 = [s_scr[j] for j in pairs]
    gm = [_dot_nt(ar[j], rb[j]) for j in pairs]
    ps = [_dot_nt(ar[j], s0[j].astype(BF16)) for j in pairs]
    pw = [jnp.where(strict, gm[j][:L, :w2], 0.0).astype(BF16) for j in pairs]
    u = [ps[j][:L] + _dot(jnp.where(strict, gm[j][:L, w2:], 0.0).astype(BF16), vbd[j]) for j in pairs]
    steps = max(1, (L - 1).bit_length())
    for i in range(steps):
        u = [u[j] + _dot(pw[j], bdiag(u[j]).astype(BF16)) for j in pairs]
        if i + 1 < steps:
            pw = [_dot(pw[j], bdiag(pw[j])).astype(BF16) for j in pairs]
    y = []
    for j in pairs:
        low = jnp.concatenate([jnp.where(incl, gm[j][L:, :w2], 0.0), jnp.where(incl, gm[j][L:, w2:], 0.0)], axis=1)
        uvd = jnp.concatenate([bdiag(u[j]).astype(BF16), vbd[j]], axis=0)
        y.append(ps[j][L:] + _dot(low.astype(BF16), uvd))
    for j in pairs:
        uv = jnp.concatenate([u[j].astype(BF16), vb[j]], axis=0)
        s_scr[j] = jnp.where(same_head, s0[j] + _dot_tn(uv, bkc[j]), 0.0) * w_all[:, psl[j]]
    yc = [y[j] - m * (1.0 / HEAD_DIM) for j, m in zip(pairs, segsum(y))]
    var = segsum([c * c for c in yc])
    yn = jnp.concatenate([c * lax.rsqrt(vr * (1.0 / HEAD_DIM) + GN_EPS) for c, vr in zip(yc, var)], axis=1)
    bonus = jnp.concatenate(segsum([rk[:, sl] for sl in psl]), axis=1) * v
    o_ref[...] = ((yn * lnw_ref[...] + lnb_ref[...] + bonus) * g).astype(o_ref.dtype)


def _rwkv(proj, col0, bsz, seq, d, mu, w0, w_up, a0, a_up, g_up, k_k, k_a, r_k, lnx_w, lnx_b):
    n = bsz * seq
    L = RWKV_L
    nt = seq // L
    nl = mu.shape[0] - 3 * d
    row = lambda x: x.reshape(1, -1)
    cb = col0 // d
    lb = (col0 + 3 * d) // nl

    def zspec(width, blk):
        return pl.BlockSpec((L, width), lambda b, t: (b * nt + t, blk))

    def pspec(shape):
        return pl.BlockSpec(shape, lambda b, t: (0,) * len(shape))

    nh = d // HEAD_DIM
    return pl.pallas_call(
        _rwkv_kernel,
        out_shape=jax.ShapeDtypeStruct((n, d), BF16),
        grid=(bsz, nt),
        in_specs=[
            zspec(d, cb), zspec(d, cb + 1), zspec(d, cb + 2), zspec(nl, lb),
            pspec((1, d)), pspec((1, d)), pspec((1, d)), pspec((1, nl)),
            pspec((1, d)), pspec((1, d)), pspec((1, d)), pspec((1, d)), pspec((1, d)),
            pspec((1, d)), pspec((1, d)),
            pspec(w_up.shape), pspec(a_up.shape), pspec(g_up.shape),
        ],
        out_specs=pl.BlockSpec((L, d), lambda b, t: (b * nt + t, 0)),
        scratch_shapes=[
            pltpu.VMEM((nh // 2, 2 * HEAD_DIM, 2 * HEAD_DIM), F32),
            pltpu.VMEM((1, d), F32), pltpu.VMEM((1, d), F32), pltpu.VMEM((1, d), F32),
            pltpu.VMEM((1, nl), F32),
        ],
        compiler_params=_cparams("parallel", "arbitrary"),
        name="rwkv7",
    )(proj, proj, proj, proj,
      row(mu[:d]), row(mu[d:2 * d]), row(mu[2 * d:3 * d]), row(mu[3 * d:]),
      row(w0), row(a0), row(k_k), row(k_a), row(r_k), row(lnx_w), row(lnx_b),
      w_up.astype(BF16), a_up.astype(BF16), g_up.astype(BF16))


SSM_GB = 8


def _cmul(ar, ai, br, bi):
    return ar * br - ai * bi, ar * bi + ai * br


def _s5_abar(lr, li, ldt):
    dt = jnp.exp(ldt)
    mag = jnp.exp(lr * dt)
    return mag * jnp.cos(li * dt), mag * jnp.sin(li * dt)


def _s5_prep_kernel(lrr_ref, lir_ref, ldtr_ref, lrc_ref, lic_ref, ldtc_ref,
                    bre_ref, bim_ref, cre_ref, cim_ref, k_ref, p_ref, q_ref):
    L = SSM_L
    w = k_ref.shape[2]
    lr, li = lrr_ref[0], lir_ref[0]
    a_re, a_im = _s5_abar(lr, li, ldtr_ref[0])
    den = lr * lr + li * li
    z_re = ((a_re - 1.0) * lr + a_im * li) / den
    z_im = (a_im * lr - (a_re - 1.0) * li) / den
    bb_re, bb_im = _cmul(z_re, z_im, bre_ref[0], bim_ref[0])
    cre, cim = cre_ref[0], cim_ref[0]
    ccat = jnp.concatenate([cre, -cim], axis=0)
    ac_re, ac_im = _s5_abar(lrc_ref[0], lic_ref[0], ldtc_ref[0])
    pr, pi = jnp.ones_like(a_re), jnp.zeros_like(a_im)
    qr, qi = ac_re, ac_im
    for tau in range(L):
        rb_re, rb_im = _cmul(pr, pi, bb_re, bb_im)
        ptau = jnp.concatenate([rb_re, rb_im], axis=1)
        p_ref[0, L - 1 - tau] = ptau.astype(p_ref.dtype)
        ktau = jnp.dot(ptau, ccat, preferred_element_type=F32, precision=lax.Precision.HIGHEST)
        k_ref[0, (L - 1 - tau) * w:(L - tau) * w, :] = ktau.astype(k_ref.dtype)
        q_ref[0, tau] = jnp.concatenate([cre * qr - cim * qi, -(cre * qi + cim * qr)], axis=0).astype(q_ref.dtype)
        pr, pi = _cmul(pr, pi, a_re, a_im)
        qr, qi = _cmul(qr, qi, ac_re, ac_im)


def _s5_prep(lam_re, lam_im, log_dt, b_re, b_im, c_re, c_im):
    G, P = lam_re.shape
    gs = SSM_GROUP
    gb = SSM_GB
    J = G // gb
    L = SSM_L
    eye = jnp.eye(gb, dtype=F32)
    ldt = jnp.repeat(log_dt, P)

    def bdiag_b(b):
        bt = jnp.swapaxes(b, 1, 2).reshape(J, gb, gs, P)
        return (bt[:, :, :, None, :] * eye[None, :, None, :, None]).reshape(J, gb * gs, gb * P)

    def bdiag_c(c):
        ct = jnp.swapaxes(c, 1, 2).reshape(J, gb, P, gs)
        return (ct[:, :, :, None, :] * eye[None, :, None, :, None]).reshape(J, gb * P, gb * gs)

    row = lambda x: x.reshape(J, 1, gb * P)
    col = lambda x: x.reshape(J, gb * P, 1)

    def spec(*shape):
        return pl.BlockSpec((1,) + shape, lambda j: (j,) + (0,) * len(shape))

    w, sw = gb * gs, gb * P
    return pl.pallas_call(
        _s5_prep_kernel,
        out_shape=[jax.ShapeDtypeStruct((J, L * w, w), BF16),
                   jax.ShapeDtypeStruct((J, L, w, 2 * sw), BF16),
                   jax.ShapeDtypeStruct((J, L, 2 * sw, w), BF16)],
        grid=(J,),
        in_specs=[spec(1, sw), spec(1, sw), spec(1, sw), spec(sw, 1), spec(sw, 1), spec(sw, 1),
                  spec(w, sw), spec(w, sw), spec(sw, w), spec(sw, w)],
        out_specs=[spec(L * w, w), spec(L, w, 2 * sw), spec(L, 2 * sw, w)],
        compiler_params=_cparams("parallel"),
        name="s5_prep",
    )(row(lam_re), row(lam_im), row(ldt), col(lam_re), col(lam_im), col(ldt),
      bdiag_b(b_re), bdiag_b(b_im), bdiag_c(c_re), bdiag_c(c_im))


def _s5_kernel(u_ref, k_ref, p_ref, q_ref, lr_ref, li_ref, ldt_ref, d_ref, o_ref,
               gre_scr, gim_scr, hre_scr, him_scr, y_scr):
    L = SSM_L
    nc = u_ref.shape[0] // L
    sw = lr_ref.shape[2]
    w = u_ref.shape[1]
    us = [u_ref[pl.ds(s, nc, stride=L), :].astype(BF16) for s in range(L)]
    gall = _dot(us[0], p_ref[0, 0])
    for s in range(1, L):
        gall = gall + _dot(us[s], p_ref[0, s])
    gre_scr[...] = gall[:, :sw]
    gim_scr[...] = gall[:, sw:]
    a_re, a_im = _s5_abar(lr_ref[0], li_ref[0], ldt_ref[0])
    al_re, al_im = a_re, a_im
    for _ in range(L.bit_length() - 1):
        al_re, al_im = _cmul(al_re, al_im, al_re, al_im)

    def body(c, carry):
        hre, him = carry
        hre_scr[pl.ds(c, 1), :] = hre
        him_scr[pl.ds(c, 1), :] = him
        gr = gre_scr[pl.ds(c, 1), :]
        gi = gim_scr[pl.ds(c, 1), :]
        return al_re * hre - al_im * him + gr, al_re * him + al_im * hre + gi

    zero = jnp.zeros((1, sw), F32)
    lax.fori_loop(0, nc, body, (zero, zero))
    hp = jnp.concatenate([hre_scr[...], him_scr[...]], axis=1).astype(BF16)
    ucat = jnp.concatenate(us, axis=1)
    kall = k_ref[0]
    for t in range(L):
        yt = _dot(hp, q_ref[0, t]) + _dot(ucat[:, :(t + 1) * w], kall[(L - 1 - t) * w:, :])
        y_scr[pl.ds(t, nc, stride=L), :] = yt
    y = y_scr[...] + d_ref[0] * u_ref[...]
    o_ref[...] = (0.5 * y * (1.0 + jnp.tanh(math.sqrt(2.0 / math.pi) * (y + 0.044715 * y * y * y)))).astype(o_ref.dtype)


def _s5(u, bsz, seq, mats, lam_re, lam_im, log_dt, d_skip):
    kb, pb, qb = mats
    G, P = lam_re.shape
    gb = SSM_GB
    J = G // gb
    L = SSM_L
    w = gb * SSM_GROUP
    sw = gb * P
    nc = seq // L
    row = lambda x: x.reshape(J, 1, sw)

    def jspec(*shape):
        return pl.BlockSpec((1,) + shape, lambda j, b: (j,) + (0,) * len(shape))

    return pl.pallas_call(
        _s5_kernel,
        out_shape=jax.ShapeDtypeStruct((bsz * seq, G * SSM_GROUP), BF16),
        grid=(J, bsz),
        in_specs=[
            pl.BlockSpec((seq, w), lambda j, b: (b, j)),
            jspec(L * w, w), jspec(L, w, 2 * sw), jspec(L, 2 * sw, w),
            jspec(1, sw), jspec(1, sw), jspec(1, sw), jspec(1, w),
        ],
        out_specs=pl.BlockSpec((seq, w), lambda j, b: (b, j)),
        scratch_shapes=[pltpu.VMEM((nc, sw), F32)] * 4 + [pltpu.VMEM((seq, w), F32)],
        compiler_params=_cparams("parallel", "arbitrary"),
        name="s5_ssm",
    )(u, kb, pb, qb, row(lam_re), row(lam_im), row(jnp.repeat(log_dt, P)), d_skip.reshape(J, 1, w))


def kernel(x, p, ffn1_norm, ffn1_w_gate, ffn1_w_up, ffn1_w_down, mix_norm, ffn2_norm, ffn2_w_gate, ffn2_w_up, ffn2_w_down, ple_norm, ple_w_gate, ple_w_proj, ab_w_in, att_q_gain, att_k_gain, att_rel_bias, rwkv_mu, rwkv_w0, rwkv_w_up, rwkv_a0, rwkv_a_up, rwkv_g_up, rwkv_k_k, rwkv_k_a, rwkv_r_k, rwkv_lnx_w, rwkv_lnx_b, ab_w_out, ssm_w_in, ssm_lambda_re, ssm_lambda_im, ssm_log_dt, ssm_b_re, ssm_b_im, ssm_c_re, ssm_c_im, ssm_d, ssm_w_out):
    bsz, seq, d = x.shape
    depth = p.shape[0]
    n = bsz * seq
    bf = lambda w: w.astype(BF16)
    h = x.reshape(n, d)
    f1 = (ffn1_w_gate, ffn1_w_up, ffn1_w_down)
    f2 = (ffn2_w_gate, ffn2_w_up, ffn2_w_down)
    wcur = tuple(bf(w[:1]) for w in f1)
    wpl = [bf(w) for w in (ple_w_gate, ple_w_proj)]
    pe = p.reshape(depth, n, -1)
    for i in range(depth):
        j = i // 2
        h, wcur = _ffn(h, ffn1_norm[i], wcur, 0, f2, i)
        if i % 2 == 0:
            d_att = att_rel_bias.shape[1] * HEAD_DIM
            d_rw = rwkv_w0.shape[1]
            n_in = ab_w_in.shape[2]
            tn = 1280 if n_in % 1280 == 0 else 128
            proj = _norm_matmul(h, mix_norm[i], bf(ab_w_in[j]), 2 * ROW_TILE, tn)
            bias = _bias_tiles(att_rel_bias[j])
            att = _attention(proj, bias, att_q_gain[j], att_k_gain[j], bsz, seq, d_att)
            rw = _rwkv(proj, 3 * d_att, bsz, seq, d_rw, rwkv_mu[j], rwkv_w0[j], rwkv_w_up[j],
                       rwkv_a0[j], rwkv_a_up[j], rwkv_g_up[j], rwkv_k_k[j], rwkv_k_a[j],
                       rwkv_r_k[j].reshape(-1), rwkv_lnx_w[j], rwkv_lnx_b[j])
            h = _out2(h, att, rw, bf(ab_w_out[j]))
        else:
            d_ssm = ssm_w_in.shape[2]
            u = _norm_matmul(h, mix_norm[i], bf(ssm_w_in[j]), ROW_TILE, d_ssm)
            mats = _s5_prep(ssm_lambda_re[j], ssm_lambda_im[j], ssm_log_dt[j], ssm_b_re[j],
                            ssm_b_im[j], ssm_c_re[j], ssm_c_im[j])
            y = _s5(u, bsz, seq, mats, ssm_lambda_re[j], ssm_lambda_im[j], ssm_log_dt[j], ssm_d[j])
            h = _glu_out(h, y, bf(ssm_w_out[j]))
        last = i + 1 == depth
        h, wcur = _ffn(h, ffn2_norm[i], wcur, 0, None if last else f1, i + 1)
        h = _ple(h, ple_norm[i], pe, wpl[0], wpl[1], i)
    return h.reshape(bsz, seq, d)
```

```python
import functools
import math

import jax
import jax.numpy as jnp
from jax import lax
from jax.experimental import pallas as pl
from jax.experimental.pallas import tpu as pltpu

F32 = jnp.float32
BF16 = jnp.bfloat16

RMS_EPS = 1e-6
GN_EPS = 64e-5
CHUNK = 64
N_LEFT_CHUNKS = 8
REL_CLIP = 128
HEAD_DIM = 64
ATT_LEFT = N_LEFT_CHUNKS * CHUNK
ATT_TQ = ATT_LEFT
ATT_SUB = 128
ATT_GROUP = 4
RWKV_L = 64
SSM_GROUP = 16
SSM_STATE = 64
SSM_L = 16
NEG = -1e30

ROW_TILE = 512
FFN_ROW_TILE = 1024
FFN_VMEM_LIMIT = 58 * 2 ** 20
COL_TILE = 512
VMEM_LIMIT = 52 * 2 ** 20


def _cparams(*sem):
    return pltpu.CompilerParams(dimension_semantics=sem, vmem_limit_bytes=VMEM_LIMIT)


def _dot(a, b):
    return jnp.dot(a, b, preferred_element_type=F32)


def _dot_nt(a, b):
    return lax.dot_general(a, b, (((1,), (1,)), ((), ())), preferred_element_type=F32)


def _dot_tn(a, b):
    return lax.dot_general(a, b, (((0,), (0,)), ((), ())), preferred_element_type=F32)


def _bdot(a, b):
    return _dot(a.astype(BF16), b.astype(BF16))


def _bdot_nt(a, b):
    return _dot_nt(a.astype(BF16), b.astype(BF16))


def _bdot_tn(a, b):
    return _dot_tn(a.astype(BF16), b.astype(BF16))


def _rms(x, g):
    return x * lax.rsqrt(jnp.mean(x * x, axis=-1, keepdims=True) + RMS_EPS) * g


def _sigmoid(x):
    return 1.0 / (1.0 + jnp.exp(-x))


def _ffn_kernel(*refs, n_cast):
    h_ref, g_ref, wg_ref, wu_ref, wd_ref = refs[:5]
    cast_in = refs[5:5 + n_cast]
    o_ref = refs[5 + n_cast]
    cast_out = refs[6 + n_cast:6 + 2 * n_cast]
    n_scr = refs[6 + 2 * n_cast]

    @pl.when(pl.program_id(1) == 0)
    def _():
        h = h_ref[...]
        n_scr[...] = _rms(h, g_ref[...]).astype(BF16)
        o_ref[...] = h

    n = n_scr[...]
    gate = _dot(n, wg_ref[...])
    up = _dot(n, wu_ref[...])
    act = (0.5 * gate * _sigmoid(gate) * up).astype(BF16)
    o_ref[...] += _dot(act, wd_ref[...])
    for src, dst in zip(cast_in, cast_out):
        dst[...] = src[...].astype(BF16)


def _ffn(h, g, w, layer, nxt=None, nxt_layer=0):
    wg, wu, wd = w
    n, d = h.shape
    dff = wg.shape[2]
    tm = min(FFN_ROW_TILE, n)
    tf = min(COL_TILE, dff)
    ni, nf = n // tm, dff // tf
    in_specs = [
        pl.BlockSpec((tm, d), lambda i, f: (i, 0)),
        pl.BlockSpec((1, d), lambda i, f: (0, 0)),
        pl.BlockSpec((None, d, tf), lambda i, f: (layer, 0, f)),
        pl.BlockSpec((None, d, tf), lambda i, f: (layer, 0, f)),
        pl.BlockSpec((None, tf, d), lambda i, f: (layer, f, 0)),
    ]
    out_shape = [jax.ShapeDtypeStruct((n, d), F32)]
    out_specs = [pl.BlockSpec((tm, d), lambda i, f: (i, 0))]
    args = [h, g.reshape(1, d), wg, wu, wd]
    if nxt is not None:
        rd, rf = d // ni, tf // ni
        up_blk, down_blk = (None, rd, tf), (None, rf, d)
        in_specs += [pl.BlockSpec(up_blk, lambda i, f: (nxt_layer, i, f)),
                     pl.BlockSpec(up_blk, lambda i, f: (nxt_layer, i, f)),
                     pl.BlockSpec(down_blk, lambda i, f: (nxt_layer, f * ni + i, 0))]
        out_specs += [pl.BlockSpec(up_blk, lambda i, f: (0, i, f)),
                      pl.BlockSpec(up_blk, lambda i, f: (0, i, f)),
                      pl.BlockSpec(down_blk, lambda i, f: (0, f * ni + i, 0))]
        out_shape += [jax.ShapeDtypeStruct((1, d, dff), BF16), jax.ShapeDtypeStruct((1, d, dff), BF16),
                      jax.ShapeDtypeStruct((1, dff, d), BF16)]
        args += list(nxt)
    outs = pl.pallas_call(
        functools.partial(_ffn_kernel, n_cast=0 if nxt is None else 3),
        out_shape=out_shape,
        grid=(ni, nf),
        in_specs=in_specs,
        out_specs=out_specs,
        scratch_shapes=[pltpu.VMEM((tm, d), BF16)],
        compiler_params=pltpu.CompilerParams(dimension_semantics=("parallel", "arbitrary"),
                                             vmem_limit_bytes=FFN_VMEM_LIMIT),
        name="ffn",
    )(*args)
    return outs[0], tuple(outs[1:])


def _cast_kernel(x_ref, o_ref):
    o_ref[...] = x_ref[...].astype(o_ref.dtype)


def _cast_layer(w, layer):
    _, r, c = w.shape
    tr = min(256, r)
    return pl.pallas_call(
        _cast_kernel,
        out_shape=jax.ShapeDtypeStruct((1, r, c), BF16),
        grid=(r // tr,),
        in_specs=[pl.BlockSpec((None, tr, c), lambda i: (layer, i, 0))],
        out_specs=pl.BlockSpec((None, tr, c), lambda i: (0, i, 0)),
        compiler_params=_cparams("parallel"),
        name="cast_layer",
    )(w)


def _nmm_kernel(x_ref, g_ref, w_ref, o_ref, n_scr):
    @pl.when(pl.program_id(1) == 0)
    def _():
        n_scr[...] = _rms(x_ref[...], g_ref[...]).astype(BF16)

    o_ref[...] = _dot(n_scr[...], w_ref[...]).astype(o_ref.dtype)


def _norm_matmul(x, g, w, tm, tn):
    n, d = x.shape
    nout = w.shape[1]
    tm = min(tm, n)
    return pl.pallas_call(
        _nmm_kernel,
        out_shape=jax.ShapeDtypeStruct((n, nout), F32),
        grid=(n // tm, nout // tn),
        in_specs=[
            pl.BlockSpec((tm, d), lambda i, j: (i, 0)),
            pl.BlockSpec((1, d), lambda i, j: (0, 0)),
            pl.BlockSpec((d, tn), lambda i, j: (0, j)),
        ],
        out_specs=pl.BlockSpec((tm, tn), lambda i, j: (i, j)),
        scratch_shapes=[pltpu.VMEM((tm, d), BF16)],
        compiler_params=_cparams("parallel", "arbitrary"),
        name="norm_matmul",
    )(x, g.reshape(1, d), w)


def _out2_kernel(res_ref, a1_ref, a2_ref, w1_ref, w2_ref, o_ref):
    o_ref[...] = res_ref[...] + _dot(a1_ref[...], w1_ref[...]) + _dot(a2_ref[...], w2_ref[...])


def _out2(res, a1, a2, w):
    n, d = res.shape
    k1 = a1.shape[1]
    tm = min(ROW_TILE, n)
    return pl.pallas_call(
        _out2_kernel,
        out_shape=jax.ShapeDtypeStruct((n, d), F32),
        grid=(n // tm,),
        in_specs=[
            pl.BlockSpec((tm, d), lambda i: (i, 0)),
            pl.BlockSpec((tm, k1), lambda i: (i, 0)),
            pl.BlockSpec((tm, k1), lambda i: (i, 0)),
            pl.BlockSpec((k1, d), lambda i: (0, 0)),
            pl.BlockSpec((k1, d), lambda i: (1, 0)),
        ],
        out_specs=pl.BlockSpec((tm, d), lambda i: (i, 0)),
        compiler_params=_cparams("parallel"),
        name="mixer_out",
    )(res, a1, a2, w, w)


def _glu_kernel(res_ref, a_ref, wa_ref, wb_ref, o_ref):
    a = a_ref[...]
    za = _dot(a, wa_ref[...])
    zb = _dot(a, wb_ref[...])
    o_ref[...] = res_ref[...] + za * _sigmoid(zb)


def _glu_out(res, a, w):
    n, d = res.shape
    k = a.shape[1]
    tm = min(ROW_TILE, n)
    return pl.pallas_call(
        _glu_kernel,
        out_shape=jax.ShapeDtypeStruct((n, d), F32),
        grid=(n // tm,),
        in_specs=[
            pl.BlockSpec((tm, d), lambda i: (i, 0)),
            pl.BlockSpec((tm, k), lambda i: (i, 0)),
            pl.BlockSpec((k, d), lambda i: (0, 0)),
            pl.BlockSpec((k, d), lambda i: (0, 1)),
        ],
        out_specs=pl.BlockSpec((tm, d), lambda i: (i, 0)),
        compiler_params=_cparams("parallel"),
        name="glu_out",
    )(res, a, w, w)


def _ple_kernel(h_ref, g_ref, p_ref, wg_ref, wp_ref, o_ref):
    h = h_ref[...]
    gate = _sigmoid(_dot(_rms(h, g_ref[...]).astype(BF16), wg_ref[...]))
    proj = _dot(p_ref[...].astype(BF16), wp_ref[...])
    o_ref[...] = h + gate * proj


def _ple(h, g, p, wg, wp, layer):
    n, d = h.shape
    dp = p.shape[2]
    tm = min(ROW_TILE, n)
    return pl.pallas_call(
        _ple_kernel,
        out_shape=jax.ShapeDtypeStruct((n, d), F32),
        grid=(n // tm,),
        in_specs=[
            pl.BlockSpec((tm, d), lambda i: (i, 0)),
            pl.BlockSpec((1, d), lambda i: (0, 0)),
            pl.BlockSpec((None, tm, dp), lambda i: (layer, i, 0)),
            pl.BlockSpec((None, d, d), lambda i: (layer, 0, 0)),
            pl.BlockSpec((None, dp, d), lambda i: (layer, 0, 0)),
        ],
        out_specs=pl.BlockSpec((tm, d), lambda i: (i, 0)),
        compiler_params=_cparams("parallel"),
        name="ple",
    )(h, g.reshape(1, d), p, wg, wp)


def _bias_kernel(tbl_ref, o_ref):
    tq = o_ref.shape[1]
    tk = o_ref.shape[2]
    n_rel = tbl_ref.shape[2]
    width = pl.next_power_of_2(tq + tk)
    n_idx = lax.broadcasted_iota(jnp.int32, (n_rel, width), 1)
    c_idx = lax.broadcasted_iota(jnp.int32, (n_rel, width), 0)
    m = jnp.where(n_idx < tk, n_idx, n_idx - width)
    idx = jnp.clip(ATT_LEFT - m, -(CHUNK - 1), REL_CLIP) + (CHUNK - 1)
    onehot = (c_idx == idx).astype(F32)
    tbl = jnp.broadcast_to(tbl_ref[0], (8, n_rel))
    ext = jnp.dot(tbl, onehot, preferred_element_type=F32, precision=lax.Precision.HIGHEST)[0:1]
    x = jnp.broadcast_to(ext, (tq, width))
    rolled = pltpu.roll(x, 0, 1, stride=1, stride_axis=0)[:, :tk]
    qc = lax.broadcasted_iota(jnp.int32, (tq, tk), 0) // CHUNK
    kc = lax.broadcasted_iota(jnp.int32, (tq, tk), 1) // CHUNK
    ok = (kc >= qc) & (kc <= qc + N_LEFT_CHUNKS)
    o_ref[0] = jnp.where(ok, rolled, NEG)


def _bias_tiles(rel_bias):
    nh, n_rel = rel_bias.shape
    tk = ATT_SUB + ATT_LEFT
    return pl.pallas_call(
        _bias_kernel,
        out_shape=jax.ShapeDtypeStruct((nh, ATT_SUB, tk), F32),
        grid=(nh,),
        in_specs=[pl.BlockSpec((1, 1, n_rel), lambda h: (h, 0, 0))],
        out_specs=pl.BlockSpec((1, ATT_SUB, tk), lambda h: (h, 0, 0)),
        compiler_params=_cparams("parallel"),
        name="rel_bias_tiles",
    )(rel_bias.reshape(nh, 1, n_rel))


def _attn_kernel(q_ref, kp_ref, kc_ref, vp_ref, vc_ref, bias_ref, qg_ref, kg_ref, o_ref):
    has_prev = pl.program_id(2) > 0
    tq, w = q_ref.shape
    tk = bias_ref.shape[2]
    col = lax.broadcasted_iota(jnp.int32, (ATT_SUB, tk), 1)
    head_of_lane = lax.broadcasted_iota(jnp.int32, (1, w), 1) // HEAD_DIM
    seg = (lax.broadcasted_iota(jnp.int32, (w, w), 0) // HEAD_DIM
           == lax.broadcasted_iota(jnp.int32, (w, w), 1) // HEAD_DIM)
    seg = jnp.where(seg, 1.0, 0.0).astype(BF16)

    def rms2(x, gain):
        sq = x * x
        hi = sq.astype(BF16)
        lo = (sq - hi.astype(F32)).astype(BF16)
        ss = _dot(hi, seg) + _dot(lo, seg)
        return x * lax.rsqrt(ss * (1.0 / HEAD_DIM) + RMS_EPS) * gain

    q = rms2(q_ref[...], qg_ref[...] * (HEAD_DIM ** -0.5))
    k = jnp.concatenate([rms2(kp_ref[...], kg_ref[...]), rms2(kc_ref[...], kg_ref[...])], axis=0).astype(BF16)
    v = jnp.concatenate([vp_ref[...], vc_ref[...]], axis=0).astype(BF16)
    nsub = tq // ATT_SUB
    heads = range(w // HEAD_DIM)
    qh = [jnp.where(head_of_lane == hh, q, 0.0).astype(BF16) for hh in heads]
    all_jobs = [(hh, i * ATT_SUB) for hh in heads for i in range(nsub)]
    ob = []
    for g0 in range(0, len(all_jobs), ATT_GROUP):
        jobs = all_jobs[g0:g0 + ATT_GROUP]
        sc = [_dot_nt(qh[hh][r0:r0 + ATT_SUB], k[r0:r0 + tk]) + bias_ref[hh] for hh, r0 in jobs]
        sc = [jnp.where(has_prev | (col >= ATT_LEFT - r0), s, NEG) if r0 < ATT_LEFT else s
              for s, (hh, r0) in zip(sc, jobs)]
        pr = [jnp.exp(s - jnp.max(s, axis=-1, keepdims=True)) for s in sc]
        den = [jnp.sum(p, axis=-1, keepdims=True) for p in pr]
        ob += [_dot(p.astype(BF16), v[r0:r0 + tk]) / d for p, d, (hh, r0) in zip(pr, den, jobs)]
    outs = [jnp.concatenate(ob[hh * nsub:(hh + 1) * nsub], axis=0) for hh in heads]
    o = outs[0]
    for hh in range(1, len(outs)):
        o = jnp.where(head_of_lane == hh, outs[hh], o)
    o_ref[...] = o.astype(o_ref.dtype)


def _attention(proj, bias, q_gain, k_gain, bsz, seq, d_att):
    n = bsz * seq
    tq = ATT_TQ
    nqb = seq // tq
    npair = d_att // (2 * HEAD_DIM)
    w = 2 * HEAD_DIM

    def cur(col0):
        return pl.BlockSpec((tq, w), lambda hp, b, qb: (b * nqb + qb, col0 + hp))

    def prev(col0):
        return pl.BlockSpec((tq, w), lambda hp, b, qb: (b * nqb + jnp.maximum(qb - 1, 0), col0 + hp))

    return pl.pallas_call(
        _attn_kernel,
        out_shape=jax.ShapeDtypeStruct((n, d_att), BF16),
        grid=(npair, bsz, nqb),
        in_specs=[
            cur(0), prev(npair), cur(npair), prev(2 * npair), cur(2 * npair),
            pl.BlockSpec((2, ATT_SUB, ATT_SUB + ATT_LEFT), lambda hp, b, qb: (hp, 0, 0)),
            pl.BlockSpec((1, w), lambda hp, b, qb: (0, 0)),
            pl.BlockSpec((1, w), lambda hp, b, qb: (0, 0)),
        ],
        out_specs=pl.BlockSpec((tq, w), lambda hp, b, qb: (b * nqb + qb, hp)),
        compiler_params=_cparams("parallel", "parallel", "arbitrary"),
        name="band_attention",
    )(proj, proj, proj, proj, proj, bias, jnp.tile(q_gain, 2).reshape(1, w), jnp.tile(k_gain, 2).reshape(1, w))


def _split3(x):
    hi = x.astype(BF16)
    r1 = x - hi.astype(F32)
    mid = r1.astype(BF16)
    lo = (r1 - mid.astype(F32)).astype(BF16)
    return hi, mid, lo


def _token_shift(z, carry_ref, mu):
    rows = lax.broadcasted_iota(jnp.int32, z.shape, 0)
    prev = jnp.where(rows == 0, carry_ref[...], pltpu.roll(z, 1, 0))
    carry_ref[...] = z[z.shape[0] - 1:, :]
    return z + (prev - z) * mu


def _rwkv_kernel(r_ref, k_ref, v_ref, l_ref, mur_ref, muk_ref, muv_ref, mul_ref,
                 w0_ref, a0_ref, kk_ref, ka_ref, rk_ref, lnw_ref, lnb_ref,
                 wup_ref, aup_ref, gup_ref, o_ref,
                 s_scr, cr_scr, ck_scr, cv_scr, cl_scr):
    first = pl.program_id(1) == 0
    L = r_ref.shape[0]
    d = r_ref.shape[1]
    nh = d // HEAD_DIM

    @pl.when(first)
    def _():
        s_scr[...] = jnp.zeros_like(s_scr)
        for c in (cr_scr, ck_scr, cv_scr, cl_scr):
            c[...] = jnp.zeros_like(c)

    r = _token_shift(r_ref[...], cr_scr, mur_ref[...])
    k = _token_shift(k_ref[...], ck_scr, muk_ref[...])
    v = _token_shift(v_ref[...], cv_scr, muv_ref[...])
    lo = _token_shift(l_ref[...], cl_scr, mul_ref[...])
    nw = wup_ref.shape[0]
    na = aup_ref.shape[0]
    xw, xa, xg = lo[:, :nw], lo[:, nw:nw + na], lo[:, nw + na:]

    wpre = w0_ref[...] + _bdot(jnp.tanh(xw), wup_ref[...])
    w_log = -(jnp.maximum(-wpre, 0.0) + jnp.log(1.0 + jnp.exp(-jnp.abs(wpre)))) - 0.5
    lw = -jnp.exp(w_log)
    a = _sigmoid(a0_ref[...] + _bdot(xa, aup_ref[...]))
    g = _bdot(_sigmoid(xg), gup_ref[...])

    kkf = k * kk_ref[...]
    k2 = k * (1.0 + (a - 1.0) * ka_ref[...])

    ti = lax.broadcasted_iota(jnp.int32, (L, L), 0)
    si = lax.broadcasted_iota(jnp.int32, (L, L), 1)
    tri = jnp.where(si <= ti, 1.0, 0.0).astype(BF16)
    h3 = _split3(lw)
    cum = _dot(tri, h3[0]) + _dot(tri, h3[1]) + _dot(tri, h3[2])
    e_pos = jnp.exp(cum)
    e_neg = jnp.exp(-cum)
    e_prev = jnp.exp(cum - lw)
    w_all = e_pos[L - 1:, :]

    rk = r * k2 * rk_ref[...]
    w2 = 2 * HEAD_DIM
    pairs = range(d // w2)
    psl = [slice(j * w2, (j + 1) * w2) for j in pairs]
    lane = lax.broadcasted_iota(jnp.int32, (1, w2), 1)
    in_a = lane < HEAD_DIM
    row_l = lax.broadcasted_iota(jnp.int32, (L, w2), 0)
    idx_l = lax.broadcasted_iota(jnp.int32, (L, w2), 1) % HEAD_DIM
    strict = idx_l < row_l
    incl = idx_l <= row_l
    same_head = (lax.broadcasted_iota(jnp.int32, (w2, w2), 0) // HEAD_DIM
                 == lax.broadcasted_iota(jnp.int32, (w2, w2), 1) // HEAD_DIM)
    seg = jnp.where(same_head, 1.0, 0.0).astype(BF16)

    def segsum(xs):
        x = jnp.concatenate(xs, axis=0)
        hi = x.astype(BF16)
        lo = (x - hi.astype(F32)).astype(BF16)
        tot = _dot(hi, seg) + _dot(lo, seg)
        return [tot[i * L:(i + 1) * L] for i in range(len(xs))]

    def bdiag(x):
        return jnp.concatenate([jnp.where(in_a, x, 0.0), jnp.where(in_a, 0.0, x)], axis=0)

    ar, bkc, rb, vb, vbd = [], [], [], [], []
    kss = segsum([kkf[:, sl] * kkf[:, sl] for sl in psl])
    for sl, ks in zip(psl, kss):
        kk = kkf[:, sl] / jnp.maximum(jnp.sqrt(ks), 1e-12)
        ah = -kk * e_prev[:, sl]
        bh = kk * a[:, sl] * e_neg[:, sl]
        kh = k2[:, sl] * e_neg[:, sl]
        rh = r[:, sl] * e_pos[:, sl]
        ar.append(jnp.concatenate([ah, rh], axis=0).astype(BF16))
        bkc.append(jnp.concatenate([bh, kh], axis=0).astype(BF16))
        rb.append(jnp.concatenate([bdiag(bh), bdiag(kh)], axis=0).astype(BF16))
        vb.append(v[:, sl].astype(BF16))
        vbd.append(bdiag(v[:, sl]).astype(BF16))
    s0 = [s_scr[j] for j in pairs]
    gm = [_dot_nt(ar[j], rb[j]) for j in pairs]
    ps = [_dot_nt(ar[j], s0[j].astype(BF16)) for j in pairs]
    pw = [jnp.where(strict, gm[j][:L, :w2], 0.0).astype(BF16) for j in pairs]
    u = [ps[j][:L] + _dot(jnp.where(strict, gm[j][:L, w2:], 0.0).astype(BF16), vbd[j]) for j in pairs]
    steps = max(1, (L - 1).bit_length())
    for i in range(steps):
        u = [u[j] + _dot(pw[j], bdiag(u[j]).astype(BF16)) for j in pairs]
        if i + 1 < steps:
            pw = [_dot(pw[j], bdiag(pw[j])).astype(BF16) for j in pairs]
    y = []
    for j in pairs:
        low = jnp.concatenate([jnp.where(incl, gm[j][L:, :w2], 0.0), jnp.where(incl, gm[j][L:, w2:], 0.0)], axis=1)
        uvd = jnp.concatenate([bdiag(u[j]).astype(BF16), vbd[j]], axis=0)
        y.append(ps[j][L:] + _dot(low.astype(BF16), uvd))
    for j in pairs:
        uv = jnp.concatenate([u[j].astype(BF16), vb[j]], axis=0)
        s_scr[j] = jnp.where(same_head, s0[j] + _dot_tn(uv, bkc[j]), 0.0) * w_all[:, psl[j]]
    yc = [y[j] - m * (1.0 / HEAD_DIM) for j, m in zip(pairs, segsum(y))]
    var = segsum([c * c for c in yc])
    yn = jnp.concatenate([c * lax.rsqrt(vr * (1.0 / HEAD_DIM) + GN_EPS) for c, vr in zip(yc, var)], axis=1)
    bonus = jnp.concatenate(segsum([rk[:, sl] for sl in psl]), axis=1) * v
    o_ref[...] = ((yn * lnw_ref[...] + lnb_ref[...] + bonus) * g).astype(o_ref.dtype)


def _rwkv(proj, col0, bsz, seq, d, mu, w0, w_up, a0, a_up, g_up, k_k, k_a, r_k, lnx_w, lnx_b):
    n = bsz * seq
    L = RWKV_L
    nt = seq // L
    nl = mu.shape[0] - 3 * d
    row = lambda x: x.reshape(1, -1)
    cb = col0 // d
    lb = (col0 + 3 * d) // nl

    def zspec(width, blk):
        return pl.BlockSpec((L, width), lambda b, t: (b * nt + t, blk))

    def pspec(shape):
        return pl.BlockSpec(shape, lambda b, t: (0,) * len(shape))

    nh = d // HEAD_DIM
    return pl.pallas_call(
        _rwkv_kernel,
        out_shape=jax.ShapeDtypeStruct((n, d), BF16),
        grid=(bsz, nt),
        in_specs=[
            zspec(d, cb), zspec(d, cb + 1), zspec(d, cb + 2), zspec(nl, lb),
            pspec((1, d)), pspec((1, d)), pspec((1, d)), pspec((1, nl)),
            pspec((1, d)), pspec((1, d)), pspec((1, d)), pspec((1, d)), pspec((1, d)),
            pspec((1, d)), pspec((1, d)),
            pspec(w_up.shape), pspec(a_up.shape), pspec(g_up.shape),
        ],
        out_specs=pl.BlockSpec((L, d), lambda b, t: (b * nt + t, 0)),
        scratch_shapes=[
            pltpu.VMEM((nh // 2, 2 * HEAD_DIM, 2 * HEAD_DIM), F32),
            pltpu.VMEM((1, d), F32), pltpu.VMEM((1, d), F32), pltpu.VMEM((1, d), F32),
            pltpu.VMEM((1, nl), F32),
        ],
        compiler_params=_cparams("parallel", "arbitrary"),
        name="rwkv7",
    )(proj, proj, proj, proj,
      row(mu[:d]), row(mu[d:2 * d]), row(mu[2 * d:3 * d]), row(mu[3 * d:]),
      row(w0), row(a0), row(k_k), row(k_a), row(r_k), row(lnx_w), row(lnx_b),
      w_up.astype(BF16), a_up.astype(BF16), g_up.astype(BF16))


SSM_GB = 8


def _cmul(ar, ai, br, bi):
    return ar * br - ai * bi, ar * bi + ai * br


def _s5_abar(lr, li, ldt):
    dt = jnp.exp(ldt)
    mag = jnp.exp(lr * dt)
    return mag * jnp.cos(li * dt), mag * jnp.sin(li * dt)


def _s5_prep_kernel(lrr_ref, lir_ref, ldtr_ref, lrc_ref, lic_ref, ldtc_ref,
                    bre_ref, bim_ref, cre_ref, cim_ref, k_ref, p_ref, q_ref):
    L = SSM_L
    w = k_ref.shape[2]
    lr, li = lrr_ref[0], lir_ref[0]
    a_re, a_im = _s5_abar(lr, li, ldtr_ref[0])
    den = lr * lr + li * li
    z_re = ((a_re - 1.0) * lr + a_im * li) / den
    z_im = (a_im * lr - (a_re - 1.0) * li) / den
    bb_re, bb_im = _cmul(z_re, z_im, bre_ref[0], bim_ref[0])
    cre, cim = cre_ref[0], cim_ref[0]
    ccat = jnp.concatenate([cre, -cim], axis=0)
    ac_re, ac_im = _s5_abar(lrc_ref[0], lic_ref[0], ldtc_ref[0])
    pr, pi = jnp.ones_like(a_re), jnp.zeros_like(a_im)
    qr, qi = ac_re, ac_im
    ptaus = []
    for tau in range(L):
        rb_re, rb_im = _cmul(pr, pi, bb_re, bb_im)
        ptau = jnp.concatenate([rb_re, rb_im], axis=1)
        ptaus.append(ptau)
        p_ref[0, L - 1 - tau] = ptau.astype(p_ref.dtype)
        q_ref[0, tau] = jnp.concatenate([cre * qr - cim * qi, -(cre * qi + cim * qr)], axis=0).astype(q_ref.dtype)
        pr, pi = _cmul(pr, pi, a_re, a_im)
        qr, qi = _cmul(qr, qi, ac_re, ac_im)
    pall = jnp.concatenate(ptaus[::-1], axis=0)
    p_hi = pall.astype(BF16)
    p_lo = (pall - p_hi.astype(F32)).astype(BF16)
    c_hi = ccat.astype(BF16)
    c_lo = (ccat - c_hi.astype(F32)).astype(BF16)
    k_ref[0] = (_dot(p_hi, c_hi) + _dot(p_lo, c_hi) + _dot(p_hi, c_lo)).astype(k_ref.dtype)


def _s5_prep(lam_re, lam_im, log_dt, b_re, b_im, c_re, c_im):
    G, P = lam_re.shape
    gs = SSM_GROUP
    gb = SSM_GB
    J = G // gb
    L = SSM_L
    eye = jnp.eye(gb, dtype=F32)
    ldt = jnp.repeat(log_dt, P)

    def bdiag_b(b):
        bt = jnp.swapaxes(b, 1, 2).reshape(J, gb, gs, P)
        return (bt[:, :, :, None, :] * eye[None, :, None, :, None]).reshape(J, gb * gs, gb * P)

    def bdiag_c(c):
        ct = jnp.swapaxes(c, 1, 2).reshape(J, gb, P, gs)
        return (ct[:, :, :, None, :] * eye[None, :, None, :, None]).reshape(J, gb * P, gb * gs)

    row = lambda x: x.reshape(J, 1, gb * P)
    col = lambda x: x.reshape(J, gb * P, 1)

    def spec(*shape):
        return pl.BlockSpec((1,) + shape, lambda j: (j,) + (0,) * len(shape))

    w, sw = gb * gs, gb * P
    return pl.pallas_call(
        _s5_prep_kernel,
        out_shape=[jax.ShapeDtypeStruct((J, L * w, w), BF16),
                   jax.ShapeDtypeStruct((J, L, w, 2 * sw), BF16),
                   jax.ShapeDtypeStruct((J, L, 2 * sw, w), BF16)],
        grid=(J,),
        in_specs=[spec(1, sw), spec(1, sw), spec(1, sw), spec(sw, 1), spec(sw, 1), spec(sw, 1),
                  spec(w, sw), spec(w, sw), spec(sw, w), spec(sw, w)],
        out_specs=[spec(L * w, w), spec(L, w, 2 * sw), spec(L, 2 * sw, w)],
        compiler_params=_cparams("parallel"),
        name="s5_prep",
    )(row(lam_re), row(lam_im), row(ldt), col(lam_re), col(lam_im), col(ldt),
      bdiag_b(b_re), bdiag_b(b_im), bdiag_c(c_re), bdiag_c(c_im))


def _s5_kernel(u_ref, k_ref, p_ref, q_ref, lr_ref, li_ref, ldt_ref, d_ref, o_ref,
               gre_scr, gim_scr, hre_scr, him_scr, y_scr):
    L = SSM_L
    nc = u_ref.shape[0] // L
    sw = lr_ref.shape[2]
    w = u_ref.shape[1]
    us = [u_ref[pl.ds(s, nc, stride=L), :].astype(BF16) for s in range(L)]
    gall = _dot(us[0], p_ref[0, 0])
    for s in range(1, L):
        gall = gall + _dot(us[s], p_ref[0, s])
    gre_scr[...] = gall[:, :sw]
    gim_scr[...] = gall[:, sw:]
    a_re, a_im = _s5_abar(lr_ref[0], li_ref[0], ldt_ref[0])
    al_re, al_im = a_re, a_im
    for _ in range(L.bit_length() - 1):
        al_re, al_im = _cmul(al_re, al_im, al_re, al_im)

    def body(c, carry):
        hre, him = carry
        hre_scr[pl.ds(c, 1), :] = hre
        him_scr[pl.ds(c, 1), :] = him
        gr = gre_scr[pl.ds(c, 1), :]
        gi = gim_scr[pl.ds(c, 1), :]
        return al_re * hre - al_im * him + gr, al_re * him + al_im * hre + gi

    zero = jnp.zeros((1, sw), F32)
    lax.fori_loop(0, nc, body, (zero, zero))
    hp = jnp.concatenate([hre_scr[...], him_scr[...]], axis=1).astype(BF16)
    ucat = jnp.concatenate(us, axis=1)
    kall = k_ref[0]
    for t in range(L):
        yt = _dot(hp, q_ref[0, t]) + _dot(ucat[:, :(t + 1) * w], kall[(L - 1 - t) * w:, :])
        y_scr[pl.ds(t, nc, stride=L), :] = yt
    y = y_scr[...] + d_ref[0] * u_ref[...]
    o_ref[...] = (0.5 * y * (1.0 + jnp.tanh(math.sqrt(2.0 / math.pi) * (y + 0.044715 * y * y * y)))).astype(o_ref.dtype)


def _s5(u, bsz, seq, mats, lam_re, lam_im, log_dt, d_skip):
    kb, pb, qb = mats
    G, P = lam_re.shape
    gb = SSM_GB
    J = G // gb
    L = SSM_L
    w = gb * SSM_GROUP
    sw = gb * P
    nc = seq // L
    row = lambda x: x.reshape(J, 1, sw)

    def jspec(*shape):
        return pl.BlockSpec((1,) + shape, lambda j, b: (j,) + (0,) * len(shape))

    return pl.pallas_call(
        _s5_kernel,
        out_shape=jax.ShapeDtypeStruct((bsz * seq, G * SSM_GROUP), BF16),
        grid=(J, bsz),
        in_specs=[
            pl.BlockSpec((seq, w), lambda j, b: (b, j)),
            jspec(L * w, w), jspec(L, w, 2 * sw), jspec(L, 2 * sw, w),
            jspec(1, sw), jspec(1, sw), jspec(1, sw), jspec(1, w),
        ],
        out_specs=pl.BlockSpec((seq, w), lambda j, b: (b, j)),
        scratch_shapes=[pltpu.VMEM((nc, sw), F32)] * 4 + [pltpu.VMEM((seq, w), F32)],
        compiler_params=_cparams("parallel", "arbitrary"),
        name="s5_ssm",
    )(u, kb, pb, qb, row(lam_re), row(lam_im), row(jnp.repeat(log_dt, P)), d_skip.reshape(J, 1, w))


def kernel(x, p, ffn1_norm, ffn1_w_gate, ffn1_w_up, ffn1_w_down, mix_norm, ffn2_norm, ffn2_w_gate, ffn2_w_up, ffn2_w_down, ple_norm, ple_w_gate, ple_w_proj, ab_w_in, att_q_gain, att_k_gain, att_rel_bias, rwkv_mu, rwkv_w0, rwkv_w_up, rwkv_a0, rwkv_a_up, rwkv_g_up, rwkv_k_k, rwkv_k_a, rwkv_r_k, rwkv_lnx_w, rwkv_lnx_b, ab_w_out, ssm_w_in, ssm_lambda_re, ssm_lambda_im, ssm_log_dt, ssm_b_re, ssm_b_im, ssm_c_re, ssm_c_im, ssm_d, ssm_w_out):
    bsz, seq, d = x.shape
    depth = p.shape[0]
    n = bsz * seq
    bf = lambda w: w.astype(BF16)
    h = x.reshape(n, d)
    f1 = (ffn1_w_gate, ffn1_w_up, ffn1_w_down)
    f2 = (ffn2_w_gate, ffn2_w_up, ffn2_w_down)
    wcur = tuple(_cast_layer(w, 0) for w in f1)
    wpl = [bf(w) for w in (ple_w_gate, ple_w_proj)]
    pe = p.reshape(depth, n, -1)
    for i in range(depth):
        j = i // 2
        h, wcur = _ffn(h, ffn1_norm[i], wcur, 0, f2, i)
        if i % 2 == 0:
            d_att = att_rel_bias.shape[1] * HEAD_DIM
            d_rw = rwkv_w0.shape[1]
            n_in = ab_w_in.shape[2]
            tn = 1280 if n_in % 1280 == 0 else 128
            proj = _norm_matmul(h, mix_norm[i], bf(ab_w_in[j]), 2 * ROW_TILE, tn)
            bias = _bias_tiles(att_rel_bias[j])
            att = _attention(proj, bias, att_q_gain[j], att_k_gain[j], bsz, seq, d_att)
            rw = _rwkv(proj, 3 * d_att, bsz, seq, d_rw, rwkv_mu[j], rwkv_w0[j], rwkv_w_up[j],
                       rwkv_a0[j], rwkv_a_up[j], rwkv_g_up[j], rwkv_k_k[j], rwkv_k_a[j],
                       rwkv_r_k[j].reshape(-1), rwkv_lnx_w[j], rwkv_lnx_b[j])
            h = _out2(h, att, rw, bf(ab_w_out[j]))
        else:
            d_ssm = ssm_w_in.shape[2]
            u = _norm_matmul(h, mix_norm[i], bf(ssm_w_in[j]), ROW_TILE, d_ssm)
            mats = _s5_prep(ssm_lambda_re[j], ssm_lambda_im[j], ssm_log_dt[j], ssm_b_re[j],
                            ssm_b_im[j], ssm_c_re[j], ssm_c_im[j])
            y = _s5(u, bsz, seq, mats, ssm_lambda_re[j], ssm_lambda_im[j], ssm_log_dt[j], ssm_d[j])
            h = _glu_out(h, y, bf(ssm_w_out[j]))
        last = i + 1 == depth
        h, wcur = _ffn(h, ffn2_norm[i], wcur, 0, None if last else f1, i + 1)
        h = _ple(h, ple_norm[i], pe, wpl[0], wpl[1], i)
    return h.reshape(bsz, seq, d)
```

```python
import functools
import math

import jax
import jax.numpy as jnp
from jax import lax
from jax.experimental import pallas as pl
from jax.experimental.pallas import tpu as pltpu

F32 = jnp.float32
BF16 = jnp.bfloat16

RMS_EPS = 1e-6
GN_EPS = 64e-5
CHUNK = 64
N_LEFT_CHUNKS = 8
REL_CLIP = 128
HEAD_DIM = 64
ATT_LEFT = N_LEFT_CHUNKS * CHUNK
ATT_TQ = ATT_LEFT
ATT_SUB = 128
ATT_GROUP = 2
RWKV_L = 64
SSM_GROUP = 16
SSM_STATE = 64
SSM_L = 16
NEG = -1e30
LOG2E = math.log2(math.e)

ROW_TILE = 512
FFN_ROW_TILE = 1024
FFN_VMEM_LIMIT = 58 * 2 ** 20
COL_TILE = 512
VMEM_LIMIT = 52 * 2 ** 20


def _cparams(*sem):
    return pltpu.CompilerParams(dimension_semantics=sem, vmem_limit_bytes=VMEM_LIMIT)


def _dot(a, b):
    return jnp.dot(a, b, preferred_element_type=F32)


def _dot_nt(a, b):
    return lax.dot_general(a, b, (((1,), (1,)), ((), ())), preferred_element_type=F32)


def _dot_tn(a, b):
    return lax.dot_general(a, b, (((0,), (0,)), ((), ())), preferred_element_type=F32)


def _bdot(a, b):
    return _dot(a.astype(BF16), b.astype(BF16))


def _bdot_nt(a, b):
    return _dot_nt(a.astype(BF16), b.astype(BF16))


def _bdot_tn(a, b):
    return _dot_tn(a.astype(BF16), b.astype(BF16))


def _rms(x, g):
    return x * lax.rsqrt(jnp.mean(x * x, axis=-1, keepdims=True) + RMS_EPS) * g


def _sigmoid(x):
    return 1.0 / (1.0 + jnp.exp(-x))


def _ffn_kernel(*refs, n_cast):
    h_ref, g_ref, wg_ref, wu_ref, wd_ref = refs[:5]
    cast_in = refs[5:5 + n_cast]
    o_ref = refs[5 + n_cast]
    cast_out = refs[6 + n_cast:6 + 2 * n_cast]
    n_scr = refs[6 + 2 * n_cast]

    @pl.when(pl.program_id(1) == 0)
    def _():
        h = h_ref[...]
        n_scr[...] = _rms(h, g_ref[...]).astype(BF16)
        o_ref[...] = h

    n = n_scr[...]
    gate = _dot(n, wg_ref[...])
    up = _dot(n, wu_ref[...])
    act = (0.5 * gate * _sigmoid(gate) * up).astype(BF16)
    o_ref[...] += _dot(act, wd_ref[...])
    for src, dst in zip(cast_in, cast_out):
        dst[...] = src[...].astype(BF16)


def _ffn(h, g, w, layer, nxt=None, nxt_layer=0):
    wg, wu, wd = w
    n, d = h.shape
    dff = wg.shape[2]
    tm = min(FFN_ROW_TILE, n)
    tf = min(COL_TILE, dff)
    ni, nf = n // tm, dff // tf
    in_specs = [
        pl.BlockSpec((tm, d), lambda i, f: (i, 0)),
        pl.BlockSpec((1, d), lambda i, f: (0, 0)),
        pl.BlockSpec((None, d, tf), lambda i, f: (layer, 0, f)),
        pl.BlockSpec((None, d, tf), lambda i, f: (layer, 0, f)),
        pl.BlockSpec((None, tf, d), lambda i, f: (layer, f, 0)),
    ]
    out_shape = [jax.ShapeDtypeStruct((n, d), F32)]
    out_specs = [pl.BlockSpec((tm, d), lambda i, f: (i, 0))]
    args = [h, g.reshape(1, d), wg, wu, wd]
    if nxt is not None:
        rd, rf = d // ni, tf // ni
        up_blk, down_blk = (None, rd, tf), (None, rf, d)
        in_specs += [pl.BlockSpec(up_blk, lambda i, f: (nxt_layer, i, f)),
                     pl.BlockSpec(up_blk, lambda i, f: (nxt_layer, i, f)),
                     pl.BlockSpec(down_blk, lambda i, f: (nxt_layer, f * ni + i, 0))]
        out_specs += [pl.BlockSpec(up_blk, lambda i, f: (0, i, f)),
                      pl.BlockSpec(up_blk, lambda i, f: (0, i, f)),
                      pl.BlockSpec(down_blk, lambda i, f: (0, f * ni + i, 0))]
        out_shape += [jax.ShapeDtypeStruct((1, d, dff), BF16), jax.ShapeDtypeStruct((1, d, dff), BF16),
                      jax.ShapeDtypeStruct((1, dff, d), BF16)]
        args += list(nxt)
    outs = pl.pallas_call(
        functools.partial(_ffn_kernel, n_cast=0 if nxt is None else 3),
        out_shape=out_shape,
        grid=(ni, nf),
        in_specs=in_specs,
        out_specs=out_specs,
        scratch_shapes=[pltpu.VMEM((tm, d), BF16)],
        compiler_params=pltpu.CompilerParams(dimension_semantics=("parallel", "arbitrary"),
                                             vmem_limit_bytes=FFN_VMEM_LIMIT),
        name="ffn",
    )(*args)
    return outs[0], tuple(outs[1:])


def _cast_kernel(x_ref, o_ref):
    o_ref[...] = x_ref[...].astype(o_ref.dtype)


def _cast_layer(w, layer):
    _, r, c = w.shape
    tr = min(256, r)
    return pl.pallas_call(
        _cast_kernel,
        out_shape=jax.ShapeDtypeStruct((1, r, c), BF16),
        grid=(r // tr,),
        in_specs=[pl.BlockSpec((None, tr, c), lambda i: (layer, i, 0))],
        out_specs=pl.BlockSpec((None, tr, c), lambda i: (0, i, 0)),
        compiler_params=_cparams("parallel"),
        name="cast_layer",
    )(w)


def _nmm_kernel(x_ref, g_ref, w_ref, o_ref, n_scr):
    @pl.when(pl.program_id(1) == 0)
    def _():
        n_scr[...] = _rms(x_ref[...], g_ref[...]).astype(BF16)

    o_ref[...] = _dot(n_scr[...], w_ref[...]).astype(o_ref.dtype)


def _norm_matmul(x, g, w, tm, tn):
    n, d = x.shape
    nout = w.shape[1]
    tm = min(tm, n)
    return pl.pallas_call(
        _nmm_kernel,
        out_shape=jax.ShapeDtypeStruct((n, nout), F32),
        grid=(n // tm, nout // tn),
        in_specs=[
            pl.BlockSpec((tm, d), lambda i, j: (i, 0)),
            pl.BlockSpec((1, d), lambda i, j: (0, 0)),
            pl.BlockSpec((d, tn), lambda i, j: (0, j)),
        ],
        out_specs=pl.BlockSpec((tm, tn), lambda i, j: (i, j)),
        scratch_shapes=[pltpu.VMEM((tm, d), BF16)],
        compiler_params=_cparams("parallel", "arbitrary"),
        name="norm_matmul",
    )(x, g.reshape(1, d), w)


def _out2_kernel(res_ref, a1_ref, a2_ref, w1_ref, w2_ref, o_ref):
    o_ref[...] = res_ref[...] + _dot(a1_ref[...], w1_ref[...]) + _dot(a2_ref[...], w2_ref[...])


def _out2(res, a1, a2, w):
    n, d = res.shape
    k1 = a1.shape[1]
    tm = min(ROW_TILE, n)
    return pl.pallas_call(
        _out2_kernel,
        out_shape=jax.ShapeDtypeStruct((n, d), F32),
        grid=(n // tm,),
        in_specs=[
            pl.BlockSpec((tm, d), lambda i: (i, 0)),
            pl.BlockSpec((tm, k1), lambda i: (i, 0)),
            pl.BlockSpec((tm, k1), lambda i: (i, 0)),
            pl.BlockSpec((k1, d), lambda i: (0, 0)),
            pl.BlockSpec((k1, d), lambda i: (1, 0)),
        ],
        out_specs=pl.BlockSpec((tm, d), lambda i: (i, 0)),
        compiler_params=_cparams("parallel"),
        name="mixer_out",
    )(res, a1, a2, w, w)


def _glu_kernel(res_ref, a_ref, wa_ref, wb_ref, o_ref):
    a = a_ref[...]
    za = _dot(a, wa_ref[...])
    zb = _dot(a, wb_ref[...])
    o_ref[...] = res_ref[...] + za * _sigmoid(zb)


def _glu_out(res, a, w):
    n, d = res.shape
    k = a.shape[1]
    tm = min(ROW_TILE, n)
    return pl.pallas_call(
        _glu_kernel,
        out_shape=jax.ShapeDtypeStruct((n, d), F32),
        grid=(n // tm,),
        in_specs=[
            pl.BlockSpec((tm, d), lambda i: (i, 0)),
            pl.BlockSpec((tm, k), lambda i: (i, 0)),
            pl.BlockSpec((k, d), lambda i: (0, 0)),
            pl.BlockSpec((k, d), lambda i: (0, 1)),
        ],
        out_specs=pl.BlockSpec((tm, d), lambda i: (i, 0)),
        compiler_params=_cparams("parallel"),
        name="glu_out",
    )(res, a, w, w)


def _ple_kernel(h_ref, g_ref, p_ref, wg_ref, wp_ref, o_ref):
    h = h_ref[...]
    gate = _sigmoid(_dot(_rms(h, g_ref[...]).astype(BF16), wg_ref[...]))
    proj = _dot(p_ref[...].astype(BF16), wp_ref[...])
    o_ref[...] = h + gate * proj


def _ple(h, g, p, wg, wp, layer):
    n, d = h.shape
    dp = p.shape[2]
    tm = min(ROW_TILE, n)
    return pl.pallas_call(
        _ple_kernel,
        out_shape=jax.ShapeDtypeStruct((n, d), F32),
        grid=(n // tm,),
        in_specs=[
            pl.BlockSpec((tm, d), lambda i: (i, 0)),
            pl.BlockSpec((1, d), lambda i: (0, 0)),
            pl.BlockSpec((None, tm, dp), lambda i: (layer, i, 0)),
            pl.BlockSpec((None, d, d), lambda i: (layer, 0, 0)),
            pl.BlockSpec((None, dp, d), lambda i: (layer, 0, 0)),
        ],
        out_specs=pl.BlockSpec((tm, d), lambda i: (i, 0)),
        compiler_params=_cparams("parallel"),
        name="ple",
    )(h, g.reshape(1, d), p, wg, wp)


def _bias_kernel(tbl_ref, o_ref):
    tq = o_ref.shape[1]
    tk = o_ref.shape[2]
    n_rel = tbl_ref.shape[2]
    width = pl.next_power_of_2(tq + tk)
    n_idx = lax.broadcasted_iota(jnp.int32, (n_rel, width), 1)
    c_idx = lax.broadcasted_iota(jnp.int32, (n_rel, width), 0)
    m = jnp.where(n_idx < tk, n_idx, n_idx - width)
    idx = jnp.clip(ATT_LEFT - m, -(CHUNK - 1), REL_CLIP) + (CHUNK - 1)
    onehot = (c_idx == idx).astype(F32)
    tbl = jnp.broadcast_to(tbl_ref[0], (8, n_rel))
    ext = jnp.dot(tbl, onehot, preferred_element_type=F32, precision=lax.Precision.HIGHEST)[0:1]
    x = jnp.broadcast_to(ext, (tq, width))
    rolled = pltpu.roll(x, 0, 1, stride=1, stride_axis=0)[:, :tk]
    qc = lax.broadcasted_iota(jnp.int32, (tq, tk), 0) // CHUNK
    kc = lax.broadcasted_iota(jnp.int32, (tq, tk), 1) // CHUNK
    ok = (kc >= qc) & (kc <= qc + N_LEFT_CHUNKS)
    o_ref[0] = jnp.where(ok, rolled * LOG2E, NEG)


def _bias_tiles(rel_bias):
    nh, n_rel = rel_bias.shape
    tk = ATT_SUB + ATT_LEFT
    return pl.pallas_call(
        _bias_kernel,
        out_shape=jax.ShapeDtypeStruct((nh, ATT_SUB, tk), F32),
        grid=(nh,),
        in_specs=[pl.BlockSpec((1, 1, n_rel), lambda h: (h, 0, 0))],
        out_specs=pl.BlockSpec((1, ATT_SUB, tk), lambda h: (h, 0, 0)),
        compiler_params=_cparams("parallel"),
        name="rel_bias_tiles",
    )(rel_bias.reshape(nh, 1, n_rel))


def _attn_kernel(q_ref, kp_ref, kc_ref, vp_ref, vc_ref, bias_ref, qg_ref, kg_ref, o_ref):
    has_prev = pl.program_id(2) > 0
    tq, w = q_ref.shape
    tk = bias_ref.shape[2]
    col = lax.broadcasted_iota(jnp.int32, (w // HEAD_DIM * ATT_SUB, tk), 1)
    head_of_lane = lax.broadcasted_iota(jnp.int32, (1, w), 1) // HEAD_DIM
    seg = (lax.broadcasted_iota(jnp.int32, (w, w), 0) // HEAD_DIM
           == lax.broadcasted_iota(jnp.int32, (w, w), 1) // HEAD_DIM)
    seg = jnp.where(seg, 1.0, 0.0).astype(BF16)

    def rms2(x, gain):
        sq = x * x
        hi = sq.astype(BF16)
        lo = (sq - hi.astype(F32)).astype(BF16)
        ss = _dot(jnp.concatenate([hi, lo], axis=1), jnp.concatenate([seg, seg], axis=0))
        return x * lax.rsqrt(ss * (1.0 / HEAD_DIM) + RMS_EPS) * gain

    q = rms2(q_ref[...], qg_ref[...] * (HEAD_DIM ** -0.5 * LOG2E))
    k = jnp.concatenate([rms2(kp_ref[...], kg_ref[...]), rms2(kc_ref[...], kg_ref[...])], axis=0).astype(BF16)
    v = jnp.concatenate([vp_ref[...], vc_ref[...]], axis=0).astype(BF16)
    nh = w // HEAD_DIM
    qh = [jnp.where(head_of_lane == hh, q, 0.0).astype(BF16) for hh in range(nh)]
    bias = jnp.concatenate([bias_ref[hh] for hh in range(nh)], axis=0)
    starts = [i * ATT_SUB for i in range(tq // ATT_SUB)]
    ob = []
    for g0 in range(0, len(starts), ATT_GROUP):
        jobs = starts[g0:g0 + ATT_GROUP]
        sc = [_dot_nt(jnp.concatenate([x[r0:r0 + ATT_SUB] for x in qh], axis=0), k[r0:r0 + tk]) + bias
              for r0 in jobs]
        sc = [jnp.where(has_prev | (col >= ATT_LEFT - r0), s, NEG) if r0 < ATT_LEFT else s
              for s, r0 in zip(sc, jobs)]
        pr = [jnp.exp2(s - jnp.max(s, axis=-1, keepdims=True)) for s in sc]
        den = [jnp.sum(p, axis=-1, keepdims=True) for p in pr]
        ob += [_dot(p.astype(BF16), v[r0:r0 + tk]) / d for p, d, r0 in zip(pr, den, jobs)]
    outs = []
    for o2 in ob:
        o = o2[:ATT_SUB]
        for hh in range(1, nh):
            o = jnp.where(head_of_lane == hh, o2[hh * ATT_SUB:(hh + 1) * ATT_SUB], o)
        outs.append(o)
    o_ref[...] = jnp.concatenate(outs, axis=0).astype(o_ref.dtype)


def _attention(proj, bias, q_gain, k_gain, bsz, seq, d_att):
    n = bsz * seq
    tq = ATT_TQ
    nqb = seq // tq
    npair = d_att // (2 * HEAD_DIM)
    w = 2 * HEAD_DIM

    def cur(col0):
        return pl.BlockSpec((tq, w), lambda hp, b, qb: (b * nqb + qb, col0 + hp))

    def prev(col0):
        return pl.BlockSpec((tq, w), lambda hp, b, qb: (b * nqb + jnp.maximum(qb - 1, 0), col0 + hp))

    return pl.pallas_call(
        _attn_kernel,
        out_shape=jax.ShapeDtypeStruct((n, d_att), BF16),
        grid=(npair, bsz, nqb),
        in_specs=[
            cur(0), prev(npair), cur(npair), prev(2 * npair), cur(2 * npair),
            pl.BlockSpec((2, ATT_SUB, ATT_SUB + ATT_LEFT), lambda hp, b, qb: (hp, 0, 0)),
            pl.BlockSpec((1, w), lambda hp, b, qb: (0, 0)),
            pl.BlockSpec((1, w), lambda hp, b, qb: (0, 0)),
        ],
        out_specs=pl.BlockSpec((tq, w), lambda hp, b, qb: (b * nqb + qb, hp)),
        compiler_params=_cparams("parallel", "parallel", "arbitrary"),
        name="band_attention",
    )(proj, proj, proj, proj, proj, bias, jnp.tile(q_gain, 2).reshape(1, w), jnp.tile(k_gain, 2).reshape(1, w))


def _split3(x):
    hi = x.astype(BF16)
    r1 = x - hi.astype(F32)
    mid = r1.astype(BF16)
    lo = (r1 - mid.astype(F32)).astype(BF16)
    return hi, mid, lo


def _token_shift(z, carry_ref, mu):
    rows = lax.broadcasted_iota(jnp.int32, z.shape, 0)
    prev = jnp.where(rows == 0, carry_ref[...], pltpu.roll(z, 1, 0))
    carry_ref[...] = z[z.shape[0] - 1:, :]
    return z + (prev - z) * mu


def _rwkv_kernel(r_ref, k_ref, v_ref, l_ref, mur_ref, muk_ref, muv_ref, mul_ref,
                 w0_ref, a0_ref, kk_ref, ka_ref, rk_ref, lnw_ref, lnb_ref,
                 wup_ref, aup_ref, gup_ref, o_ref,
                 s_scr, cr_scr, ck_scr, cv_scr, cl_scr):
    first = pl.program_id(1) == 0
    L = r_ref.shape[0]
    d = r_ref.shape[1]
    nh = d // HEAD_DIM

    @pl.when(first)
    def _():
        s_scr[...] = jnp.zeros_like(s_scr)
        for c in (cr_scr, ck_scr, cv_scr, cl_scr):
            c[...] = jnp.zeros_like(c)

    r = _token_shift(r_ref[...], cr_scr, mur_ref[...])
    k = _token_shift(k_ref[...], ck_scr, muk_ref[...])
    v = _token_shift(v_ref[...], cv_scr, muv_ref[...])
    lo = _token_shift(l_ref[...], cl_scr, mul_ref[...])
    nw = wup_ref.shape[0]
    na = aup_ref.shape[0]
    xw, xa, xg = lo[:, :nw], lo[:, nw:nw + na], lo[:, nw + na:]

    wpre = w0_ref[...] + _bdot(jnp.tanh(xw), wup_ref[...])
    w_log = -(jnp.maximum(-wpre, 0.0) + jnp.log(1.0 + jnp.exp(-jnp.abs(wpre)))) - 0.5
    lw = -jnp.exp(w_log)
    a = _sigmoid(a0_ref[...] + _bdot(xa, aup_ref[...]))
    g = _bdot(_sigmoid(xg), gup_ref[...])

    kkf = k * kk_ref[...]
    k2 = k * (1.0 + (a - 1.0) * ka_ref[...])

    ti = lax.broadcasted_iota(jnp.int32, (L, L), 0)
    si = lax.broadcasted_iota(jnp.int32, (L, L), 1)
    tri = jnp.where(si <= ti, 1.0, 0.0).astype(BF16)
    h3 = _split3(lw)
    cum = _dot(jnp.concatenate([tri, tri, tri], axis=1), jnp.concatenate(h3, axis=0))
    e_pos = jnp.exp(cum)
    e_neg = jnp.exp(-cum)
    e_prev = jnp.exp(cum - lw)
    w_all = e_pos[L - 1:, :]

    rk = r * k2 * rk_ref[...]
    w2 = 2 * HEAD_DIM
    pairs = range(d // w2)
    psl = [slice(j * w2, (j + 1) * w2) for j in pairs]
    lane = lax.broadcasted_iota(jnp.int32, (1, w2), 1)
    in_a = lane < HEAD_DIM
    row_l = lax.broadcasted_iota(jnp.int32, (L, w2), 0)
    idx_l = lax.broadcasted_iota(jnp.int32, (L, w2), 1) % HEAD_DIM
    strict = idx_l < row_l
    incl = idx_l <= row_l
    same_head = (lax.broadcasted_iota(jnp.int32, (w2, w2), 0) // HEAD_DIM
                 == lax.broadcasted_iota(jnp.int32, (w2, w2), 1) // HEAD_DIM)
    seg = jnp.where(same_head, 1.0, 0.0).astype(BF16)

    def segsum(xs):
        x = jnp.concatenate(xs, axis=0)
        hi = x.astype(BF16)
        lo = (x - hi.astype(F32)).astype(BF16)
        tot = _dot(jnp.concatenate([hi, lo], axis=1), jnp.concatenate([seg, seg], axis=0))
        return [tot[i * L:(i + 1) * L] for i in range(len(xs))]

    def bdiag(x):
        return jnp.concatenate([jnp.where(in_a, x, 0.0), jnp.where(in_a, 0.0, x)], axis=0)

    ar, bkc, rb, vb, vbd = [], [], [], [], []
    kss = segsum([kkf[:, sl] * kkf[:, sl] for sl in psl])
    for sl, ks in zip(psl, kss):
        kk = kkf[:, sl] / jnp.maximum(jnp.sqrt(ks), 1e-12)
        ah = -kk * e_prev[:, sl]
        bh = kk * a[:, sl] * e_neg[:, sl]
        kh = k2[:, sl] * e_neg[:, sl]
        rh = r[:, sl] * e_pos[:, sl]
        ar.append(jnp.concatenate([ah, rh], axis=0).astype(BF16))
        bkc.append(jnp.concatenate([bh, kh], axis=0).astype(BF16))
        rb.append(jnp.concatenate([bdiag(bh), bdiag(kh)], axis=0).astype(BF16))
        vb.append(v[:, sl].astype(BF16))
        vbd.append(bdiag(v[:, sl]).astype(BF16))
    s0 = [s_scr[j] for j in pairs]
    gm = [_dot_nt(ar[j], rb[j]) for j in pairs]
    ps = [_dot_nt(ar[j], s0[j].astype(BF16)) for j in pairs]
    pw = [jnp.where(strict, gm[j][:L, :w2], 0.0).astype(BF16) for j in pairs]
    u = [ps[j][:L] + _dot(jnp.where(strict, gm[j][:L, w2:], 0.0).astype(BF16), vbd[j]) for j in pairs]
    steps = max(1, (L - 1).bit_length())
    for i in range(steps):
        u = [u[j] + _dot(pw[j], bdiag(u[j]).astype(BF16)) for j in pairs]
        if i + 1 < steps:
            pw = [_dot(pw[j], bdiag(pw[j])).astype(BF16) for j in pairs]
    y = []
    for j in pairs:
        low = jnp.concatenate([jnp.where(incl, gm[j][L:, :w2], 0.0), jnp.where(incl, gm[j][L:, w2:], 0.0)], axis=1)
        uvd = jnp.concatenate([bdiag(u[j]).astype(BF16), vbd[j]], axis=0)
        y.append(ps[j][L:] + _dot(low.astype(BF16), uvd))
    for j in pairs:
        uv = jnp.concatenate([u[j].astype(BF16), vb[j]], axis=0)
        s_scr[j] = jnp.where(same_head, s0[j] + _dot_tn(uv, bkc[j]), 0.0) * w_all[:, psl[j]]
    yc = [y[j] - m * (1.0 / HEAD_DIM) for j, m in zip(pairs, segsum(y))]
    var = segsum([c * c for c in yc])
    yn = jnp.concatenate([c * lax.rsqrt(vr * (1.0 / HEAD_DIM) + GN_EPS) for c, vr in zip(yc, var)], axis=1)
    bonus = jnp.concatenate(segsum([rk[:, sl] for sl in psl]), axis=1) * v
    o_ref[...] = ((yn * lnw_ref[...] + lnb_ref[...] + bonus) * g).astype(o_ref.dtype)


def _rwkv(proj, col0, bsz, seq, d, mu, w0, w_up, a0, a_up, g_up, k_k, k_a, r_k, lnx_w, lnx_b):
    n = bsz * seq
    L = RWKV_L
    nt = seq // L
    nl = mu.shape[0] - 3 * d
    row = lambda x: x.reshape(1, -1)
    cb = col0 // d
    lb = (col0 + 3 * d) // nl

    def zspec(width, blk):
        return pl.BlockSpec((L, width), lambda b, t: (b * nt + t, blk))

    def pspec(shape):
        return pl.BlockSpec(shape, lambda b, t: (0,) * len(shape))

    nh = d // HEAD_DIM
    return pl.pallas_call(
        _rwkv_kernel,
        out_shape=jax.ShapeDtypeStruct((n, d), BF16),
        grid=(bsz, nt),
        in_specs=[
            zspec(d, cb), zspec(d, cb + 1), zspec(d, cb + 2), zspec(nl, lb),
            pspec((1, d)), pspec((1, d)), pspec((1, d)), pspec((1, nl)),
            pspec((1, d)), pspec((1, d)), pspec((1, d)), pspec((1, d)), pspec((1, d)),
            pspec((1, d)), pspec((1, d)),
            pspec(w_up.shape), pspec(a_up.shape), pspec(g_up.shape),
        ],
        out_specs=pl.BlockSpec((L, d), lambda b, t: (b * nt + t, 0)),
        scratch_shapes=[
            pltpu.VMEM((nh // 2, 2 * HEAD_DIM, 2 * HEAD_DIM), F32),
            pltpu.VMEM((1, d), F32), pltpu.VMEM((1, d), F32), pltpu.VMEM((1, d), F32),
            pltpu.VMEM((1, nl), F32),
        ],
        compiler_params=_cparams("parallel", "arbitrary"),
        name="rwkv7",
    )(proj, proj, proj, proj,
      row(mu[:d]), row(mu[d:2 * d]), row(mu[2 * d:3 * d]), row(mu[3 * d:]),
      row(w0), row(a0), row(k_k), row(k_a), row(r_k), row(lnx_w), row(lnx_b),
      w_up.astype(BF16), a_up.astype(BF16), g_up.astype(BF16))


SSM_GB = 8


def _cmul(ar, ai, br, bi):
    return ar * br - ai * bi, ar * bi + ai * br


def _s5_abar(lr, li, ldt):
    dt = jnp.exp(ldt)
    mag = jnp.exp(lr * dt)
    return mag * jnp.cos(li * dt), mag * jnp.sin(li * dt)


def _s5_prep_kernel(lrr_ref, lir_ref, ldtr_ref, lrc_ref, lic_ref, ldtc_ref,
                    bre_ref, bim_ref, cre_ref, cim_ref, k_ref, p_ref, q_ref):
    L = SSM_L
    w = k_ref.shape[2]
    lr, li = lrr_ref[0], lir_ref[0]
    a_re, a_im = _s5_abar(lr, li, ldtr_ref[0])
    den = lr * lr + li * li
    z_re = ((a_re - 1.0) * lr + a_im * li) / den
    z_im = (a_im * lr - (a_re - 1.0) * li) / den
    bb_re, bb_im = _cmul(z_re, z_im, bre_ref[0], bim_ref[0])
    cre, cim = cre_ref[0], cim_ref[0]
    ccat = jnp.concatenate([cre, -cim], axis=0)
    ac_re, ac_im = _s5_abar(lrc_ref[0], lic_ref[0], ldtc_ref[0])
    pr, pi = jnp.ones_like(a_re), jnp.zeros_like(a_im)
    qr, qi = ac_re, ac_im
    ptaus = []
    for tau in range(L):
        rb_re, rb_im = _cmul(pr, pi, bb_re, bb_im)
        ptau = jnp.concatenate([rb_re, rb_im], axis=1)
        ptaus.append(ptau)
        p_ref[0, (L - 1 - tau) * w:(L - tau) * w, :] = ptau.astype(p_ref.dtype)
        q_ref[0, :, tau * w:(tau + 1) * w] = jnp.concatenate(
            [cre * qr - cim * qi, -(cre * qi + cim * qr)], axis=0).astype(q_ref.dtype)
        pr, pi = _cmul(pr, pi, a_re, a_im)
        qr, qi = _cmul(qr, qi, ac_re, ac_im)
    pall = jnp.concatenate(ptaus[::-1], axis=0)
    p_hi = pall.astype(BF16)
    p_lo = (pall - p_hi.astype(F32)).astype(BF16)
    c_hi = ccat.astype(BF16)
    c_lo = (ccat - c_hi.astype(F32)).astype(BF16)
    k_ref[0] = (_dot(p_hi, c_hi) + _dot(p_lo, c_hi) + _dot(p_hi, c_lo)).astype(k_ref.dtype)


def _s5_prep(lam_re, lam_im, log_dt, b_re, b_im, c_re, c_im):
    G, P = lam_re.shape
    gs = SSM_GROUP
    gb = SSM_GB
    J = G // gb
    L = SSM_L
    eye = jnp.eye(gb, dtype=F32)
    ldt = jnp.repeat(log_dt, P)

    def bdiag_b(b):
        bt = jnp.swapaxes(b, 1, 2).reshape(J, gb, gs, P)
        return (bt[:, :, :, None, :] * eye[None, :, None, :, None]).reshape(J, gb * gs, gb * P)

    def bdiag_c(c):
        ct = jnp.swapaxes(c, 1, 2).reshape(J, gb, P, gs)
        return (ct[:, :, :, None, :] * eye[None, :, None, :, None]).reshape(J, gb * P, gb * gs)

    row = lambda x: x.reshape(J, 1, gb * P)
    col = lambda x: x.reshape(J, gb * P, 1)

    def spec(*shape):
        return pl.BlockSpec((1,) + shape, lambda j: (j,) + (0,) * len(shape))

    w, sw = gb * gs, gb * P
    return pl.pallas_call(
        _s5_prep_kernel,
        out_shape=[jax.ShapeDtypeStruct((J, L * w, w), BF16),
                   jax.ShapeDtypeStruct((J, L * w, 2 * sw), BF16),
                   jax.ShapeDtypeStruct((J, 2 * sw, L * w), BF16)],
        grid=(J,),
        in_specs=[spec(1, sw), spec(1, sw), spec(1, sw), spec(sw, 1), spec(sw, 1), spec(sw, 1),
                  spec(w, sw), spec(w, sw), spec(sw, w), spec(sw, w)],
        out_specs=[spec(L * w, w), spec(L * w, 2 * sw), spec(2 * sw, L * w)],
        compiler_params=_cparams("parallel"),
        name="s5_prep",
    )(row(lam_re), row(lam_im), row(ldt), col(lam_re), col(lam_im), col(ldt),
      bdiag_b(b_re), bdiag_b(b_im), bdiag_c(c_re), bdiag_c(c_im))


def _s5_kernel(u_ref, k_ref, p_ref, q_ref, lr_ref, li_ref, ldt_ref, d_ref, o_ref,
               gre_scr, gim_scr, hre_scr, him_scr, y_scr):
    L = SSM_L
    nc = u_ref.shape[0] // L
    sw = lr_ref.shape[2]
    w = u_ref.shape[1]
    us = [u_ref[pl.ds(s, nc, stride=L), :].astype(BF16) for s in range(L)]
    ucat = jnp.concatenate(us, axis=1)
    gall = _dot(ucat, p_ref[0])
    gre_scr[...] = gall[:, :sw]
    gim_scr[...] = gall[:, sw:]
    a_re, a_im = _s5_abar(lr_ref[0], li_ref[0], ldt_ref[0])
    al_re, al_im = a_re, a_im
    for _ in range(L.bit_length() - 1):
        al_re, al_im = _cmul(al_re, al_im, al_re, al_im)

    def body(c, carry):
        hre, him = carry
        hre_scr[pl.ds(c, 1), :] = hre
        him_scr[pl.ds(c, 1), :] = him
        gr = gre_scr[pl.ds(c, 1), :]
        gi = gim_scr[pl.ds(c, 1), :]
        return al_re * hre - al_im * him + gr, al_re * him + al_im * hre + gi

    zero = jnp.zeros((1, sw), F32)
    lax.fori_loop(0, nc, body, (zero, zero))
    hp = jnp.concatenate([hre_scr[...], him_scr[...]], axis=1).astype(BF16)
    ycarry = _dot(hp, q_ref[0])
    kall = k_ref[0]
    for t in range(L):
        yt = ycarry[:, t * w:(t + 1) * w] + _dot(ucat[:, :(t + 1) * w], kall[(L - 1 - t) * w:, :])
        y_scr[pl.ds(t, nc, stride=L), :] = yt
    y = y_scr[...] + d_ref[0] * u_ref[...]
    o_ref[...] = (0.5 * y * (1.0 + jnp.tanh(math.sqrt(2.0 / math.pi) * (y + 0.044715 * y * y * y)))).astype(o_ref.dtype)


def _s5(u, bsz, seq, mats, lam_re, lam_im, log_dt, d_skip):
    kb, pb, qb = mats
    G, P = lam_re.shape
    gb = SSM_GB
    J = G // gb
    L = SSM_L
    w = gb * SSM_GROUP
    sw = gb * P
    nc = seq // L
    row = lambda x: x.reshape(J, 1, sw)

    def jspec(*shape):
        return pl.BlockSpec((1,) + shape, lambda j, b: (j,) + (0,) * len(shape))

    return pl.pallas_call(
        _s5_kernel,
        out_shape=jax.ShapeDtypeStruct((bsz * seq, G * SSM_GROUP), BF16),
        grid=(J, bsz),
        in_specs=[
            pl.BlockSpec((seq, w), lambda j, b: (b, j)),
            jspec(L * w, w), jspec(L * w, 2 * sw), jspec(2 * sw, L * w),
            jspec(1, sw), jspec(1, sw), jspec(1, sw), jspec(1, w),
        ],
        out_specs=pl.BlockSpec((seq, w), lambda j, b: (b, j)),
        scratch_shapes=[pltpu.VMEM((nc, sw), F32)] * 4 + [pltpu.VMEM((seq, w), F32)],
        compiler_params=_cparams("parallel", "arbitrary"),
        name="s5_ssm",
    )(u, kb, pb, qb, row(lam_re), row(lam_im), row(jnp.repeat(log_dt, P)), d_skip.reshape(J, 1, w))


def kernel(x, p, ffn1_norm, ffn1_w_gate, ffn1_w_up, ffn1_w_down, mix_norm, ffn2_norm, ffn2_w_gate, ffn2_w_up, ffn2_w_down, ple_norm, ple_w_gate, ple_w_proj, ab_w_in, att_q_gain, att_k_gain, att_rel_bias, rwkv_mu, rwkv_w0, rwkv_w_up, rwkv_a0, rwkv_a_up, rwkv_g_up, rwkv_k_k, rwkv_k_a, rwkv_r_k, rwkv_lnx_w, rwkv_lnx_b, ab_w_out, ssm_w_in, ssm_lambda_re, ssm_lambda_im, ssm_log_dt, ssm_b_re, ssm_b_im, ssm_c_re, ssm_c_im, ssm_d, ssm_w_out):
    bsz, seq, d = x.shape
    depth = p.shape[0]
    n = bsz * seq
    bf = lambda w: w.astype(BF16)
    h = x.reshape(n, d)
    f1 = (ffn1_w_gate, ffn1_w_up, ffn1_w_down)
    f2 = (ffn2_w_gate, ffn2_w_up, ffn2_w_down)
    wcur = tuple(_cast_layer(w, 0) for w in f1)
    wpl = [bf(w) for w in (ple_w_gate, ple_w_proj)]
    pe = p.reshape(depth, n, -1)
    for i in range(depth):
        j = i // 2
        h, wcur = _ffn(h, ffn1_norm[i], wcur, 0, f2, i)
        if i % 2 == 0:
            d_att = att_rel_bias.shape[1] * HEAD_DIM
            d_rw = rwkv_w0.shape[1]
            n_in = ab_w_in.shape[2]
            tn = 1280 if n_in % 1280 == 0 else 128
            proj = _norm_matmul(h, mix_norm[i], bf(ab_w_in[j]), 2 * ROW_TILE, tn)
            bias = _bias_tiles(att_rel_bias[j])
            att = _attention(proj, bias, att_q_gain[j], att_k_gain[j], bsz, seq, d_att)
            rw = _rwkv(proj, 3 * d_att, bsz, seq, d_rw, rwkv_mu[j], rwkv_w0[j], rwkv_w_up[j],
                       rwkv_a0[j], rwkv_a_up[j], rwkv_g_up[j], rwkv_k_k[j], rwkv_k_a[j],
                       rwkv_r_k[j].reshape(-1), rwkv_lnx_w[j], rwkv_lnx_b[j])
            h = _out2(h, att, rw, bf(ab_w_out[j]))
        else:
            d_ssm = ssm_w_in.shape[2]
            u = _norm_matmul(h, mix_norm[i], bf(ssm_w_in[j]), ROW_TILE, d_ssm)
            mats = _s5_prep(ssm_lambda_re[j], ssm_lambda_im[j], ssm_log_dt[j], ssm_b_re[j],
                            ssm_b_im[j], ssm_c_re[j], ssm_c_im[j])
            y = _s5(u, bsz, seq, mats, ssm_lambda_re[j], ssm_lambda_im[j], ssm_log_dt[j], ssm_d[j])
            h = _glu_out(h, y, bf(ssm_w_out[j]))
        last = i + 1 == depth
        h, wcur = _ffn(h, ffn2_norm[i], wcur, 0, None if last else f1, i + 1)
        h = _ple(h, ple_norm[i], pe, wpl[0], wpl[1], i)
    return h.reshape(bsz, seq, d)
```

```python
import functools
import math

import jax
import jax.numpy as jnp
from jax import lax
from jax.experimental import pallas as pl
from jax.experimental.pallas import tpu as pltpu

F32 = jnp.float32
BF16 = jnp.bfloat16

RMS_EPS = 1e-6
GN_EPS = 64e-5
CHUNK = 64
N_LEFT_CHUNKS = 8
REL_CLIP = 128
HEAD_DIM = 64
ATT_LEFT = N_LEFT_CHUNKS * CHUNK
ATT_TQ = ATT_LEFT
ATT_SUB = 128
ATT_GROUP = 2
RWKV_L = 64
RWKV_NB = 4
SSM_GROUP = 16
SSM_STATE = 64
SSM_L = 16
NEG = -1e30
LOG2E = math.log2(math.e)

ROW_TILE = 512
FFN_ROW_TILE = 1024
FFN_VMEM_LIMIT = 58 * 2 ** 20
COL_TILE = 512
VMEM_LIMIT = 52 * 2 ** 20


def _cparams(*sem):
    return pltpu.CompilerParams(dimension_semantics=sem, vmem_limit_bytes=VMEM_LIMIT)


def _dot(a, b):
    return jnp.dot(a, b, preferred_element_type=F32)


def _dot_nt(a, b):
    return lax.dot_general(a, b, (((1,), (1,)), ((), ())), preferred_element_type=F32)


def _dot_tn(a, b):
    return lax.dot_general(a, b, (((0,), (0,)), ((), ())), preferred_element_type=F32)


def _bdot(a, b):
    return _dot(a.astype(BF16), b.astype(BF16))


def _bdot_nt(a, b):
    return _dot_nt(a.astype(BF16), b.astype(BF16))


def _bdot_tn(a, b):
    return _dot_tn(a.astype(BF16), b.astype(BF16))


def _rms(x, g):
    return x * lax.rsqrt(jnp.mean(x * x, axis=-1, keepdims=True) + RMS_EPS) * g


def _sigmoid(x):
    return 1.0 / (1.0 + jnp.exp(-x))


def _ffn_kernel(*refs, n_cast):
    h_ref, g_ref, wg_ref, wu_ref, wd_ref = refs[:5]
    cast_in = refs[5:5 + n_cast]
    o_ref = refs[5 + n_cast]
    cast_out = refs[6 + n_cast:6 + 2 * n_cast]
    n_scr = refs[6 + 2 * n_cast]

    @pl.when(pl.program_id(1) == 0)
    def _():
        h = h_ref[...]
        n_scr[...] = _rms(h, g_ref[...]).astype(BF16)
        o_ref[...] = h

    n = n_scr[...]
    gate = _dot(n, wg_ref[...])
    up = _dot(n, wu_ref[...])
    act = (0.5 * gate * _sigmoid(gate) * up).astype(BF16)
    o_ref[...] += _dot(act, wd_ref[...])
    for src, dst in zip(cast_in, cast_out):
        dst[...] = src[...].astype(BF16)


def _ffn(h, g, w, layer, nxt=None, nxt_layer=0):
    wg, wu, wd = w
    n, d = h.shape
    dff = wg.shape[2]
    tm = min(FFN_ROW_TILE, n)
    tf = min(COL_TILE, dff)
    ni, nf = n // tm, dff // tf
    in_specs = [
        pl.BlockSpec((tm, d), lambda i, f: (i, 0)),
        pl.BlockSpec((1, d), lambda i, f: (0, 0)),
        pl.BlockSpec((None, d, tf), lambda i, f: (layer, 0, f)),
        pl.BlockSpec((None, d, tf), lambda i, f: (layer, 0, f)),
        pl.BlockSpec((None, tf, d), lambda i, f: (layer, f, 0)),
    ]
    out_shape = [jax.ShapeDtypeStruct((n, d), F32)]
    out_specs = [pl.BlockSpec((tm, d), lambda i, f: (i, 0))]
    args = [h, g.reshape(1, d), wg, wu, wd]
    if nxt is not None:
        rd, rf = d // ni, tf // ni
        up_blk, down_blk = (None, rd, tf), (None, rf, d)
        in_specs += [pl.BlockSpec(up_blk, lambda i, f: (nxt_layer, i, f)),
                     pl.BlockSpec(up_blk, lambda i, f: (nxt_layer, i, f)),
                     pl.BlockSpec(down_blk, lambda i, f: (nxt_layer, f * ni + i, 0))]
        out_specs += [pl.BlockSpec(up_blk, lambda i, f: (0, i, f)),
                      pl.BlockSpec(up_blk, lambda i, f: (0, i, f)),
                      pl.BlockSpec(down_blk, lambda i, f: (0, f * ni + i, 0))]
        out_shape += [jax.ShapeDtypeStruct((1, d, dff), BF16), jax.ShapeDtypeStruct((1, d, dff), BF16),
                      jax.ShapeDtypeStruct((1, dff, d), BF16)]
        args += list(nxt)
    outs = pl.pallas_call(
        functools.partial(_ffn_kernel, n_cast=0 if nxt is None else 3),
        out_shape=out_shape,
        grid=(ni, nf),
        in_specs=in_specs,
        out_specs=out_specs,
        scratch_shapes=[pltpu.VMEM((tm, d), BF16)],
        compiler_params=pltpu.CompilerParams(dimension_semantics=("parallel", "arbitrary"),
                                             vmem_limit_bytes=FFN_VMEM_LIMIT),
        name="ffn",
    )(*args)
    return outs[0], tuple(outs[1:])


def _cast_kernel(x_ref, o_ref):
    o_ref[...] = x_ref[...].astype(o_ref.dtype)


def _cast_layer(w, layer):
    _, r, c = w.shape
    tr = min(256, r)
    return pl.pallas_call(
        _cast_kernel,
        out_shape=jax.ShapeDtypeStruct((1, r, c), BF16),
        grid=(r // tr,),
        in_specs=[pl.BlockSpec((None, tr, c), lambda i: (layer, i, 0))],
        out_specs=pl.BlockSpec((None, tr, c), lambda i: (0, i, 0)),
        compiler_params=_cparams("parallel"),
        name="cast_layer",
    )(w)


def _nmm_kernel(x_ref, g_ref, w_ref, o_ref, n_scr):
    @pl.when(pl.program_id(1) == 0)
    def _():
        n_scr[...] = _rms(x_ref[...], g_ref[...]).astype(BF16)

    o_ref[...] = _dot(n_scr[...], w_ref[...]).astype(o_ref.dtype)


def _norm_matmul(x, g, w, tm, tn):
    n, d = x.shape
    nout = w.shape[1]
    tm = min(tm, n)
    return pl.pallas_call(
        _nmm_kernel,
        out_shape=jax.ShapeDtypeStruct((n, nout), F32),
        grid=(n // tm, nout // tn),
        in_specs=[
            pl.BlockSpec((tm, d), lambda i, j: (i, 0)),
            pl.BlockSpec((1, d), lambda i, j: (0, 0)),
            pl.BlockSpec((d, tn), lambda i, j: (0, j)),
        ],
        out_specs=pl.BlockSpec((tm, tn), lambda i, j: (i, j)),
        scratch_shapes=[pltpu.VMEM((tm, d), BF16)],
        compiler_params=_cparams("parallel", "arbitrary"),
        name="norm_matmul",
    )(x, g.reshape(1, d), w)


def _out2_kernel(res_ref, a1_ref, a2_ref, w1_ref, w2_ref, o_ref):
    o_ref[...] = res_ref[...] + _dot(a1_ref[...], w1_ref[...]) + _dot(a2_ref[...], w2_ref[...])


def _out2(res, a1, a2, w):
    n, d = res.shape
    k1 = a1.shape[1]
    tm = min(ROW_TILE, n)
    return pl.pallas_call(
        _out2_kernel,
        out_shape=jax.ShapeDtypeStruct((n, d), F32),
        grid=(n // tm,),
        in_specs=[
            pl.BlockSpec((tm, d), lambda i: (i, 0)),
            pl.BlockSpec((tm, k1), lambda i: (i, 0)),
            pl.BlockSpec((tm, k1), lambda i: (i, 0)),
            pl.BlockSpec((k1, d), lambda i: (0, 0)),
            pl.BlockSpec((k1, d), lambda i: (1, 0)),
        ],
        out_specs=pl.BlockSpec((tm, d), lambda i: (i, 0)),
        compiler_params=_cparams("parallel"),
        name="mixer_out",
    )(res, a1, a2, w, w)


def _glu_kernel(res_ref, a_ref, wa_ref, wb_ref, o_ref):
    a = a_ref[...]
    za = _dot(a, wa_ref[...])
    zb = _dot(a, wb_ref[...])
    o_ref[...] = res_ref[...] + za * _sigmoid(zb)


def _glu_out(res, a, w):
    n, d = res.shape
    k = a.shape[1]
    tm = min(ROW_TILE, n)
    return pl.pallas_call(
        _glu_kernel,
        out_shape=jax.ShapeDtypeStruct((n, d), F32),
        grid=(n // tm,),
        in_specs=[
            pl.BlockSpec((tm, d), lambda i: (i, 0)),
            pl.BlockSpec((tm, k), lambda i: (i, 0)),
            pl.BlockSpec((k, d), lambda i: (0, 0)),
            pl.BlockSpec((k, d), lambda i: (0, 1)),
        ],
        out_specs=pl.BlockSpec((tm, d), lambda i: (i, 0)),
        compiler_params=_cparams("parallel"),
        name="glu_out",
    )(res, a, w, w)


def _ple_kernel(h_ref, g_ref, p_ref, wg_ref, wp_ref, o_ref):
    h = h_ref[...]
    gate = _sigmoid(_dot(_rms(h, g_ref[...]).astype(BF16), wg_ref[...]))
    proj = _dot(p_ref[...].astype(BF16), wp_ref[...])
    o_ref[...] = h + gate * proj


def _ple(h, g, p, wg, wp, layer):
    n, d = h.shape
    dp = p.shape[2]
    tm = min(ROW_TILE, n)
    return pl.pallas_call(
        _ple_kernel,
        out_shape=jax.ShapeDtypeStruct((n, d), F32),
        grid=(n // tm,),
        in_specs=[
            pl.BlockSpec((tm, d), lambda i: (i, 0)),
            pl.BlockSpec((1, d), lambda i: (0, 0)),
            pl.BlockSpec((None, tm, dp), lambda i: (layer, i, 0)),
            pl.BlockSpec((None, d, d), lambda i: (layer, 0, 0)),
            pl.BlockSpec((None, dp, d), lambda i: (layer, 0, 0)),
        ],
        out_specs=pl.BlockSpec((tm, d), lambda i: (i, 0)),
        compiler_params=_cparams("parallel"),
        name="ple",
    )(h, g.reshape(1, d), p, wg, wp)


def _bias_kernel(tbl_ref, o_ref):
    tq = o_ref.shape[1]
    tk = o_ref.shape[2]
    n_rel = tbl_ref.shape[2]
    width = pl.next_power_of_2(tq + tk)
    n_idx = lax.broadcasted_iota(jnp.int32, (n_rel, width), 1)
    c_idx = lax.broadcasted_iota(jnp.int32, (n_rel, width), 0)
    m = jnp.where(n_idx < tk, n_idx, n_idx - width)
    idx = jnp.clip(ATT_LEFT - m, -(CHUNK - 1), REL_CLIP) + (CHUNK - 1)
    onehot = (c_idx == idx).astype(F32)
    tbl = jnp.broadcast_to(tbl_ref[0], (8, n_rel))
    ext = jnp.dot(tbl, onehot, preferred_element_type=F32, precision=lax.Precision.HIGHEST)[0:1]
    x = jnp.broadcast_to(ext, (tq, width))
    rolled = pltpu.roll(x, 0, 1, stride=1, stride_axis=0)[:, :tk]
    qc = lax.broadcasted_iota(jnp.int32, (tq, tk), 0) // CHUNK
    kc = lax.broadcasted_iota(jnp.int32, (tq, tk), 1) // CHUNK
    ok = (kc >= qc) & (kc <= qc + N_LEFT_CHUNKS)
    o_ref[0] = jnp.where(ok, rolled * LOG2E, NEG)


def _bias_tiles(rel_bias):
    nh, n_rel = rel_bias.shape
    tk = ATT_SUB + ATT_LEFT
    return pl.pallas_call(
        _bias_kernel,
        out_shape=jax.ShapeDtypeStruct((nh, ATT_SUB, tk), F32),
        grid=(nh,),
        in_specs=[pl.BlockSpec((1, 1, n_rel), lambda h: (h, 0, 0))],
        out_specs=pl.BlockSpec((1, ATT_SUB, tk), lambda h: (h, 0, 0)),
        compiler_params=_cparams("parallel"),
        name="rel_bias_tiles",
    )(rel_bias.reshape(nh, 1, n_rel))


def _attn_kernel(q_ref, kp_ref, kc_ref, vp_ref, vc_ref, bias_ref, qg_ref, kg_ref, o_ref):
    has_prev = pl.program_id(2) > 0
    tq, w = q_ref.shape
    tk = bias_ref.shape[2]
    col = lax.broadcasted_iota(jnp.int32, (w // HEAD_DIM * ATT_SUB, tk), 1)
    head_of_lane = lax.broadcasted_iota(jnp.int32, (1, w), 1) // HEAD_DIM
    seg = (lax.broadcasted_iota(jnp.int32, (w, w), 0) // HEAD_DIM
           == lax.broadcasted_iota(jnp.int32, (w, w), 1) // HEAD_DIM)
    seg = jnp.where(seg, 1.0, 0.0).astype(BF16)

    def rms2(x, gain):
        sq = x * x
        hi = sq.astype(BF16)
        lo = (sq - hi.astype(F32)).astype(BF16)
        ss = _dot(jnp.concatenate([hi, lo], axis=1), jnp.concatenate([seg, seg], axis=0))
        return x * lax.rsqrt(ss * (1.0 / HEAD_DIM) + RMS_EPS) * gain

    q = rms2(q_ref[...], qg_ref[...] * (HEAD_DIM ** -0.5 * LOG2E))
    k = jnp.concatenate([rms2(kp_ref[...], kg_ref[...]), rms2(kc_ref[...], kg_ref[...])], axis=0).astype(BF16)
    v = jnp.concatenate([vp_ref[...], vc_ref[...]], axis=0).astype(BF16)
    nh = w // HEAD_DIM
    qh = [jnp.where(head_of_lane == hh, q, 0.0).astype(BF16) for hh in range(nh)]
    bias = jnp.concatenate([bias_ref[hh] for hh in range(nh)], axis=0)
    starts = [i * ATT_SUB for i in range(tq // ATT_SUB)]
    ob = []
    for g0 in range(0, len(starts), ATT_GROUP):
        jobs = starts[g0:g0 + ATT_GROUP]
        sc = [_dot_nt(jnp.concatenate([x[r0:r0 + ATT_SUB] for x in qh], axis=0), k[r0:r0 + tk]) + bias
              for r0 in jobs]
        sc = [jnp.where(has_prev | (col >= ATT_LEFT - r0), s, NEG) if r0 < ATT_LEFT else s
              for s, r0 in zip(sc, jobs)]
        pr = [jnp.exp2(s - jnp.max(s, axis=-1, keepdims=True)) for s in sc]
        den = [jnp.sum(p, axis=-1, keepdims=True) for p in pr]
        ob += [_dot(p.astype(BF16), v[r0:r0 + tk]) / d for p, d, r0 in zip(pr, den, jobs)]
    outs = []
    for o2 in ob:
        o = o2[:ATT_SUB]
        for hh in range(1, nh):
            o = jnp.where(head_of_lane == hh, o2[hh * ATT_SUB:(hh + 1) * ATT_SUB], o)
        outs.append(o)
    o_ref[...] = jnp.concatenate(outs, axis=0).astype(o_ref.dtype)


def _attention(proj, bias, q_gain, k_gain, bsz, seq, d_att):
    n = bsz * seq
    tq = ATT_TQ
    nqb = seq // tq
    npair = d_att // (2 * HEAD_DIM)
    w = 2 * HEAD_DIM

    def cur(col0):
        return pl.BlockSpec((tq, w), lambda hp, b, qb: (b * nqb + qb, col0 + hp))

    def prev(col0):
        return pl.BlockSpec((tq, w), lambda hp, b, qb: (b * nqb + jnp.maximum(qb - 1, 0), col0 + hp))

    return pl.pallas_call(
        _attn_kernel,
        out_shape=jax.ShapeDtypeStruct((n, d_att), BF16),
        grid=(npair, bsz, nqb),
        in_specs=[
            cur(0), prev(npair), cur(npair), prev(2 * npair), cur(2 * npair),
            pl.BlockSpec((2, ATT_SUB, ATT_SUB + ATT_LEFT), lambda hp, b, qb: (hp, 0, 0)),
            pl.BlockSpec((1, w), lambda hp, b, qb: (0, 0)),
            pl.BlockSpec((1, w), lambda hp, b, qb: (0, 0)),
        ],
        out_specs=pl.BlockSpec((tq, w), lambda hp, b, qb: (b * nqb + qb, hp)),
        compiler_params=_cparams("parallel", "parallel", "arbitrary"),
        name="band_attention",
    )(proj, proj, proj, proj, proj, bias, jnp.tile(q_gain, 2).reshape(1, w), jnp.tile(k_gain, 2).reshape(1, w))


def _split3(x):
    hi = x.astype(BF16)
    r1 = x - hi.astype(F32)
    mid = r1.astype(BF16)
    lo = (r1 - mid.astype(F32)).astype(BF16)
    return hi, mid, lo


def _token_shift(z, carry_ref, mu):
    rows = lax.broadcasted_iota(jnp.int32, z.shape, 0)
    prev = jnp.where(rows == 0, carry_ref[...], pltpu.roll(z, 1, 0))
    carry_ref[...] = z[z.shape[0] - 1:, :]
    return z + (prev - z) * mu


def _rwkv_kernel(r_ref, k_ref, v_ref, l_ref, mur_ref, muk_ref, muv_ref, mul_ref,
                 w0_ref, a0_ref, kk_ref, ka_ref, rk_ref, lnw_ref, lnb_ref,
                 wup_ref, aup_ref, gup_ref, o_ref,
                 s_scr, cr_scr, ck_scr, cv_scr, cl_scr):
    first = pl.program_id(1) == 0
    nb, L, d = r_ref.shape

    @pl.when(first)
    def _():
        s_scr[...] = jnp.zeros_like(s_scr)
        for c in (cr_scr, ck_scr, cv_scr, cl_scr):
            c[...] = jnp.zeros_like(c)

    def shifted(ref, carry, mu_ref):
        return jnp.concatenate([_token_shift(ref[b], carry.at[b], mu_ref[...]) for b in range(nb)], axis=0)

    r = shifted(r_ref, cr_scr, mur_ref)
    k = shifted(k_ref, ck_scr, muk_ref)
    v = shifted(v_ref, cv_scr, muv_ref)
    lo = shifted(l_ref, cl_scr, mul_ref)
    nw = wup_ref.shape[0]
    na = aup_ref.shape[0]
    xw, xa, xg = lo[:, :nw], lo[:, nw:nw + na], lo[:, nw + na:]

    wpre = w0_ref[...] + _bdot(jnp.tanh(xw), wup_ref[...])
    w_log = -(jnp.maximum(-wpre, 0.0) + jnp.log(1.0 + jnp.exp(-jnp.abs(wpre)))) - 0.5
    lw = -jnp.exp(w_log)
    a = _sigmoid(a0_ref[...] + _bdot(xa, aup_ref[...]))
    g = _bdot(_sigmoid(xg), gup_ref[...])

    kkf = k * kk_ref[...]
    k2 = k * (1.0 + (a - 1.0) * ka_ref[...])

    ti = lax.broadcasted_iota(jnp.int32, (nb * L, nb * L), 0)
    si = lax.broadcasted_iota(jnp.int32, (nb * L, nb * L), 1)
    tri = jnp.where((si <= ti) & (si // L == ti // L), 1.0, 0.0).astype(BF16)
    h3 = _split3(lw)
    cum = _dot(jnp.concatenate([tri, tri, tri], axis=1), jnp.concatenate(h3, axis=0))
    e_pos = jnp.exp(cum)
    e_neg = jnp.exp(-cum)
    e_prev = jnp.exp(cum - lw)

    rk = r * k2 * rk_ref[...]
    w2 = 2 * HEAD_DIM
    chains = [(b, j) for b in range(nb) for j in range(d // w2)]
    at = lambda x, c: x[c[0] * L:(c[0] + 1) * L, c[1] * w2:(c[1] + 1) * w2]
    lane = lax.broadcasted_iota(jnp.int32, (1, w2), 1)
    in_a = lane < HEAD_DIM
    row_l = lax.broadcasted_iota(jnp.int32, (L, w2), 0)
    idx_l = lax.broadcasted_iota(jnp.int32, (L, w2), 1) % HEAD_DIM
    strict = idx_l < row_l
    incl = idx_l <= row_l
    same_head = (lax.broadcasted_iota(jnp.int32, (w2, w2), 0) // HEAD_DIM
                 == lax.broadcasted_iota(jnp.int32, (w2, w2), 1) // HEAD_DIM)
    seg = jnp.where(same_head, 1.0, 0.0).astype(BF16)

    def segsum(xs):
        x = jnp.concatenate(xs, axis=0)
        hi = x.astype(BF16)
        lo = (x - hi.astype(F32)).astype(BF16)
        tot = _dot(jnp.concatenate([hi, lo], axis=1), jnp.concatenate([seg, seg], axis=0))
        return [tot[i * L:(i + 1) * L] for i in range(len(xs))]

    def bdiag(x):
        zero = jnp.zeros_like(x)
        return jnp.concatenate([jnp.where(in_a, x, zero), jnp.where(in_a, zero, x)], axis=0)

    def lower(x, mask):
        xb = x.astype(BF16)
        return jnp.where(mask, xb, jnp.zeros_like(xb))

    kss = segsum([at(kkf, c) * at(kkf, c) for c in chains])
    steps = max(1, (L - 1).bit_length())
    ar, bkc, rb, vb, vbd = [], [], [], [], []
    for c, ks in zip(chains, kss):
        kk = at(kkf, c) / jnp.maximum(jnp.sqrt(ks), 1e-12)
        ah = -kk * at(e_prev, c)
        bh = (kk * at(a, c) * at(e_neg, c)).astype(BF16)
        kh = (at(k2, c) * at(e_neg, c)).astype(BF16)
        rh = at(r, c) * at(e_pos, c)
        ar.append(jnp.concatenate([ah, rh], axis=0).astype(BF16))
        bkc.append(jnp.concatenate([bh, kh], axis=0))
        rb.append(jnp.concatenate([bdiag(bh), bdiag(kh)], axis=0))
        vb.append(at(v, c).astype(BF16))
        vbd.append(bdiag(vb[-1]))
    ids = range(len(chains))
    s0 = [s_scr[c] for c in chains]
    gm = [_dot_nt(ar[i], rb[i]) for i in ids]
    ps = [_dot_nt(ar[i], s0[i].astype(BF16)) for i in ids]
    pw = [lower(gm[i][:L, :w2], strict) for i in ids]
    u = [ps[i][:L] + _dot(lower(gm[i][:L, w2:], strict), vbd[i]) for i in ids]
    for st in range(steps):
        u = [u[i] + _dot(pw[i], bdiag(u[i].astype(BF16))) for i in ids]
        if st + 1 < steps:
            pw = [_dot(pw[i], bdiag(pw[i])).astype(BF16) for i in ids]
    y = []
    for i, c in zip(ids, chains):
        ub = u[i].astype(BF16)
        low = jnp.concatenate([lower(gm[i][L:, :w2], incl), lower(gm[i][L:, w2:], incl)], axis=1)
        uvd = jnp.concatenate([bdiag(ub), vbd[i]], axis=0)
        y.append(ps[i][L:] + _dot(low, uvd))
        uv = jnp.concatenate([ub, vb[i]], axis=0)
        w_last = at(e_pos, c)[L - 1:, :]
        s_scr[c] = jnp.where(same_head, s0[i] + _dot_tn(uv, bkc[i]), 0.0) * w_last
    yc = [yi - m * (1.0 / HEAD_DIM) for yi, m in zip(y, segsum(y))]
    var = segsum([c * c for c in yc])
    yn = [c * lax.rsqrt(vr * (1.0 / HEAD_DIM) + GN_EPS) for c, vr in zip(yc, var)]
    bs = segsum([at(rk, c) for c in chains])
    npair = d // w2
    grid2 = lambda xs: jnp.concatenate([jnp.concatenate(xs[b * npair:(b + 1) * npair], axis=1)
                                        for b in range(nb)], axis=0)
    out = (grid2(yn) * lnw_ref[...] + lnb_ref[...] + grid2(bs) * v) * g
    o_ref[...] = out.reshape(nb, L, d).astype(o_ref.dtype)


def _rwkv(proj, col0, bsz, seq, d, mu, w0, w_up, a0, a_up, g_up, k_k, k_a, r_k, lnx_w, lnx_b):
    L = RWKV_L
    nb = RWKV_NB if bsz % RWKV_NB == 0 else 1
    nt = seq // L
    nl = mu.shape[0] - 3 * d
    row = lambda x: x.reshape(1, -1)
    cb = col0 // d
    lb = (col0 + 3 * d) // nl
    proj3 = proj.reshape(bsz, seq, -1)

    def zspec(width, blk):
        return pl.BlockSpec((nb, L, width), lambda b, t: (b, t, blk))

    def pspec(shape):
        return pl.BlockSpec(shape, lambda b, t: (0,) * len(shape))

    npair = d // (2 * HEAD_DIM)
    out = pl.pallas_call(
        _rwkv_kernel,
        out_shape=jax.ShapeDtypeStruct((bsz, seq, d), BF16),
        grid=(bsz // nb, nt),
        in_specs=[
            zspec(d, cb), zspec(d, cb + 1), zspec(d, cb + 2), zspec(nl, lb),
            pspec((1, d)), pspec((1, d)), pspec((1, d)), pspec((1, nl)),
            pspec((1, d)), pspec((1, d)), pspec((1, d)), pspec((1, d)), pspec((1, d)),
            pspec((1, d)), pspec((1, d)),
            pspec(w_up.shape), pspec(a_up.shape), pspec(g_up.shape),
        ],
        out_specs=pl.BlockSpec((nb, L, d), lambda b, t: (b, t, 0)),
        scratch_shapes=[
            pltpu.VMEM((nb, npair, 2 * HEAD_DIM, 2 * HEAD_DIM), F32),
            pltpu.VMEM((nb, 1, d), F32), pltpu.VMEM((nb, 1, d), F32), pltpu.VMEM((nb, 1, d), F32),
            pltpu.VMEM((nb, 1, nl), F32),
        ],
        compiler_params=_cparams("parallel", "arbitrary"),
        name="rwkv7",
    )(proj3, proj3, proj3, proj3,
      row(mu[:d]), row(mu[d:2 * d]), row(mu[2 * d:3 * d]), row(mu[3 * d:]),
      row(w0), row(a0), row(k_k), row(k_a), row(r_k), row(lnx_w), row(lnx_b),
      w_up.astype(BF16), a_up.astype(BF16), g_up.astype(BF16))
    return out.reshape(bsz * seq, d)


SSM_GB = 8


def _cmul(ar, ai, br, bi):
    return ar * br - ai * bi, ar * bi + ai * br


def _s5_abar(lr, li, ldt):
    dt = jnp.exp(ldt)
    mag = jnp.exp(lr * dt)
    return mag * jnp.cos(li * dt), mag * jnp.sin(li * dt)


def _s5_prep_kernel(lrr_ref, lir_ref, ldtr_ref, lrc_ref, lic_ref, ldtc_ref,
                    bre_ref, bim_ref, cre_ref, cim_ref, k_ref, p_ref, q_ref):
    L = SSM_L
    w = k_ref.shape[2]
    lr, li = lrr_ref[0], lir_ref[0]
    a_re, a_im = _s5_abar(lr, li, ldtr_ref[0])
    den = lr * lr + li * li
    z_re = ((a_re - 1.0) * lr + a_im * li) / den
    z_im = (a_im * lr - (a_re - 1.0) * li) / den
    bb_re, bb_im = _cmul(z_re, z_im, bre_ref[0], bim_ref[0])
    cre, cim = cre_ref[0], cim_ref[0]
    ccat = jnp.concatenate([cre, -cim], axis=0)
    ac_re, ac_im = _s5_abar(lrc_ref[0], lic_ref[0], ldtc_ref[0])
    pr, pi = jnp.ones_like(a_re), jnp.zeros_like(a_im)
    qr, qi = ac_re, ac_im
    ptaus = []
    for tau in range(L):
        rb_re, rb_im = _cmul(pr, pi, bb_re, bb_im)
        ptau = jnp.concatenate([rb_re, rb_im], axis=1)
        ptaus.append(ptau)
        p_ref[0, (L - 1 - tau) * w:(L - tau) * w, :] = ptau.astype(p_ref.dtype)
        q_ref[0, :, tau * w:(tau + 1) * w] = jnp.concatenate(
            [cre * qr - cim * qi, -(cre * qi + cim * qr)], axis=0).astype(q_ref.dtype)
        pr, pi = _cmul(pr, pi, a_re, a_im)
        qr, qi = _cmul(qr, qi, ac_re, ac_im)
    pall = jnp.concatenate(ptaus[::-1], axis=0)
    p_hi = pall.astype(BF16)
    p_lo = (pall - p_hi.astype(F32)).astype(BF16)
    c_hi = ccat.astype(BF16)
    c_lo = (ccat - c_hi.astype(F32)).astype(BF16)
    k_ref[0] = (_dot(p_hi, c_hi) + _dot(p_lo, c_hi) + _dot(p_hi, c_lo)).astype(k_ref.dtype)


def _s5_prep(lam_re, lam_im, log_dt, b_re, b_im, c_re, c_im):
    G, P = lam_re.shape
    gs = SSM_GROUP
    gb = SSM_GB
    J = G // gb
    L = SSM_L
    eye = jnp.eye(gb, dtype=F32)
    ldt = jnp.repeat(log_dt, P)

    def bdiag_b(b):
        bt = jnp.swapaxes(b, 1, 2).reshape(J, gb, gs, P)
        return (bt[:, :, :, None, :] * eye[None, :, None, :, None]).reshape(J, gb * gs, gb * P)

    def bdiag_c(c):
        ct = jnp.swapaxes(c, 1, 2).reshape(J, gb, P, gs)
        return (ct[:, :, :, None, :] * eye[None, :, None, :, None]).reshape(J, gb * P, gb * gs)

    row = lambda x: x.reshape(J, 1, gb * P)
    col = lambda x: x.reshape(J, gb * P, 1)

    def spec(*shape):
        return pl.BlockSpec((1,) + shape, lambda j: (j,) + (0,) * len(shape))

    w, sw = gb * gs, gb * P
    return pl.pallas_call(
        _s5_prep_kernel,
        out_shape=[jax.ShapeDtypeStruct((J, L * w, w), BF16),
                   jax.ShapeDtypeStruct((J, L * w, 2 * sw), BF16),
                   jax.ShapeDtypeStruct((J, 2 * sw, L * w), BF16)],
        grid=(J,),
        in_specs=[spec(1, sw), spec(1, sw), spec(1, sw), spec(sw, 1), spec(sw, 1), spec(sw, 1),
                  spec(w, sw), spec(w, sw), spec(sw, w), spec(sw, w)],
        out_specs=[spec(L * w, w), spec(L * w, 2 * sw), spec(2 * sw, L * w)],
        compiler_params=_cparams("parallel"),
        name="s5_prep",
    )(row(lam_re), row(lam_im), row(ldt), col(lam_re), col(lam_im), col(ldt),
      bdiag_b(b_re), bdiag_b(b_im), bdiag_c(c_re), bdiag_c(c_im))


def _s5_kernel(u_ref, k_ref, p_ref, q_ref, lr_ref, li_ref, ldt_ref, d_ref, o_ref,
               gre_scr, gim_scr, hre_scr, him_scr, y_scr):
    L = SSM_L
    nc = u_ref.shape[0] // L
    sw = lr_ref.shape[2]
    w = u_ref.shape[1]
    us = [u_ref[pl.ds(s, nc, stride=L), :].astype(BF16) for s in range(L)]
    ucat = jnp.concatenate(us, axis=1)
    gall = _dot(ucat, p_ref[0])
    gre_scr[...] = gall[:, :sw]
    gim_scr[...] = gall[:, sw:]
    a_re, a_im = _s5_abar(lr_ref[0], li_ref[0], ldt_ref[0])
    al_re, al_im = a_re, a_im
    for _ in range(L.bit_length() - 1):
        al_re, al_im = _cmul(al_re, al_im, al_re, al_im)

    def body(c, carry):
        hre, him = carry
        hre_scr[pl.ds(c, 1), :] = hre
        him_scr[pl.ds(c, 1), :] = him
        gr = gre_scr[pl.ds(c, 1), :]
        gi = gim_scr[pl.ds(c, 1), :]
        return al_re * hre - al_im * him + gr, al_re * him + al_im * hre + gi

    zero = jnp.zeros((1, sw), F32)
    lax.fori_loop(0, nc, body, (zero, zero))
    hp = jnp.concatenate([hre_scr[...], him_scr[...]], axis=1).astype(BF16)
    ycarry = _dot(hp, q_ref[0])
    kall = k_ref[0]
    for t in range(L):
        yt = ycarry[:, t * w:(t + 1) * w] + _dot(ucat[:, :(t + 1) * w], kall[(L - 1 - t) * w:, :])
        y_scr[pl.ds(t, nc, stride=L), :] = yt
    y = y_scr[...] + d_ref[0] * u_ref[...]
    o_ref[...] = (0.5 * y * (1.0 + jnp.tanh(math.sqrt(2.0 / math.pi) * (y + 0.044715 * y * y * y)))).astype(o_ref.dtype)


def _s5(u, bsz, seq, mats, lam_re, lam_im, log_dt, d_skip):
    kb, pb, qb = mats
    G, P = lam_re.shape
    gb = SSM_GB
    J = G // gb
    L = SSM_L
    w = gb * SSM_GROUP
    sw = gb * P
    nc = seq // L
    row = lambda x: x.reshape(J, 1, sw)

    def jspec(*shape):
        return pl.BlockSpec((1,) + shape, lambda j, b: (j,) + (0,) * len(shape))

    return pl.pallas_call(
        _s5_kernel,
        out_shape=jax.ShapeDtypeStruct((bsz * seq, G * SSM_GROUP), BF16),
        grid=(J, bsz),
        in_specs=[
            pl.BlockSpec((seq, w), lambda j, b: (b, j)),
            jspec(L * w, w), jspec(L * w, 2 * sw), jspec(2 * sw, L * w),
            jspec(1, sw), jspec(1, sw), jspec(1, sw), jspec(1, w),
        ],
        out_specs=pl.BlockSpec((seq, w), lambda j, b: (b, j)),
        scratch_shapes=[pltpu.VMEM((nc, sw), F32)] * 4 + [pltpu.VMEM((seq, w), F32)],
        compiler_params=_cparams("parallel", "arbitrary"),
        name="s5_ssm",
    )(u, kb, pb, qb, row(lam_re), row(lam_im), row(jnp.repeat(log_dt, P)), d_skip.reshape(J, 1, w))


def kernel(x, p, ffn1_norm, ffn1_w_gate, ffn1_w_up, ffn1_w_down, mix_norm, ffn2_norm, ffn2_w_gate, ffn2_w_up, ffn2_w_down, ple_norm, ple_w_gate, ple_w_proj, ab_w_in, att_q_gain, att_k_gain, att_rel_bias, rwkv_mu, rwkv_w0, rwkv_w_up, rwkv_a0, rwkv_a_up, rwkv_g_up, rwkv_k_k, rwkv_k_a, rwkv_r_k, rwkv_lnx_w, rwkv_lnx_b, ab_w_out, ssm_w_in, ssm_lambda_re, ssm_lambda_im, ssm_log_dt, ssm_b_re, ssm_b_im, ssm_c_re, ssm_c_im, ssm_d, ssm_w_out):
    bsz, seq, d = x.shape
    depth = p.shape[0]
    n = bsz * seq
    bf = lambda w: w.astype(BF16)
    h = x.reshape(n, d)
    f1 = (ffn1_w_gate, ffn1_w_up, ffn1_w_down)
    f2 = (ffn2_w_gate, ffn2_w_up, ffn2_w_down)
    wcur = tuple(_cast_layer(w, 0) for w in f1)
    wpl = [bf(w) for w in (ple_w_gate, ple_w_proj)]
    pe = p.reshape(depth, n, -1)
    for i in range(depth):
        j = i // 2
        h, wcur = _ffn(h, ffn1_norm[i], wcur, 0, f2, i)
        if i % 2 == 0:
            d_att = att_rel_bias.shape[1] * HEAD_DIM
            d_rw = rwkv_w0.shape[1]
            n_in = ab_w_in.shape[2]
            tn = 1280 if n_in % 1280 == 0 else 128
            proj = _norm_matmul(h, mix_norm[i], bf(ab_w_in[j]), 2 * ROW_TILE, tn)
            bias = _bias_tiles(att_rel_bias[j])
            att = _attention(proj, bias, att_q_gain[j], att_k_gain[j], bsz, seq, d_att)
            rw = _rwkv(proj, 3 * d_att, bsz, seq, d_rw, rwkv_mu[j], rwkv_w0[j], rwkv_w_up[j],
                       rwkv_a0[j], rwkv_a_up[j], rwkv_g_up[j], rwkv_k_k[j], rwkv_k_a[j],
                       rwkv_r_k[j].reshape(-1), rwkv_lnx_w[j], rwkv_lnx_b[j])
            h = _out2(h, att, rw, bf(ab_w_out[j]))
        else:
            d_ssm = ssm_w_in.shape[2]
            u = _norm_matmul(h, mix_norm[i], bf(ssm_w_in[j]), ROW_TILE, d_ssm)
            mats = _s5_prep(ssm_lambda_re[j], ssm_lambda_im[j], ssm_log_dt[j], ssm_b_re[j],
                            ssm_b_im[j], ssm_c_re[j], ssm_c_im[j])
            y = _s5(u, bsz, seq, mats, ssm_lambda_re[j], ssm_lambda_im[j], ssm_log_dt[j], ssm_d[j])
            h = _glu_out(h, y, bf(ssm_w_out[j]))
        last = i + 1 == depth
        h, wcur = _ffn(h, ffn2_norm[i], wcur, 0, None if last else f1, i + 1)
        h = _ple(h, ple_norm[i], pe, wpl[0], wpl[1], i)
    return h.reshape(bsz, seq, d)
```

```python
import functools
import math

import jax
import jax.numpy as jnp
from jax import lax
from jax.experimental import pallas as pl
from jax.experimental.pallas import tpu as pltpu

F32 = jnp.float32
BF16 = jnp.bfloat16

RMS_EPS = 1e-6
GN_EPS = 64e-5
CHUNK = 64
N_LEFT_CHUNKS = 8
REL_CLIP = 128
HEAD_DIM = 64
ATT_LEFT = N_LEFT_CHUNKS * CHUNK
ATT_TQ = ATT_LEFT
ATT_SUB = 128
ATT_GROUP = 2
RWKV_L = 64
RWKV_NB = 4
SSM_GROUP = 16
SSM_STATE = 64
SSM_L = 16
NEG = -1e30
LOG2E = math.log2(math.e)

ROW_TILE = 512
FFN_ROW_TILE = 1024
FFN_VMEM_LIMIT = 58 * 2 ** 20
COL_TILE = 512
VMEM_LIMIT = 52 * 2 ** 20


def _cparams(*sem):
    return pltpu.CompilerParams(dimension_semantics=sem, vmem_limit_bytes=VMEM_LIMIT)


def _dot(a, b):
    return jnp.dot(a, b, preferred_element_type=F32)


def _dot_nt(a, b):
    return lax.dot_general(a, b, (((1,), (1,)), ((), ())), preferred_element_type=F32)


def _dot_tn(a, b):
    return lax.dot_general(a, b, (((0,), (0,)), ((), ())), preferred_element_type=F32)


def _bdot(a, b):
    return _dot(a.astype(BF16), b.astype(BF16))


def _bdot_nt(a, b):
    return _dot_nt(a.astype(BF16), b.astype(BF16))


def _bdot_tn(a, b):
    return _dot_tn(a.astype(BF16), b.astype(BF16))


def _rms(x, g):
    return x * lax.rsqrt(jnp.mean(x * x, axis=-1, keepdims=True) + RMS_EPS) * g


def _sigmoid(x):
    return 1.0 / (1.0 + jnp.exp(-x))


def _ffn_kernel(*refs, n_cast):
    h_ref, g_ref, wg_ref, wu_ref, wd_ref = refs[:5]
    cast_in = refs[5:5 + n_cast]
    o_ref = refs[5 + n_cast]
    cast_out = refs[6 + n_cast:6 + 2 * n_cast]
    n_scr = refs[6 + 2 * n_cast]

    @pl.when(pl.program_id(1) == 0)
    def _():
        h = h_ref[...]
        n_scr[...] = _rms(h, g_ref[...]).astype(BF16)
        o_ref[...] = h

    n = n_scr[...]
    gate = _dot(n, wg_ref[...])
    up = _dot(n, wu_ref[...])
    act = (0.5 * gate * _sigmoid(gate) * up).astype(BF16)
    o_ref[...] += _dot(act, wd_ref[...])
    for src, dst in zip(cast_in, cast_out):
        dst[...] = src[...].astype(BF16)


def _ffn(h, g, w, layer, nxt=None, nxt_layer=0):
    wg, wu, wd = w
    n, d = h.shape
    dff = wg.shape[2]
    tm = min(FFN_ROW_TILE, n)
    tf = min(COL_TILE, dff)
    ni, nf = n // tm, dff // tf
    in_specs = [
        pl.BlockSpec((tm, d), lambda i, f: (i, 0)),
        pl.BlockSpec((1, d), lambda i, f: (0, 0)),
        pl.BlockSpec((None, d, tf), lambda i, f: (layer, 0, f)),
        pl.BlockSpec((None, d, tf), lambda i, f: (layer, 0, f)),
        pl.BlockSpec((None, tf, d), lambda i, f: (layer, f, 0)),
    ]
    out_shape = [jax.ShapeDtypeStruct((n, d), F32)]
    out_specs = [pl.BlockSpec((tm, d), lambda i, f: (i, 0))]
    args = [h, g.reshape(1, d), wg, wu, wd]
    if nxt is not None:
        rd, rf = d // ni, tf // ni
        up_blk, down_blk = (None, rd, tf), (None, rf, d)
        in_specs += [pl.BlockSpec(up_blk, lambda i, f: (nxt_layer, i, f)),
                     pl.BlockSpec(up_blk, lambda i, f: (nxt_layer, i, f)),
                     pl.BlockSpec(down_blk, lambda i, f: (nxt_layer, f * ni + i, 0))]
        out_specs += [pl.BlockSpec(up_blk, lambda i, f: (0, i, f)),
                      pl.BlockSpec(up_blk, lambda i, f: (0, i, f)),
                      pl.BlockSpec(down_blk, lambda i, f: (0, f * ni + i, 0))]
        out_shape += [jax.ShapeDtypeStruct((1, d, dff), BF16), jax.ShapeDtypeStruct((1, d, dff), BF16),
                      jax.ShapeDtypeStruct((1, dff, d), BF16)]
        args += list(nxt)
    outs = pl.pallas_call(
        functools.partial(_ffn_kernel, n_cast=0 if nxt is None else 3),
        out_shape=out_shape,
        grid=(ni, nf),
        in_specs=in_specs,
        out_specs=out_specs,
        scratch_shapes=[pltpu.VMEM((tm, d), BF16)],
        compiler_params=pltpu.CompilerParams(dimension_semantics=("parallel", "arbitrary"),
                                             vmem_limit_bytes=FFN_VMEM_LIMIT),
        name="ffn",
    )(*args)
    return outs[0], tuple(outs[1:])


def _cast_kernel(x_ref, o_ref):
    o_ref[...] = x_ref[...].astype(o_ref.dtype)


def _cast_layer(w, layer):
    _, r, c = w.shape
    tr = min(256, r)
    return pl.pallas_call(
        _cast_kernel,
        out_shape=jax.ShapeDtypeStruct((1, r, c), BF16),
        grid=(r // tr,),
        in_specs=[pl.BlockSpec((None, tr, c), lambda i: (layer, i, 0))],
        out_specs=pl.BlockSpec((None, tr, c), lambda i: (0, i, 0)),
        compiler_params=_cparams("parallel"),
        name="cast_layer",
    )(w)


def _nmm_kernel(x_ref, g_ref, w_ref, o_ref, n_scr):
    @pl.when(pl.program_id(1) == 0)
    def _():
        n_scr[...] = _rms(x_ref[...], g_ref[...]).astype(BF16)

    o_ref[...] = _dot(n_scr[...], w_ref[...]).astype(o_ref.dtype)


def _norm_matmul(x, g, w, tm, tn):
    n, d = x.shape
    nout = w.shape[1]
    tm = min(tm, n)
    return pl.pallas_call(
        _nmm_kernel,
        out_shape=jax.ShapeDtypeStruct((n, nout), F32),
        grid=(n // tm, nout // tn),
        in_specs=[
            pl.BlockSpec((tm, d), lambda i, j: (i, 0)),
            pl.BlockSpec((1, d), lambda i, j: (0, 0)),
            pl.BlockSpec((d, tn), lambda i, j: (0, j)),
        ],
        out_specs=pl.BlockSpec((tm, tn), lambda i, j: (i, j)),
        scratch_shapes=[pltpu.VMEM((tm, d), BF16)],
        compiler_params=_cparams("parallel", "arbitrary"),
        name="norm_matmul",
    )(x, g.reshape(1, d), w)


def _out2_kernel(res_ref, a1_ref, a2_ref, w1_ref, w2_ref, o_ref):
    o_ref[...] = res_ref[...] + _dot(a1_ref[...], w1_ref[...]) + _dot(a2_ref[...], w2_ref[...])


def _out2(res, a1, a2, w):
    n, d = res.shape
    k1 = a1.shape[1]
    tm = min(ROW_TILE, n)
    return pl.pallas_call(
        _out2_kernel,
        out_shape=jax.ShapeDtypeStruct((n, d), F32),
        grid=(n // tm,),
        in_specs=[
            pl.BlockSpec((tm, d), lambda i: (i, 0)),
            pl.BlockSpec((tm, k1), lambda i: (i, 0)),
            pl.BlockSpec((tm, k1), lambda i: (i, 0)),
            pl.BlockSpec((k1, d), lambda i: (0, 0)),
            pl.BlockSpec((k1, d), lambda i: (1, 0)),
        ],
        out_specs=pl.BlockSpec((tm, d), lambda i: (i, 0)),
        compiler_params=_cparams("parallel"),
        name="mixer_out",
    )(res, a1, a2, w, w)


def _glu_kernel(res_ref, a_ref, wa_ref, wb_ref, o_ref):
    a = a_ref[...]
    za = _dot(a, wa_ref[...])
    zb = _dot(a, wb_ref[...])
    o_ref[...] = res_ref[...] + za * _sigmoid(zb)


def _glu_out(res, a, w):
    n, d = res.shape
    k = a.shape[1]
    tm = min(ROW_TILE, n)
    return pl.pallas_call(
        _glu_kernel,
        out_shape=jax.ShapeDtypeStruct((n, d), F32),
        grid=(n // tm,),
        in_specs=[
            pl.BlockSpec((tm, d), lambda i: (i, 0)),
            pl.BlockSpec((tm, k), lambda i: (i, 0)),
            pl.BlockSpec((k, d), lambda i: (0, 0)),
            pl.BlockSpec((k, d), lambda i: (0, 1)),
        ],
        out_specs=pl.BlockSpec((tm, d), lambda i: (i, 0)),
        compiler_params=_cparams("parallel"),
        name="glu_out",
    )(res, a, w, w)


def _ple_kernel(h_ref, g_ref, p_ref, wg_ref, wp_ref, o_ref):
    h = h_ref[...]
    gate = _sigmoid(_dot(_rms(h, g_ref[...]).astype(BF16), wg_ref[...]))
    proj = _dot(p_ref[...].astype(BF16), wp_ref[...])
    o_ref[...] = h + gate * proj


def _ple(h, g, p, wg, wp, layer):
    n, d = h.shape
    dp = p.shape[2]
    tm = min(ROW_TILE, n)
    return pl.pallas_call(
        _ple_kernel,
        out_shape=jax.ShapeDtypeStruct((n, d), F32),
        grid=(n // tm,),
        in_specs=[
            pl.BlockSpec((tm, d), lambda i: (i, 0)),
            pl.BlockSpec((1, d), lambda i: (0, 0)),
            pl.BlockSpec((None, tm, dp), lambda i: (layer, i, 0)),
            pl.BlockSpec((None, d, d), lambda i: (layer, 0, 0)),
            pl.BlockSpec((None, dp, d), lambda i: (layer, 0, 0)),
        ],
        out_specs=pl.BlockSpec((tm, d), lambda i: (i, 0)),
        compiler_params=_cparams("parallel"),
        name="ple",
    )(h, g.reshape(1, d), p, wg, wp)


def _bias_kernel(tbl_ref, o_ref):
    tq = o_ref.shape[1]
    tk = o_ref.shape[2]
    n_rel = tbl_ref.shape[2]
    width = pl.next_power_of_2(tq + tk)
    n_idx = lax.broadcasted_iota(jnp.int32, (n_rel, width), 1)
    c_idx = lax.broadcasted_iota(jnp.int32, (n_rel, width), 0)
    m = jnp.where(n_idx < tk, n_idx, n_idx - width)
    idx = jnp.clip(ATT_LEFT - m, -(CHUNK - 1), REL_CLIP) + (CHUNK - 1)
    onehot = (c_idx == idx).astype(F32)
    tbl = jnp.broadcast_to(tbl_ref[0], (8, n_rel))
    ext = jnp.dot(tbl, onehot, preferred_element_type=F32, precision=lax.Precision.HIGHEST)[0:1]
    x = jnp.broadcast_to(ext, (tq, width))
    rolled = pltpu.roll(x, 0, 1, stride=1, stride_axis=0)[:, :tk]
    qc = lax.broadcasted_iota(jnp.int32, (tq, tk), 0) // CHUNK
    kc = lax.broadcasted_iota(jnp.int32, (tq, tk), 1) // CHUNK
    ok = (kc >= qc) & (kc <= qc + N_LEFT_CHUNKS)
    o_ref[0] = jnp.where(ok, rolled * LOG2E, NEG)


def _bias_tiles(rel_bias):
    nh, n_rel = rel_bias.shape
    tk = ATT_SUB + ATT_LEFT
    return pl.pallas_call(
        _bias_kernel,
        out_shape=jax.ShapeDtypeStruct((nh, ATT_SUB, tk), F32),
        grid=(nh,),
        in_specs=[pl.BlockSpec((1, 1, n_rel), lambda h: (h, 0, 0))],
        out_specs=pl.BlockSpec((1, ATT_SUB, tk), lambda h: (h, 0, 0)),
        compiler_params=_cparams("parallel"),
        name="rel_bias_tiles",
    )(rel_bias.reshape(nh, 1, n_rel))


def _attn_kernel(q_ref, kp_ref, kc_ref, vp_ref, vc_ref, bias_ref, qg_ref, kg_ref, o_ref):
    has_prev = pl.program_id(2) > 0
    tq, w = q_ref.shape
    tk = bias_ref.shape[2]
    col = lax.broadcasted_iota(jnp.int32, (w // HEAD_DIM * ATT_SUB, tk), 1)
    head_of_lane = lax.broadcasted_iota(jnp.int32, (1, w), 1) // HEAD_DIM
    seg = (lax.broadcasted_iota(jnp.int32, (w, w), 0) // HEAD_DIM
           == lax.broadcasted_iota(jnp.int32, (w, w), 1) // HEAD_DIM)
    seg = jnp.where(seg, 1.0, 0.0).astype(BF16)

    def rms2(x, gain):
        sq = x * x
        hi = sq.astype(BF16)
        lo = (sq - hi.astype(F32)).astype(BF16)
        ss = _dot(jnp.concatenate([hi, lo], axis=1), jnp.concatenate([seg, seg], axis=0))
        return x * lax.rsqrt(ss * (1.0 / HEAD_DIM) + RMS_EPS) * gain

    q = rms2(q_ref[...], qg_ref[...] * (HEAD_DIM ** -0.5 * LOG2E))
    k = jnp.concatenate([rms2(kp_ref[...], kg_ref[...]), rms2(kc_ref[...], kg_ref[...])], axis=0).astype(BF16)
    v = jnp.concatenate([vp_ref[...], vc_ref[...]], axis=0).astype(BF16)
    nh = w // HEAD_DIM
    qh = [jnp.where(head_of_lane == hh, q, 0.0).astype(BF16) for hh in range(nh)]
    bias = jnp.concatenate([bias_ref[hh] for hh in range(nh)], axis=0)
    starts = [i * ATT_SUB for i in range(tq // ATT_SUB)]
    ob = []
    for g0 in range(0, len(starts), ATT_GROUP):
        jobs = starts[g0:g0 + ATT_GROUP]
        sc = [_dot_nt(jnp.concatenate([x[r0:r0 + ATT_SUB] for x in qh], axis=0), k[r0:r0 + tk]) + bias
              for r0 in jobs]
        sc = [jnp.where(has_prev | (col >= ATT_LEFT - r0), s, NEG) if r0 < ATT_LEFT else s
              for s, r0 in zip(sc, jobs)]
        pr = [jnp.exp2(s - jnp.max(s, axis=-1, keepdims=True)) for s in sc]
        den = [jnp.sum(p, axis=-1, keepdims=True) for p in pr]
        ob += [_dot(p.astype(BF16), v[r0:r0 + tk]) / d for p, d, r0 in zip(pr, den, jobs)]
    outs = []
    for o2 in ob:
        o = o2[:ATT_SUB]
        for hh in range(1, nh):
            o = jnp.where(head_of_lane == hh, o2[hh * ATT_SUB:(hh + 1) * ATT_SUB], o)
        outs.append(o)
    o_ref[...] = jnp.concatenate(outs, axis=0).astype(o_ref.dtype)


def _attention(proj, bias, q_gain, k_gain, bsz, seq, d_att):
    n = bsz * seq
    tq = ATT_TQ
    nqb = seq // tq
    npair = d_att // (2 * HEAD_DIM)
    w = 2 * HEAD_DIM

    def cur(col0):
        return pl.BlockSpec((tq, w), lambda hp, b, qb: (b * nqb + qb, col0 + hp))

    def prev(col0):
        return pl.BlockSpec((tq, w), lambda hp, b, qb: (b * nqb + jnp.maximum(qb - 1, 0), col0 + hp))

    return pl.pallas_call(
        _attn_kernel,
        out_shape=jax.ShapeDtypeStruct((n, d_att), BF16),
        grid=(npair, bsz, nqb),
        in_specs=[
            cur(0), prev(npair), cur(npair), prev(2 * npair), cur(2 * npair),
            pl.BlockSpec((2, ATT_SUB, ATT_SUB + ATT_LEFT), lambda hp, b, qb: (hp, 0, 0)),
            pl.BlockSpec((1, w), lambda hp, b, qb: (0, 0)),
            pl.BlockSpec((1, w), lambda hp, b, qb: (0, 0)),
        ],
        out_specs=pl.BlockSpec((tq, w), lambda hp, b, qb: (b * nqb + qb, hp)),
        compiler_params=_cparams("parallel", "parallel", "arbitrary"),
        name="band_attention",
    )(proj, proj, proj, proj, proj, bias, jnp.tile(q_gain, 2).reshape(1, w), jnp.tile(k_gain, 2).reshape(1, w))


def _split3(x):
    hi = x.astype(BF16)
    r1 = x - hi.astype(F32)
    mid = r1.astype(BF16)
    lo = (r1 - mid.astype(F32)).astype(BF16)
    return hi, mid, lo


def _token_shift(z, carry_ref, mu):
    rows = lax.broadcasted_iota(jnp.int32, z.shape, 0)
    prev = jnp.where(rows == 0, carry_ref[...], pltpu.roll(z, 1, 0))
    carry_ref[...] = z[z.shape[0] - 1:, :]
    return z + (prev - z) * mu


def _rwkv_kernel(r_ref, k_ref, v_ref, l_ref, mur_ref, muk_ref, muv_ref, mul_ref,
                 w0_ref, a0_ref, kk_ref, ka_ref, rk_ref, lnw_ref, lnb_ref,
                 wup_ref, aup_ref, gup_ref, o_ref,
                 s_scr, cr_scr, ck_scr, cv_scr, cl_scr):
    first = pl.program_id(1) == 0
    nb, L, d = r_ref.shape

    @pl.when(first)
    def _():
        s_scr[...] = jnp.zeros_like(s_scr)
        for c in (cr_scr, ck_scr, cv_scr, cl_scr):
            c[...] = jnp.zeros_like(c)

    def shifted(ref, carry, mu_ref):
        return jnp.concatenate([_token_shift(ref[b], carry.at[b], mu_ref[...]) for b in range(nb)], axis=0)

    r = shifted(r_ref, cr_scr, mur_ref)
    k = shifted(k_ref, ck_scr, muk_ref)
    v = shifted(v_ref, cv_scr, muv_ref)
    lo = shifted(l_ref, cl_scr, mul_ref)
    nw = wup_ref.shape[0]
    na = aup_ref.shape[0]
    xw, xa, xg = lo[:, :nw], lo[:, nw:nw + na], lo[:, nw + na:]

    wpre = w0_ref[...] + _bdot(jnp.tanh(xw), wup_ref[...])
    w_log = -(jnp.maximum(-wpre, 0.0) + jnp.log(1.0 + jnp.exp(-jnp.abs(wpre)))) - 0.5
    lw = -jnp.exp(w_log)
    a = _sigmoid(a0_ref[...] + _bdot(xa, aup_ref[...]))
    g = _bdot(_sigmoid(xg), gup_ref[...])

    kkf = k * kk_ref[...]
    k2 = k * (1.0 + (a - 1.0) * ka_ref[...])

    ti = lax.broadcasted_iota(jnp.int32, (nb * L, nb * L), 0)
    si = lax.broadcasted_iota(jnp.int32, (nb * L, nb * L), 1)
    tri = jnp.where((si <= ti) & (si // L == ti // L), 1.0, 0.0).astype(BF16)
    h3 = _split3(lw)
    cum = _dot(jnp.concatenate([tri, tri, tri], axis=1), jnp.concatenate(h3, axis=0))
    e_pos = jnp.exp(cum)
    e_neg = jnp.exp(-cum)
    e_prev = jnp.exp(cum - lw)

    rk = r * k2 * rk_ref[...]
    w2 = 2 * HEAD_DIM
    chains = [(b, j) for b in range(nb) for j in range(d // w2)]
    at = lambda x, c: x[c[0] * L:(c[0] + 1) * L, c[1] * w2:(c[1] + 1) * w2]
    lane = lax.broadcasted_iota(jnp.int32, (1, w2), 1)
    in_a = lane < HEAD_DIM
    row_l = lax.broadcasted_iota(jnp.int32, (L, w2), 0)
    idx_l = lax.broadcasted_iota(jnp.int32, (L, w2), 1) % HEAD_DIM
    strict = idx_l < row_l
    incl = idx_l <= row_l
    same_head = (lax.broadcasted_iota(jnp.int32, (w2, w2), 0) // HEAD_DIM
                 == lax.broadcasted_iota(jnp.int32, (w2, w2), 1) // HEAD_DIM)
    seg = jnp.where(same_head, 1.0, 0.0).astype(BF16)

    def segsum(xs):
        x = jnp.concatenate(xs, axis=0)
        hi = x.astype(BF16)
        lo = (x - hi.astype(F32)).astype(BF16)
        tot = _dot(jnp.concatenate([hi, lo], axis=1), jnp.concatenate([seg, seg], axis=0))
        return [tot[i * L:(i + 1) * L] for i in range(len(xs))]

    def bdiag(x):
        zero = jnp.zeros_like(x)
        return jnp.concatenate([jnp.where(in_a, x, zero), jnp.where(in_a, zero, x)], axis=0)

    def lower(x, mask):
        xb = x.astype(BF16)
        return jnp.where(mask, xb, jnp.zeros_like(xb))

    kss = segsum([at(kkf, c) * at(kkf, c) for c in chains])
    steps = max(1, (L - 1).bit_length())
    ar, bkc, rb, vb, vbd = [], [], [], [], []
    for c, ks in zip(chains, kss):
        kk = at(kkf, c) / jnp.maximum(jnp.sqrt(ks), 1e-12)
        ah = -kk * at(e_prev, c)
        bh = (kk * at(a, c) * at(e_neg, c)).astype(BF16)
        kh = (at(k2, c) * at(e_neg, c)).astype(BF16)
        rh = at(r, c) * at(e_pos, c)
        ar.append(jnp.concatenate([ah, rh], axis=0).astype(BF16))
        bkc.append(jnp.concatenate([bh, kh], axis=0))
        rb.append(jnp.concatenate([bdiag(bh), bdiag(kh)], axis=0))
        vb.append(at(v, c).astype(BF16))
        vbd.append(bdiag(vb[-1]))
    ids = range(len(chains))
    s0 = [s_scr[c] for c in chains]
    gm = [_dot_nt(ar[i], rb[i]) for i in ids]
    ps = [_dot_nt(ar[i], s0[i].astype(BF16)) for i in ids]
    pw = [lower(gm[i][:L, :w2], strict) for i in ids]
    u = [ps[i][:L] + _dot(lower(gm[i][:L, w2:], strict), vbd[i]) for i in ids]
    for st in range(steps):
        u = [u[i] + _dot(pw[i], bdiag(u[i].astype(BF16))) for i in ids]
        if st + 1 < steps:
            pw = [_dot(pw[i], bdiag(pw[i])).astype(BF16) for i in ids]
    y = []
    for i, c in zip(ids, chains):
        ub = u[i].astype(BF16)
        low = jnp.concatenate([lower(gm[i][L:, :w2], incl), lower(gm[i][L:, w2:], incl)], axis=1)
        uvd = jnp.concatenate([bdiag(ub), vbd[i]], axis=0)
        y.append(ps[i][L:] + _dot(low, uvd))
        uv = jnp.concatenate([ub, vb[i]], axis=0)
        w_last = at(e_pos, c)[L - 1:, :]
        s_scr[c] = jnp.where(same_head, s0[i] + _dot_tn(uv, bkc[i]), 0.0) * w_last
    yc = [yi - m * (1.0 / HEAD_DIM) for yi, m in zip(y, segsum(y))]
    var = segsum([c * c for c in yc])
    yn = [c * lax.rsqrt(vr * (1.0 / HEAD_DIM) + GN_EPS) for c, vr in zip(yc, var)]
    bs = segsum([at(rk, c) for c in chains])
    npair = d // w2
    grid2 = lambda xs: jnp.concatenate([jnp.concatenate(xs[b * npair:(b + 1) * npair], axis=1)
                                        for b in range(nb)], axis=0)
    out = (grid2(yn) * lnw_ref[...] + lnb_ref[...] + grid2(bs) * v) * g
    o_ref[...] = out.reshape(nb, L, d).astype(o_ref.dtype)


def _rwkv(proj, col0, bsz, seq, d, mu, w0, w_up, a0, a_up, g_up, k_k, k_a, r_k, lnx_w, lnx_b):
    L = RWKV_L
    nb = RWKV_NB if bsz % RWKV_NB == 0 else 1
    nt = seq // L
    nl = mu.shape[0] - 3 * d
    row = lambda x: x.reshape(1, -1)
    cb = col0 // d
    lb = (col0 + 3 * d) // nl
    proj3 = proj.reshape(bsz, seq, -1)

    def zspec(width, blk):
        return pl.BlockSpec((nb, L, width), lambda b, t: (b, t, blk))

    def pspec(shape):
        return pl.BlockSpec(shape, lambda b, t: (0,) * len(shape))

    npair = d // (2 * HEAD_DIM)
    out = pl.pallas_call(
        _rwkv_kernel,
        out_shape=jax.ShapeDtypeStruct((bsz, seq, d), BF16),
        grid=(bsz // nb, nt),
        in_specs=[
            zspec(d, cb), zspec(d, cb + 1), zspec(d, cb + 2), zspec(nl, lb),
            pspec((1, d)), pspec((1, d)), pspec((1, d)), pspec((1, nl)),
            pspec((1, d)), pspec((1, d)), pspec((1, d)), pspec((1, d)), pspec((1, d)),
            pspec((1, d)), pspec((1, d)),
            pspec(w_up.shape), pspec(a_up.shape), pspec(g_up.shape),
        ],
        out_specs=pl.BlockSpec((nb, L, d), lambda b, t: (b, t, 0)),
        scratch_shapes=[
            pltpu.VMEM((nb, npair, 2 * HEAD_DIM, 2 * HEAD_DIM), F32),
            pltpu.VMEM((nb, 1, d), F32), pltpu.VMEM((nb, 1, d), F32), pltpu.VMEM((nb, 1, d), F32),
            pltpu.VMEM((nb, 1, nl), F32),
        ],
        compiler_params=_cparams("parallel", "arbitrary"),
        name="rwkv7",
    )(proj3, proj3, proj3, proj3,
      row(mu[:d]), row(mu[d:2 * d]), row(mu[2 * d:3 * d]), row(mu[3 * d:]),
      row(w0), row(a0), row(k_k), row(k_a), row(r_k), row(lnx_w), row(lnx_b),
      w_up.astype(BF16), a_up.astype(BF16), g_up.astype(BF16))
    return out.reshape(bsz * seq, d)


SSM_GB = 8
SSM_NB = 2


def _cmul(ar, ai, br, bi):
    return ar * br - ai * bi, ar * bi + ai * br


def _s5_abar(lr, li, ldt):
    dt = jnp.exp(ldt)
    mag = jnp.exp(lr * dt)
    return mag * jnp.cos(li * dt), mag * jnp.sin(li * dt)


def _s5_prep_kernel(lrr_ref, lir_ref, ldtr_ref, lrc_ref, lic_ref, ldtc_ref,
                    bre_ref, bim_ref, cre_ref, cim_ref, k_ref, p_ref, q_ref):
    L = SSM_L
    w = k_ref.shape[2]
    lr, li = lrr_ref[0], lir_ref[0]
    a_re, a_im = _s5_abar(lr, li, ldtr_ref[0])
    den = lr * lr + li * li
    z_re = ((a_re - 1.0) * lr + a_im * li) / den
    z_im = (a_im * lr - (a_re - 1.0) * li) / den
    bb_re, bb_im = _cmul(z_re, z_im, bre_ref[0], bim_ref[0])
    cre, cim = cre_ref[0], cim_ref[0]
    ccat = jnp.concatenate([cre, -cim], axis=0)
    ac_re, ac_im = _s5_abar(lrc_ref[0], lic_ref[0], ldtc_ref[0])
    pr, pi = jnp.ones_like(a_re), jnp.zeros_like(a_im)
    qr, qi = ac_re, ac_im
    ptaus = []
    for tau in range(L):
        rb_re, rb_im = _cmul(pr, pi, bb_re, bb_im)
        ptau = jnp.concatenate([rb_re, rb_im], axis=1)
        ptaus.append(ptau)
        p_ref[0, (L - 1 - tau) * w:(L - tau) * w, :] = ptau.astype(p_ref.dtype)
        q_ref[0, :, tau * w:(tau + 1) * w] = jnp.concatenate(
            [cre * qr - cim * qi, -(cre * qi + cim * qr)], axis=0).astype(q_ref.dtype)
        pr, pi = _cmul(pr, pi, a_re, a_im)
        qr, qi = _cmul(qr, qi, ac_re, ac_im)
    pall = jnp.concatenate(ptaus[::-1], axis=0)
    p_hi = pall.astype(BF16)
    p_lo = (pall - p_hi.astype(F32)).astype(BF16)
    c_hi = ccat.astype(BF16)
    c_lo = (ccat - c_hi.astype(F32)).astype(BF16)
    k_ref[0, :L * w, :] = (_dot(p_hi, c_hi) + _dot(p_lo, c_hi) + _dot(p_hi, c_lo)).astype(k_ref.dtype)
    k_ref[0, L * w:, :] = jnp.zeros((w, w), k_ref.dtype)


def _s5_prep(lam_re, lam_im, log_dt, b_re, b_im, c_re, c_im):
    G, P = lam_re.shape
    gs = SSM_GROUP
    gb = SSM_GB
    J = G // gb
    L = SSM_L
    eye = jnp.eye(gb, dtype=F32)
    ldt = jnp.repeat(log_dt, P)

    def bdiag_b(b):
        bt = jnp.swapaxes(b, 1, 2).reshape(J, gb, gs, P)
        return (bt[:, :, :, None, :] * eye[None, :, None, :, None]).reshape(J, gb * gs, gb * P)

    def bdiag_c(c):
        ct = jnp.swapaxes(c, 1, 2).reshape(J, gb, P, gs)
        return (ct[:, :, :, None, :] * eye[None, :, None, :, None]).reshape(J, gb * P, gb * gs)

    row = lambda x: x.reshape(J, 1, gb * P)
    col = lambda x: x.reshape(J, gb * P, 1)

    def spec(*shape):
        return pl.BlockSpec((1,) + shape, lambda j: (j,) + (0,) * len(shape))

    w, sw = gb * gs, gb * P
    return pl.pallas_call(
        _s5_prep_kernel,
        out_shape=[jax.ShapeDtypeStruct((J, (L + 1) * w, w), BF16),
                   jax.ShapeDtypeStruct((J, L * w, 2 * sw), BF16),
                   jax.ShapeDtypeStruct((J, 2 * sw, L * w), BF16)],
        grid=(J,),
        in_specs=[spec(1, sw), spec(1, sw), spec(1, sw), spec(sw, 1), spec(sw, 1), spec(sw, 1),
                  spec(w, sw), spec(w, sw), spec(sw, w), spec(sw, w)],
        out_specs=[spec((L + 1) * w, w), spec(L * w, 2 * sw), spec(2 * sw, L * w)],
        compiler_params=_cparams("parallel"),
        name="s5_prep",
    )(row(lam_re), row(lam_im), row(ldt), col(lam_re), col(lam_im), col(ldt),
      bdiag_b(b_re), bdiag_b(b_im), bdiag_c(c_re), bdiag_c(c_im))


def _s5_kernel(u_ref, k_ref, p_ref, q_ref, lr_ref, li_ref, ldt_ref, d_ref, o_ref,
               gre_scr, gim_scr, hre_scr, him_scr, y_scr, *, nb):
    L = SSM_L
    nc = u_ref.shape[0] // L
    ncs = nc // nb
    sw = lr_ref.shape[2]
    w = u_ref.shape[1]
    us = [u_ref[pl.ds(s, nc, stride=L), :].astype(BF16) for s in range(L)]
    ucat = jnp.concatenate(us, axis=1)
    gall = _dot(ucat, p_ref[0])
    gre_scr[...] = gall[:, :sw]
    gim_scr[...] = gall[:, sw:]
    a_re, a_im = _s5_abar(lr_ref[0], li_ref[0], ldt_ref[0])
    al_re, al_im = a_re, a_im
    for _ in range(L.bit_length() - 1):
        al_re, al_im = _cmul(al_re, al_im, al_re, al_im)

    def body(c, carry):
        new = []
        for b, (hre, him) in enumerate(carry):
            row = b * ncs + c
            hre_scr[pl.ds(row, 1), :] = hre
            him_scr[pl.ds(row, 1), :] = him
            gr = gre_scr[pl.ds(row, 1), :]
            gi = gim_scr[pl.ds(row, 1), :]
            new.append((al_re * hre - al_im * him + gr, al_re * him + al_im * hre + gi))
        return tuple(new)

    zero = jnp.zeros((1, sw), F32)
    lax.fori_loop(0, ncs, body, tuple((zero, zero) for _ in range(nb)))
    hp = jnp.concatenate([hre_scr[...], him_scr[...]], axis=1).astype(BF16)
    ycarry = _dot(hp, q_ref[0])
    kall = k_ref[0]
    for t in range(0, L, 2):
        wt = jnp.concatenate([kall[(L - 1 - t) * w:, :], kall[(L - 2 - t) * w:L * w, :]], axis=1)
        yt = ycarry[:, t * w:(t + 2) * w] + _dot(ucat[:, :(t + 2) * w], wt)
        y_scr[pl.ds(t, nc, stride=L), :] = yt[:, :w]
        y_scr[pl.ds(t + 1, nc, stride=L), :] = yt[:, w:]
    y = y_scr[...] + d_ref[0] * u_ref[...]
    o_ref[...] = (0.5 * y * (1.0 + jnp.tanh(math.sqrt(2.0 / math.pi) * (y + 0.044715 * y * y * y)))).astype(o_ref.dtype)


def _s5(u, bsz, seq, mats, lam_re, lam_im, log_dt, d_skip):
    kb, pb, qb = mats
    G, P = lam_re.shape
    gb = SSM_GB
    J = G // gb
    L = SSM_L
    w = gb * SSM_GROUP
    sw = gb * P
    nb = SSM_NB if bsz % SSM_NB == 0 else 1
    rows = nb * seq
    nc = rows // L
    row = lambda x: x.reshape(J, 1, sw)

    def jspec(*shape):
        return pl.BlockSpec((1,) + shape, lambda j, b: (j,) + (0,) * len(shape))

    return pl.pallas_call(
        functools.partial(_s5_kernel, nb=nb),
        out_shape=jax.ShapeDtypeStruct((bsz * seq, G * SSM_GROUP), BF16),
        grid=(J, bsz // nb),
        in_specs=[
            pl.BlockSpec((rows, w), lambda j, b: (b, j)),
            jspec((L + 1) * w, w), jspec(L * w, 2 * sw), jspec(2 * sw, L * w),
            jspec(1, sw), jspec(1, sw), jspec(1, sw), jspec(1, w),
        ],
        out_specs=pl.BlockSpec((rows, w), lambda j, b: (b, j)),
        scratch_shapes=[pltpu.VMEM((nc, sw), F32)] * 4 + [pltpu.VMEM((rows, w), F32)],
        compiler_params=_cparams("parallel", "arbitrary"),
        name="s5_ssm",
    )(u, kb, pb, qb, row(lam_re), row(lam_im), row(jnp.repeat(log_dt, P)), d_skip.reshape(J, 1, w))


def kernel(x, p, ffn1_norm, ffn1_w_gate, ffn1_w_up, ffn1_w_down, mix_norm, ffn2_norm, ffn2_w_gate, ffn2_w_up, ffn2_w_down, ple_norm, ple_w_gate, ple_w_proj, ab_w_in, att_q_gain, att_k_gain, att_rel_bias, rwkv_mu, rwkv_w0, rwkv_w_up, rwkv_a0, rwkv_a_up, rwkv_g_up, rwkv_k_k, rwkv_k_a, rwkv_r_k, rwkv_lnx_w, rwkv_lnx_b, ab_w_out, ssm_w_in, ssm_lambda_re, ssm_lambda_im, ssm_log_dt, ssm_b_re, ssm_b_im, ssm_c_re, ssm_c_im, ssm_d, ssm_w_out):
    bsz, seq, d = x.shape
    depth = p.shape[0]
    n = bsz * seq
    bf = lambda w: w.astype(BF16)
    h = x.reshape(n, d)
    f1 = (ffn1_w_gate, ffn1_w_up, ffn1_w_down)
    f2 = (ffn2_w_gate, ffn2_w_up, ffn2_w_down)
    wcur = tuple(_cast_layer(w, 0) for w in f1)
    wpl = [bf(w) for w in (ple_w_gate, ple_w_proj)]
    pe = p.reshape(depth, n, -1)
    for i in range(depth):
        j = i // 2
        h, wcur = _ffn(h, ffn1_norm[i], wcur, 0, f2, i)
        if i % 2 == 0:
            d_att = att_rel_bias.shape[1] * HEAD_DIM
            d_rw = rwkv_w0.shape[1]
            n_in = ab_w_in.shape[2]
            tn = 1280 if n_in % 1280 == 0 else 128
            proj = _norm_matmul(h, mix_norm[i], bf(ab_w_in[j]), 2 * ROW_TILE, tn)
            bias = _bias_tiles(att_rel_bias[j])
            att = _attention(proj, bias, att_q_gain[j], att_k_gain[j], bsz, seq, d_att)
            rw = _rwkv(proj, 3 * d_att, bsz, seq, d_rw, rwkv_mu[j], rwkv_w0[j], rwkv_w_up[j],
                       rwkv_a0[j], rwkv_a_up[j], rwkv_g_up[j], rwkv_k_k[j], rwkv_k_a[j],
                       rwkv_r_k[j].reshape(-1), rwkv_lnx_w[j], rwkv_lnx_b[j])
            h = _out2(h, att, rw, bf(ab_w_out[j]))
        else:
            d_ssm = ssm_w_in.shape[2]
            u = _norm_matmul(h, mix_norm[i], bf(ssm_w_in[j]), ROW_TILE, d_ssm)
            mats = _s5_prep(ssm_lambda_re[j], ssm_lambda_im[j], ssm_log_dt[j], ssm_b_re[j],
                            ssm_b_im[j], ssm_c_re[j], ssm_c_im[j])
            y = _s5(u, bsz, seq, mats, ssm_lambda_re[j], ssm_lambda_im[j], ssm_log_dt[j], ssm_d[j])
            h = _glu_out(h, y, bf(ssm_w_out[j]))
        last = i + 1 == depth
        h, wcur = _ffn(h, ffn2_norm[i], wcur, 0, None if last else f1, i + 1)
        h = _ple(h, ple_norm[i], pe, wpl[0], wpl[1], i)
    return h.reshape(bsz, seq, d)
```

```python
import functools
import math

import jax
import jax.numpy as jnp
from jax import lax
from jax.experimental import pallas as pl
from jax.experimental.pallas import tpu as pltpu

F32 = jnp.float32
BF16 = jnp.bfloat16

RMS_EPS = 1e-6
GN_EPS = 64e-5
CHUNK = 64
N_LEFT_CHUNKS = 8
REL_CLIP = 128
HEAD_DIM = 64
ATT_LEFT = N_LEFT_CHUNKS * CHUNK
ATT_TQ = ATT_LEFT
ATT_SUB = 128
ATT_GROUP = 2
ATT_PAIRS = 8
RWKV_L = 64
RWKV_NB = 4
SSM_GROUP = 16
SSM_STATE = 64
SSM_L = 16
NEG = -1e30
LOG2E = math.log2(math.e)

ROW_TILE = 512
FFN_ROW_TILE = 1024
FFN_VMEM_LIMIT = 58 * 2 ** 20
COL_TILE = 512
VMEM_LIMIT = 52 * 2 ** 20


def _cparams(*sem):
    return pltpu.CompilerParams(dimension_semantics=sem, vmem_limit_bytes=VMEM_LIMIT)


def _dot(a, b):
    return jnp.dot(a, b, preferred_element_type=F32)


def _dot_nt(a, b):
    return lax.dot_general(a, b, (((1,), (1,)), ((), ())), preferred_element_type=F32)


def _dot_tn(a, b):
    return lax.dot_general(a, b, (((0,), (0,)), ((), ())), preferred_element_type=F32)


def _bdot(a, b):
    return _dot(a.astype(BF16), b.astype(BF16))


def _bdot_nt(a, b):
    return _dot_nt(a.astype(BF16), b.astype(BF16))


def _bdot_tn(a, b):
    return _dot_tn(a.astype(BF16), b.astype(BF16))


def _rms(x, g):
    return x * lax.rsqrt(jnp.mean(x * x, axis=-1, keepdims=True) + RMS_EPS) * g


def _sigmoid(x):
    return 1.0 / (1.0 + jnp.exp(-x))


def _ffn_kernel(*refs, n_cast):
    h_ref, g_ref, wg_ref, wu_ref, wd_ref = refs[:5]
    cast_in = refs[5:5 + n_cast]
    o_ref = refs[5 + n_cast]
    cast_out = refs[6 + n_cast:6 + 2 * n_cast]
    n_scr = refs[6 + 2 * n_cast]

    @pl.when(pl.program_id(1) == 0)
    def _():
        h = h_ref[...]
        n_scr[...] = _rms(h, g_ref[...]).astype(BF16)
        o_ref[...] = h

    n = n_scr[...]
    gate = _dot(n, wg_ref[...])
    up = _dot(n, wu_ref[...])
    act = (0.5 * gate * _sigmoid(gate) * up).astype(BF16)
    o_ref[...] += _dot(act, wd_ref[...])
    for src, dst in zip(cast_in, cast_out):
        dst[...] = src[...].astype(BF16)


def _ffn(h, g, w, layer, nxt=None, nxt_layer=0):
    wg, wu, wd = w
    n, d = h.shape
    dff = wg.shape[2]
    tm = min(FFN_ROW_TILE, n)
    tf = min(COL_TILE, dff)
    ni, nf = n // tm, dff // tf
    in_specs = [
        pl.BlockSpec((tm, d), lambda i, f: (i, 0)),
        pl.BlockSpec((1, d), lambda i, f: (0, 0)),
        pl.BlockSpec((None, d, tf), lambda i, f: (layer, 0, f)),
        pl.BlockSpec((None, d, tf), lambda i, f: (layer, 0, f)),
        pl.BlockSpec((None, tf, d), lambda i, f: (layer, f, 0)),
    ]
    out_shape = [jax.ShapeDtypeStruct((n, d), F32)]
    out_specs = [pl.BlockSpec((tm, d), lambda i, f: (i, 0))]
    args = [h, g.reshape(1, d), wg, wu, wd]
    if nxt is not None:
        rd, rf = d // ni, tf // ni
        up_blk, down_blk = (None, rd, tf), (None, rf, d)
        in_specs += [pl.BlockSpec(up_blk, lambda i, f: (nxt_layer, i, f)),
                     pl.BlockSpec(up_blk, lambda i, f: (nxt_layer, i, f)),
                     pl.BlockSpec(down_blk, lambda i, f: (nxt_layer, f * ni + i, 0))]
        out_specs += [pl.BlockSpec(up_blk, lambda i, f: (0, i, f)),
                      pl.BlockSpec(up_blk, lambda i, f: (0, i, f)),
                      pl.BlockSpec(down_blk, lambda i, f: (0, f * ni + i, 0))]
        out_shape += [jax.ShapeDtypeStruct((1, d, dff), BF16), jax.ShapeDtypeStruct((1, d, dff), BF16),
                      jax.ShapeDtypeStruct((1, dff, d), BF16)]
        args += list(nxt)
    outs = pl.pallas_call(
        functools.partial(_ffn_kernel, n_cast=0 if nxt is None else 3),
        out_shape=out_shape,
        grid=(ni, nf),
        in_specs=in_specs,
        out_specs=out_specs,
        scratch_shapes=[pltpu.VMEM((tm, d), BF16)],
        compiler_params=pltpu.CompilerParams(dimension_semantics=("parallel", "arbitrary"),
                                             vmem_limit_bytes=FFN_VMEM_LIMIT),
        name="ffn",
    )(*args)
    return outs[0], tuple(outs[1:])


def _cast_kernel(x_ref, o_ref):
    o_ref[...] = x_ref[...].astype(o_ref.dtype)


def _cast_layer(w, layer):
    _, r, c = w.shape
    tr = min(256, r)
    return pl.pallas_call(
        _cast_kernel,
        out_shape=jax.ShapeDtypeStruct((1, r, c), BF16),
        grid=(r // tr,),
        in_specs=[pl.BlockSpec((None, tr, c), lambda i: (layer, i, 0))],
        out_specs=pl.BlockSpec((None, tr, c), lambda i: (0, i, 0)),
        compiler_params=_cparams("parallel"),
        name="cast_layer",
    )(w)


def _nmm_kernel(x_ref, g_ref, w_ref, o_ref, n_scr):
    @pl.when(pl.program_id(1) == 0)
    def _():
        n_scr[...] = _rms(x_ref[...], g_ref[...]).astype(BF16)

    o_ref[...] = _dot(n_scr[...], w_ref[...]).astype(o_ref.dtype)


def _norm_matmul(x, g, w, tm, tn):
    n, d = x.shape
    nout = w.shape[1]
    tm = min(tm, n)
    return pl.pallas_call(
        _nmm_kernel,
        out_shape=jax.ShapeDtypeStruct((n, nout), F32),
        grid=(n // tm, nout // tn),
        in_specs=[
            pl.BlockSpec((tm, d), lambda i, j: (i, 0)),
            pl.BlockSpec((1, d), lambda i, j: (0, 0)),
            pl.BlockSpec((d, tn), lambda i, j: (0, j)),
        ],
        out_specs=pl.BlockSpec((tm, tn), lambda i, j: (i, j)),
        scratch_shapes=[pltpu.VMEM((tm, d), BF16)],
        compiler_params=_cparams("parallel", "arbitrary"),
        name="norm_matmul",
    )(x, g.reshape(1, d), w)


def _out2_kernel(res_ref, a1_ref, a2_ref, w1_ref, w2_ref, o_ref):
    o_ref[...] = res_ref[...] + _dot(a1_ref[...], w1_ref[...]) + _dot(a2_ref[...], w2_ref[...])


def _out2(res, a1, a2, w):
    n, d = res.shape
    k1 = a1.shape[1]
    tm = min(ROW_TILE, n)
    return pl.pallas_call(
        _out2_kernel,
        out_shape=jax.ShapeDtypeStruct((n, d), F32),
        grid=(n // tm,),
        in_specs=[
            pl.BlockSpec((tm, d), lambda i: (i, 0)),
            pl.BlockSpec((tm, k1), lambda i: (i, 0)),
            pl.BlockSpec((tm, k1), lambda i: (i, 0)),
            pl.BlockSpec((k1, d), lambda i: (0, 0)),
            pl.BlockSpec((k1, d), lambda i: (1, 0)),
        ],
        out_specs=pl.BlockSpec((tm, d), lambda i: (i, 0)),
        compiler_params=_cparams("parallel"),
        name="mixer_out",
    )(res, a1, a2, w, w)


def _glu_kernel(res_ref, a_ref, wa_ref, wb_ref, o_ref):
    a = a_ref[...]
    za = _dot(a, wa_ref[...])
    zb = _dot(a, wb_ref[...])
    o_ref[...] = res_ref[...] + za * _sigmoid(zb)


def _glu_out(res, a, w):
    n, d = res.shape
    k = a.shape[1]
    tm = min(ROW_TILE, n)
    return pl.pallas_call(
        _glu_kernel,
        out_shape=jax.ShapeDtypeStruct((n, d), F32),
        grid=(n // tm,),
        in_specs=[
            pl.BlockSpec((tm, d), lambda i: (i, 0)),
            pl.BlockSpec((tm, k), lambda i: (i, 0)),
            pl.BlockSpec((k, d), lambda i: (0, 0)),
            pl.BlockSpec((k, d), lambda i: (0, 1)),
        ],
        out_specs=pl.BlockSpec((tm, d), lambda i: (i, 0)),
        compiler_params=_cparams("parallel"),
        name="glu_out",
    )(res, a, w, w)


def _ple_kernel(h_ref, g_ref, p_ref, wg_ref, wp_ref, o_ref):
    h = h_ref[...]
    gate = _sigmoid(_dot(_rms(h, g_ref[...]).astype(BF16), wg_ref[...]))
    proj = _dot(p_ref[...].astype(BF16), wp_ref[...])
    o_ref[...] = h + gate * proj


def _ple(h, g, p, wg, wp, layer):
    n, d = h.shape
    dp = p.shape[2]
    tm = min(ROW_TILE, n)
    return pl.pallas_call(
        _ple_kernel,
        out_shape=jax.ShapeDtypeStruct((n, d), F32),
        grid=(n // tm,),
        in_specs=[
            pl.BlockSpec((tm, d), lambda i: (i, 0)),
            pl.BlockSpec((1, d), lambda i: (0, 0)),
            pl.BlockSpec((None, tm, dp), lambda i: (layer, i, 0)),
            pl.BlockSpec((None, d, d), lambda i: (layer, 0, 0)),
            pl.BlockSpec((None, dp, d), lambda i: (layer, 0, 0)),
        ],
        out_specs=pl.BlockSpec((tm, d), lambda i: (i, 0)),
        compiler_params=_cparams("parallel"),
        name="ple",
    )(h, g.reshape(1, d), p, wg, wp)


def _bias_kernel(tbl_ref, o_ref):
    tq = o_ref.shape[1]
    tk = o_ref.shape[2]
    n_rel = tbl_ref.shape[2]
    width = pl.next_power_of_2(tq + tk)
    n_idx = lax.broadcasted_iota(jnp.int32, (n_rel, width), 1)
    c_idx = lax.broadcasted_iota(jnp.int32, (n_rel, width), 0)
    m = jnp.where(n_idx < tk, n_idx, n_idx - width)
    idx = jnp.clip(ATT_LEFT - m, -(CHUNK - 1), REL_CLIP) + (CHUNK - 1)
    onehot = (c_idx == idx).astype(F32)
    tbl = jnp.broadcast_to(tbl_ref[0], (8, n_rel))
    ext = jnp.dot(tbl, onehot, preferred_element_type=F32, precision=lax.Precision.HIGHEST)[0:1]
    x = jnp.broadcast_to(ext, (tq, width))
    rolled = pltpu.roll(x, 0, 1, stride=1, stride_axis=0)[:, :tk]
    qc = lax.broadcasted_iota(jnp.int32, (tq, tk), 0) // CHUNK
    kc = lax.broadcasted_iota(jnp.int32, (tq, tk), 1) // CHUNK
    ok = (kc >= qc) & (kc <= qc + N_LEFT_CHUNKS)
    o_ref[0] = jnp.where(ok, rolled * LOG2E, NEG)


def _bias_tiles(rel_bias):
    nh, n_rel = rel_bias.shape
    tk = ATT_SUB + ATT_LEFT
    return pl.pallas_call(
        _bias_kernel,
        out_shape=jax.ShapeDtypeStruct((nh, ATT_SUB, tk), F32),
        grid=(nh,),
        in_specs=[pl.BlockSpec((1, 1, n_rel), lambda h: (h, 0, 0))],
        out_specs=pl.BlockSpec((1, ATT_SUB, tk), lambda h: (h, 0, 0)),
        compiler_params=_cparams("parallel"),
        name="rel_bias_tiles",
    )(rel_bias.reshape(nh, 1, n_rel))


def _attn_kernel(q_ref, kp_ref, kc_ref, vp_ref, vc_ref, bias_ref, qg_ref, kg_ref, o_ref):
    has_prev = pl.program_id(2) > 0
    tq = q_ref.shape[0]
    w = 2 * HEAD_DIM
    tk = bias_ref.shape[2]
    nh = w // HEAD_DIM
    col = lax.broadcasted_iota(jnp.int32, (nh * ATT_SUB, tk), 1)
    head_of_lane = lax.broadcasted_iota(jnp.int32, (1, w), 1) // HEAD_DIM
    seg = (lax.broadcasted_iota(jnp.int32, (w, w), 0) // HEAD_DIM
           == lax.broadcasted_iota(jnp.int32, (w, w), 1) // HEAD_DIM)
    seg = jnp.where(seg, 1.0, 0.0).astype(BF16)

    def rms2(x, gain):
        sq = x * x
        hi = sq.astype(BF16)
        lo = (sq - hi.astype(F32)).astype(BF16)
        ss = _dot(jnp.concatenate([hi, lo], axis=1), jnp.concatenate([seg, seg], axis=0))
        return x * lax.rsqrt(ss * (1.0 / HEAD_DIM) + RMS_EPS) * gain

    def one_pair(pi):
        ln = slice(pi * w, (pi + 1) * w)
        q = rms2(q_ref[:, ln], qg_ref[...] * (HEAD_DIM ** -0.5 * LOG2E))
        k = jnp.concatenate([rms2(kp_ref[:, ln], kg_ref[...]), rms2(kc_ref[:, ln], kg_ref[...])], axis=0).astype(BF16)
        v = jnp.concatenate([vp_ref[:, ln], vc_ref[:, ln]], axis=0).astype(BF16)
        qh = [jnp.where(head_of_lane == hh, q, 0.0).astype(BF16) for hh in range(nh)]
        bias = jnp.concatenate([bias_ref[pi * nh + hh] for hh in range(nh)], axis=0)
        starts = [i * ATT_SUB for i in range(tq // ATT_SUB)]
        ob = []
        for g0 in range(0, len(starts), ATT_GROUP):
            jobs = starts[g0:g0 + ATT_GROUP]
            sc = [_dot_nt(jnp.concatenate([x[r0:r0 + ATT_SUB] for x in qh], axis=0), k[r0:r0 + tk]) + bias
                  for r0 in jobs]
            sc = [jnp.where(has_prev | (col >= ATT_LEFT - r0), s, NEG) if r0 < ATT_LEFT else s
                  for s, r0 in zip(sc, jobs)]
            pr = [jnp.exp2(s - jnp.max(s, axis=-1, keepdims=True)) for s in sc]
            den = [jnp.sum(p, axis=-1, keepdims=True) for p in pr]
            ob += [_dot(p.astype(BF16), v[r0:r0 + tk]) / d for p, d, r0 in zip(pr, den, jobs)]
        outs = []
        for o2 in ob:
            o = o2[:ATT_SUB]
            for hh in range(1, nh):
                o = jnp.where(head_of_lane == hh, o2[hh * ATT_SUB:(hh + 1) * ATT_SUB], o)
            outs.append(o)
        o_ref[:, ln] = jnp.concatenate(outs, axis=0).astype(o_ref.dtype)

    for pi in range(q_ref.shape[1] // w):
        one_pair(pi)


def _attention(proj, bias, q_gain, k_gain, bsz, seq, d_att):
    n = bsz * seq
    tq = ATT_TQ
    nqb = seq // tq
    w = 2 * HEAD_DIM
    npair = d_att // w
    pps = ATT_PAIRS if npair % ATT_PAIRS == 0 else 1
    ngrp = npair // pps
    wb = pps * w

    def cur(col0):
        return pl.BlockSpec((tq, wb), lambda hp, b, qb: (b * nqb + qb, col0 + hp))

    def prev(col0):
        return pl.BlockSpec((tq, wb), lambda hp, b, qb: (b * nqb + jnp.maximum(qb - 1, 0), col0 + hp))

    return pl.pallas_call(
        _attn_kernel,
        out_shape=jax.ShapeDtypeStruct((n, d_att), BF16),
        grid=(ngrp, bsz, nqb),
        in_specs=[
            cur(0), prev(ngrp), cur(ngrp), prev(2 * ngrp), cur(2 * ngrp),
            pl.BlockSpec((2 * pps, ATT_SUB, ATT_SUB + ATT_LEFT), lambda hp, b, qb: (hp, 0, 0)),
            pl.BlockSpec((1, w), lambda hp, b, qb: (0, 0)),
            pl.BlockSpec((1, w), lambda hp, b, qb: (0, 0)),
        ],
        out_specs=pl.BlockSpec((tq, wb), lambda hp, b, qb: (b * nqb + qb, hp)),
        compiler_params=_cparams("parallel", "parallel", "arbitrary"),
        name="band_attention",
    )(proj, proj, proj, proj, proj, bias, jnp.tile(q_gain, 2).reshape(1, w), jnp.tile(k_gain, 2).reshape(1, w))


def _split3(x):
    hi = x.astype(BF16)
    r1 = x - hi.astype(F32)
    mid = r1.astype(BF16)
    lo = (r1 - mid.astype(F32)).astype(BF16)
    return hi, mid, lo


def _token_shift(z, carry_ref, mu):
    rows = lax.broadcasted_iota(jnp.int32, z.shape, 0)
    prev = jnp.where(rows == 0, carry_ref[...], pltpu.roll(z, 1, 0))
    carry_ref[...] = z[z.shape[0] - 1:, :]
    return z + (prev - z) * mu


def _rwkv_kernel(r_ref, k_ref, v_ref, l_ref, mur_ref, muk_ref, muv_ref, mul_ref,
                 w0_ref, a0_ref, kk_ref, ka_ref, rk_ref, lnw_ref, lnb_ref,
                 wup_ref, aup_ref, gup_ref, o_ref,
                 s_scr, cr_scr, ck_scr, cv_scr, cl_scr):
    first = pl.program_id(1) == 0
    nb, L, d = r_ref.shape

    @pl.when(first)
    def _():
        s_scr[...] = jnp.zeros_like(s_scr)
        for c in (cr_scr, ck_scr, cv_scr, cl_scr):
            c[...] = jnp.zeros_like(c)

    def shifted(ref, carry, mu_ref):
        return jnp.concatenate([_token_shift(ref[b], carry.at[b], mu_ref[...]) for b in range(nb)], axis=0)

    r = shifted(r_ref, cr_scr, mur_ref)
    k = shifted(k_ref, ck_scr, muk_ref)
    v = shifted(v_ref, cv_scr, muv_ref)
    lo = shifted(l_ref, cl_scr, mul_ref)
    nw = wup_ref.shape[0]
    na = aup_ref.shape[0]
    xw, xa, xg = lo[:, :nw], lo[:, nw:nw + na], lo[:, nw + na:]

    wpre = w0_ref[...] + _bdot(jnp.tanh(xw), wup_ref[...])
    w_log = -(jnp.maximum(-wpre, 0.0) + jnp.log(1.0 + jnp.exp(-jnp.abs(wpre)))) - 0.5
    lw = -jnp.exp(w_log)
    a = _sigmoid(a0_ref[...] + _bdot(xa, aup_ref[...]))
    g = _bdot(_sigmoid(xg), gup_ref[...])

    kkf = k * kk_ref[...]
    k2 = k * (1.0 + (a - 1.0) * ka_ref[...])

    ti = lax.broadcasted_iota(jnp.int32, (nb * L, nb * L), 0)
    si = lax.broadcasted_iota(jnp.int32, (nb * L, nb * L), 1)
    tri = jnp.where((si <= ti) & (si // L == ti // L), 1.0, 0.0).astype(BF16)
    h3 = _split3(lw)
    cum = _dot(jnp.concatenate([tri, tri, tri], axis=1), jnp.concatenate(h3, axis=0))
    e_pos = jnp.exp(cum)
    e_neg = jnp.exp(-cum)
    e_prev = jnp.exp(cum - lw)

    rk = r * k2 * rk_ref[...]
    w2 = 2 * HEAD_DIM
    chains = [(b, j) for b in range(nb) for j in range(d // w2)]
    at = lambda x, c: x[c[0] * L:(c[0] + 1) * L, c[1] * w2:(c[1] + 1) * w2]
    lane = lax.broadcasted_iota(jnp.int32, (1, w2), 1)
    in_a = lane < HEAD_DIM
    row_l = lax.broadcasted_iota(jnp.int32, (L, w2), 0)
    idx_l = lax.broadcasted_iota(jnp.int32, (L, w2), 1) % HEAD_DIM
    strict = idx_l < row_l
    incl = idx_l <= row_l
    same_head = (lax.broadcasted_iota(jnp.int32, (w2, w2), 0) // HEAD_DIM
                 == lax.broadcasted_iota(jnp.int32, (w2, w2), 1) // HEAD_DIM)
    seg = jnp.where(same_head, 1.0, 0.0).astype(BF16)

    def segsum(xs):
        x = jnp.concatenate(xs, axis=0)
        hi = x.astype(BF16)
        lo = (x - hi.astype(F32)).astype(BF16)
        tot = _dot(jnp.concatenate([hi, lo], axis=1), jnp.concatenate([seg, seg], axis=0))
        return [tot[i * L:(i + 1) * L] for i in range(len(xs))]

    def bdiag(x):
        zero = jnp.zeros_like(x)
        return jnp.concatenate([jnp.where(in_a, x, zero), jnp.where(in_a, zero, x)], axis=0)

    def lower(x, mask):
        xb = x.astype(BF16)
        return jnp.where(mask, xb, jnp.zeros_like(xb))

    kss = segsum([at(kkf, c) * at(kkf, c) for c in chains])
    steps = max(1, (L - 1).bit_length())
    ar, bkc, rb, vb, vbd = [], [], [], [], []
    for c, ks in zip(chains, kss):
        kk = at(kkf, c) / jnp.maximum(jnp.sqrt(ks), 1e-12)
        ah = -kk * at(e_prev, c)
        bh = (kk * at(a, c) * at(e_neg, c)).astype(BF16)
        kh = (at(k2, c) * at(e_neg, c)).astype(BF16)
        rh = at(r, c) * at(e_pos, c)
        ar.append(jnp.concatenate([ah, rh], axis=0).astype(BF16))
        bkc.append(jnp.concatenate([bh, kh], axis=0))
        rb.append(jnp.concatenate([bdiag(bh), bdiag(kh)], axis=0))
        vb.append(at(v, c).astype(BF16))
        vbd.append(bdiag(vb[-1]))
    ids = range(len(chains))
    s0 = [s_scr[c] for c in chains]
    gm = [_dot_nt(ar[i], rb[i]) for i in ids]
    ps = [_dot_nt(ar[i], s0[i].astype(BF16)) for i in ids]
    pw = [lower(gm[i][:L, :w2], strict) for i in ids]
    u = [ps[i][:L] + _dot(lower(gm[i][:L, w2:], strict), vbd[i]) for i in ids]
    for st in range(steps):
        u = [u[i] + _dot(pw[i], bdiag(u[i].astype(BF16))) for i in ids]
        if st + 1 < steps:
            pw = [_dot(pw[i], bdiag(pw[i])).astype(BF16) for i in ids]
    y = []
    for i, c in zip(ids, chains):
        ub = u[i].astype(BF16)
        low = jnp.concatenate([lower(gm[i][L:, :w2], incl), lower(gm[i][L:, w2:], incl)], axis=1)
        uvd = jnp.concatenate([bdiag(ub), vbd[i]], axis=0)
        y.append(ps[i][L:] + _dot(low, uvd))
        uv = jnp.concatenate([ub, vb[i]], axis=0)
        w_last = at(e_pos, c)[L - 1:, :]
        s_scr[c] = jnp.where(same_head, s0[i] + _dot_tn(uv, bkc[i]), 0.0) * w_last
    yc = [yi - m * (1.0 / HEAD_DIM) for yi, m in zip(y, segsum(y))]
    var = segsum([c * c for c in yc])
    yn = [c * lax.rsqrt(vr * (1.0 / HEAD_DIM) + GN_EPS) for c, vr in zip(yc, var)]
    bs = segsum([at(rk, c) for c in chains])
    npair = d // w2
    grid2 = lambda xs: jnp.concatenate([jnp.concatenate(xs[b * npair:(b + 1) * npair], axis=1)
                                        for b in range(nb)], axis=0)
    out = (grid2(yn) * lnw_ref[...] + lnb_ref[...] + grid2(bs) * v) * g
    o_ref[...] = out.reshape(nb, L, d).astype(o_ref.dtype)


def _rwkv(proj, col0, bsz, seq, d, mu, w0, w_up, a0, a_up, g_up, k_k, k_a, r_k, lnx_w, lnx_b):
    L = RWKV_L
    nb = RWKV_NB if bsz % RWKV_NB == 0 else 1
    nt = seq // L
    nl = mu.shape[0] - 3 * d
    row = lambda x: x.reshape(1, -1)
    cb = col0 // d
    lb = (col0 + 3 * d) // nl
    proj3 = proj.reshape(bsz, seq, -1)

    def zspec(width, blk):
        return pl.BlockSpec((nb, L, width), lambda b, t: (b, t, blk))

    def pspec(shape):
        return pl.BlockSpec(shape, lambda b, t: (0,) * len(shape))

    npair = d // (2 * HEAD_DIM)
    out = pl.pallas_call(
        _rwkv_kernel,
        out_shape=jax.ShapeDtypeStruct((bsz, seq, d), BF16),
        grid=(bsz // nb, nt),
        in_specs=[
            zspec(d, cb), zspec(d, cb + 1), zspec(d, cb + 2), zspec(nl, lb),
            pspec((1, d)), pspec((1, d)), pspec((1, d)), pspec((1, nl)),
            pspec((1, d)), pspec((1, d)), pspec((1, d)), pspec((1, d)), pspec((1, d)),
            pspec((1, d)), pspec((1, d)),
            pspec(w_up.shape), pspec(a_up.shape), pspec(g_up.shape),
        ],
        out_specs=pl.BlockSpec((nb, L, d), lambda b, t: (b, t, 0)),
        scratch_shapes=[
            pltpu.VMEM((nb, npair, 2 * HEAD_DIM, 2 * HEAD_DIM), F32),
            pltpu.VMEM((nb, 1, d), F32), pltpu.VMEM((nb, 1, d), F32), pltpu.VMEM((nb, 1, d), F32),
            pltpu.VMEM((nb, 1, nl), F32),
        ],
        compiler_params=_cparams("parallel", "arbitrary"),
        name="rwkv7",
    )(proj3, proj3, proj3, proj3,
      row(mu[:d]), row(mu[d:2 * d]), row(mu[2 * d:3 * d]), row(mu[3 * d:]),
      row(w0), row(a0), row(k_k), row(k_a), row(r_k), row(lnx_w), row(lnx_b),
      w_up.astype(BF16), a_up.astype(BF16), g_up.astype(BF16))
    return out.reshape(bsz * seq, d)


SSM_GB = 8
SSM_NB = 2


def _cmul(ar, ai, br, bi):
    return ar * br - ai * bi, ar * bi + ai * br


def _s5_abar(lr, li, ldt):
    dt = jnp.exp(ldt)
    mag = jnp.exp(lr * dt)
    return mag * jnp.cos(li * dt), mag * jnp.sin(li * dt)


def _s5_prep_kernel(lrr_ref, lir_ref, ldtr_ref, lrc_ref, lic_ref, ldtc_ref,
                    bre_ref, bim_ref, cre_ref, cim_ref, k_ref, p_ref, q_ref):
    L = SSM_L
    w = k_ref.shape[2]
    lr, li = lrr_ref[0], lir_ref[0]
    a_re, a_im = _s5_abar(lr, li, ldtr_ref[0])
    den = lr * lr + li * li
    z_re = ((a_re - 1.0) * lr + a_im * li) / den
    z_im = (a_im * lr - (a_re - 1.0) * li) / den
    bb_re, bb_im = _cmul(z_re, z_im, bre_ref[0], bim_ref[0])
    cre, cim = cre_ref[0], cim_ref[0]
    ccat = jnp.concatenate([cre, -cim], axis=0)
    ac_re, ac_im = _s5_abar(lrc_ref[0], lic_ref[0], ldtc_ref[0])
    pr, pi = jnp.ones_like(a_re), jnp.zeros_like(a_im)
    qr, qi = ac_re, ac_im
    ptaus = []
    for tau in range(L):
        rb_re, rb_im = _cmul(pr, pi, bb_re, bb_im)
        ptau = jnp.concatenate([rb_re, rb_im], axis=1)
        ptaus.append(ptau)
        p_ref[0, (L - 1 - tau) * w:(L - tau) * w, :] = ptau.astype(p_ref.dtype)
        q_ref[0, :, tau * w:(tau + 1) * w] = jnp.concatenate(
            [cre * qr - cim * qi, -(cre * qi + cim * qr)], axis=0).astype(q_ref.dtype)
        pr, pi = _cmul(pr, pi, a_re, a_im)
        qr, qi = _cmul(qr, qi, ac_re, ac_im)
    pall = jnp.concatenate(ptaus[::-1], axis=0)
    p_hi = pall.astype(BF16)
    p_lo = (pall - p_hi.astype(F32)).astype(BF16)
    c_hi = ccat.astype(BF16)
    c_lo = (ccat - c_hi.astype(F32)).astype(BF16)
    k_ref[0, :L * w, :] = (_dot(p_hi, c_hi) + _dot(p_lo, c_hi) + _dot(p_hi, c_lo)).astype(k_ref.dtype)
    k_ref[0, L * w:, :] = jnp.zeros((w, w), k_ref.dtype)


def _s5_prep(lam_re, lam_im, log_dt, b_re, b_im, c_re, c_im):
    G, P = lam_re.shape
    gs = SSM_GROUP
    gb = SSM_GB
    J = G // gb
    L = SSM_L
    eye = jnp.eye(gb, dtype=F32)
    ldt = jnp.repeat(log_dt, P)

    def bdiag_b(b):
        bt = jnp.swapaxes(b, 1, 2).reshape(J, gb, gs, P)
        return (bt[:, :, :, None, :] * eye[None, :, None, :, None]).reshape(J, gb * gs, gb * P)

    def bdiag_c(c):
        ct = jnp.swapaxes(c, 1, 2).reshape(J, gb, P, gs)
        return (ct[:, :, :, None, :] * eye[None, :, None, :, None]).reshape(J, gb * P, gb * gs)

    row = lambda x: x.reshape(J, 1, gb * P)
    col = lambda x: x.reshape(J, gb * P, 1)

    def spec(*shape):
        return pl.BlockSpec((1,) + shape, lambda j: (j,) + (0,) * len(shape))

    w, sw = gb * gs, gb * P
    return pl.pallas_call(
        _s5_prep_kernel,
        out_shape=[jax.ShapeDtypeStruct((J, (L + 1) * w, w), BF16),
                   jax.ShapeDtypeStruct((J, L * w, 2 * sw), BF16),
                   jax.ShapeDtypeStruct((J, 2 * sw, L * w), BF16)],
        grid=(J,),
        in_specs=[spec(1, sw), spec(1, sw), spec(1, sw), spec(sw, 1), spec(sw, 1), spec(sw, 1),
                  spec(w, sw), spec(w, sw), spec(sw, w), spec(sw, w)],
        out_specs=[spec((L + 1) * w, w), spec(L * w, 2 * sw), spec(2 * sw, L * w)],
        compiler_params=_cparams("parallel"),
        name="s5_prep",
    )(row(lam_re), row(lam_im), row(ldt), col(lam_re), col(lam_im), col(ldt),
      bdiag_b(b_re), bdiag_b(b_im), bdiag_c(c_re), bdiag_c(c_im))


def _s5_kernel(u_ref, k_ref, p_ref, q_ref, lr_ref, li_ref, ldt_ref, d_ref, o_ref,
               gre_scr, gim_scr, hre_scr, him_scr, y_scr, *, nb):
    L = SSM_L
    nc = u_ref.shape[0] // L
    ncs = nc // nb
    sw = lr_ref.shape[2]
    w = u_ref.shape[1]
    us = [u_ref[pl.ds(s, nc, stride=L), :].astype(BF16) for s in range(L)]
    ucat = jnp.concatenate(us, axis=1)
    gall = _dot(ucat, p_ref[0])
    gre_scr[...] = gall[:, :sw]
    gim_scr[...] = gall[:, sw:]
    a_re, a_im = _s5_abar(lr_ref[0], li_ref[0], ldt_ref[0])
    al_re, al_im = a_re, a_im
    for _ in range(L.bit_length() - 1):
        al_re, al_im = _cmul(al_re, al_im, al_re, al_im)

    def body(c, carry):
        new = []
        for b, (hre, him) in enumerate(carry):
            row = b * ncs + c
            hre_scr[pl.ds(row, 1), :] = hre
            him_scr[pl.ds(row, 1), :] = him
            gr = gre_scr[pl.ds(row, 1), :]
            gi = gim_scr[pl.ds(row, 1), :]
            new.append((al_re * hre - al_im * him + gr, al_re * him + al_im * hre + gi))
        return tuple(new)

    zero = jnp.zeros((1, sw), F32)
    lax.fori_loop(0, ncs, body, tuple((zero, zero) for _ in range(nb)))
    hp = jnp.concatenate([hre_scr[...], him_scr[...]], axis=1).astype(BF16)
    ycarry = _dot(hp, q_ref[0])
    kall = k_ref[0]
    for t in range(0, L, 2):
        wt = jnp.concatenate([kall[(L - 1 - t) * w:, :], kall[(L - 2 - t) * w:L * w, :]], axis=1)
        yt = ycarry[:, t * w:(t + 2) * w] + _dot(ucat[:, :(t + 2) * w], wt)
        y_scr[pl.ds(t, nc, stride=L), :] = yt[:, :w]
        y_scr[pl.ds(t + 1, nc, stride=L), :] = yt[:, w:]
    y = y_scr[...] + d_ref[0] * u_ref[...]
    o_ref[...] = (0.5 * y * (1.0 + jnp.tanh(math.sqrt(2.0 / math.pi) * (y + 0.044715 * y * y * y)))).astype(o_ref.dtype)


def _s5(u, bsz, seq, mats, lam_re, lam_im, log_dt, d_skip):
    kb, pb, qb = mats
    G, P = lam_re.shape
    gb = SSM_GB
    J = G // gb
    L = SSM_L
    w = gb * SSM_GROUP
    sw = gb * P
    nb = SSM_NB if bsz % SSM_NB == 0 else 1
    rows = nb * seq
    nc = rows // L
    row = lambda x: x.reshape(J, 1, sw)

    def jspec(*shape):
        return pl.BlockSpec((1,) + shape, lambda j, b: (j,) + (0,) * len(shape))

    return pl.pallas_call(
        functools.partial(_s5_kernel, nb=nb),
        out_shape=jax.ShapeDtypeStruct((bsz * seq, G * SSM_GROUP), BF16),
        grid=(J, bsz // nb),
        in_specs=[
            pl.BlockSpec((rows, w), lambda j, b: (b, j)),
            jspec((L + 1) * w, w), jspec(L * w, 2 * sw), jspec(2 * sw, L * w),
            jspec(1, sw), jspec(1, sw), jspec(1, sw), jspec(1, w),
        ],
        out_specs=pl.BlockSpec((rows, w), lambda j, b: (b, j)),
        scratch_shapes=[pltpu.VMEM((nc, sw), F32)] * 4 + [pltpu.VMEM((rows, w), F32)],
        compiler_params=_cparams("parallel", "arbitrary"),
        name="s5_ssm",
    )(u, kb, pb, qb, row(lam_re), row(lam_im), row(jnp.repeat(log_dt, P)), d_skip.reshape(J, 1, w))


def kernel(x, p, ffn1_norm, ffn1_w_gate, ffn1_w_up, ffn1_w_down, mix_norm, ffn2_norm, ffn2_w_gate, ffn2_w_up, ffn2_w_down, ple_norm, ple_w_gate, ple_w_proj, ab_w_in, att_q_gain, att_k_gain, att_rel_bias, rwkv_mu, rwkv_w0, rwkv_w_up, rwkv_a0, rwkv_a_up, rwkv_g_up, rwkv_k_k, rwkv_k_a, rwkv_r_k, rwkv_lnx_w, rwkv_lnx_b, ab_w_out, ssm_w_in, ssm_lambda_re, ssm_lambda_im, ssm_log_dt, ssm_b_re, ssm_b_im, ssm_c_re, ssm_c_im, ssm_d, ssm_w_out):
    bsz, seq, d = x.shape
    depth = p.shape[0]
    n = bsz * seq
    bf = lambda w: w.astype(BF16)
    h = x.reshape(n, d)
    f1 = (ffn1_w_gate, ffn1_w_up, ffn1_w_down)
    f2 = (ffn2_w_gate, ffn2_w_up, ffn2_w_down)
    wcur = tuple(_cast_layer(w, 0) for w in f1)
    wpl = [bf(w) for w in (ple_w_gate, ple_w_proj)]
    pe = p.reshape(depth, n, -1)
    for i in range(depth):
        j = i // 2
        h, wcur = _ffn(h, ffn1_norm[i], wcur, 0, f2, i)
        if i % 2 == 0:
            d_att = att_rel_bias.shape[1] * HEAD_DIM
            d_rw = rwkv_w0.shape[1]
            n_in = ab_w_in.shape[2]
            tn = 1280 if n_in % 1280 == 0 else 128
            proj = _norm_matmul(h, mix_norm[i], bf(ab_w_in[j]), 2 * ROW_TILE, tn)
            bias = _bias_tiles(att_rel_bias[j])
            att = _attention(proj, bias, att_q_gain[j], att_k_gain[j], bsz, seq, d_att)
            rw = _rwkv(proj, 3 * d_att, bsz, seq, d_rw, rwkv_mu[j], rwkv_w0[j], rwkv_w_up[j],
                       rwkv_a0[j], rwkv_a_up[j], rwkv_g_up[j], rwkv_k_k[j], rwkv_k_a[j],
                       rwkv_r_k[j].reshape(-1), rwkv_lnx_w[j], rwkv_lnx_b[j])
            h = _out2(h, att, rw, bf(ab_w_out[j]))
        else:
            d_ssm = ssm_w_in.shape[2]
            u = _norm_matmul(h, mix_norm[i], bf(ssm_w_in[j]), ROW_TILE, d_ssm)
            mats = _s5_prep(ssm_lambda_re[j], ssm_lambda_im[j], ssm_log_dt[j], ssm_b_re[j],
                            ssm_b_im[j], ssm_c_re[j], ssm_c_im[j])
            y = _s5(u, bsz, seq, mats, ssm_lambda_re[j], ssm_lambda_im[j], ssm_log_dt[j], ssm_d[j])
            h = _glu_out(h, y, bf(ssm_w_out[j]))
        last = i + 1 == depth
        h, wcur = _ffn(h, ffn2_norm[i], wcur, 0, None if last else f1, i + 1)
        h = _ple(h, ple_norm[i], pe, wpl[0], wpl[1], i)
    return h.reshape(bsz, seq, d)
```

```python
import functools
import math

import jax
import jax.numpy as jnp
from jax import lax
from jax.experimental import pallas as pl
from jax.experimental.pallas import tpu as pltpu

F32 = jnp.float32
BF16 = jnp.bfloat16

RMS_EPS = 1e-6
GN_EPS = 64e-5
CHUNK = 64
N_LEFT_CHUNKS = 8
REL_CLIP = 128
HEAD_DIM = 64
ATT_LEFT = N_LEFT_CHUNKS * CHUNK
ATT_TQ = ATT_LEFT
ATT_SUB = 128
ATT_GROUP = 2
ATT_PAIRS = 8
RWKV_L = 64
RWKV_NB = 4
SSM_GROUP = 16
SSM_L = 16
NEG = -1e30
LOG2E = math.log2(math.e)

ROW_TILE = 512
FFN_ROW_TILE = 1024
FFN_VMEM_LIMIT = 58 * 2 ** 20
COL_TILE = 512
VMEM_LIMIT = 52 * 2 ** 20


def _cparams(*sem):
    return pltpu.CompilerParams(dimension_semantics=sem, vmem_limit_bytes=VMEM_LIMIT)


def _dot(a, b):
    return jnp.dot(a, b, preferred_element_type=F32)


def _dot_nt(a, b):
    return lax.dot_general(a, b, (((1,), (1,)), ((), ())), preferred_element_type=F32)


def _dot_tn(a, b):
    return lax.dot_general(a, b, (((0,), (0,)), ((), ())), preferred_element_type=F32)


def _bdot(a, b):
    return _dot(a.astype(BF16), b.astype(BF16))


def _rms(x, g):
    return x * lax.rsqrt(jnp.mean(x * x, axis=-1, keepdims=True) + RMS_EPS) * g


def _sigmoid(x):
    return 1.0 / (1.0 + jnp.exp(-x))


def _ffn_kernel(*refs, n_cast):
    h_ref, g_ref, wg_ref, wu_ref, wd_ref = refs[:5]
    cast_in = refs[5:5 + n_cast]
    o_ref = refs[5 + n_cast]
    cast_out = refs[6 + n_cast:6 + 2 * n_cast]
    n_scr = refs[6 + 2 * n_cast]

    @pl.when(pl.program_id(1) == 0)
    def _():
        h = h_ref[...]
        n_scr[...] = _rms(h, g_ref[...]).astype(BF16)
        o_ref[...] = h

    n = n_scr[...]
    gate = _dot(n, wg_ref[...])
    up = _dot(n, wu_ref[...])
    act = (0.5 * gate * _sigmoid(gate) * up).astype(BF16)
    o_ref[...] += _dot(act, wd_ref[...])
    for src, dst in zip(cast_in, cast_out):
        dst[...] = src[...].astype(BF16)


def _ffn(h, g, w, layer, nxt=None, nxt_layer=0):
    wg, wu, wd = w
    n, d = h.shape
    dff = wg.shape[2]
    tm = min(FFN_ROW_TILE, n)
    tf = min(COL_TILE, dff)
    ni, nf = n // tm, dff // tf
    in_specs = [
        pl.BlockSpec((tm, d), lambda i, f: (i, 0)),
        pl.BlockSpec((1, d), lambda i, f: (0, 0)),
        pl.BlockSpec((None, d, tf), lambda i, f: (layer, 0, f)),
        pl.BlockSpec((None, d, tf), lambda i, f: (layer, 0, f)),
        pl.BlockSpec((None, tf, d), lambda i, f: (layer, f, 0)),
    ]
    out_shape = [jax.ShapeDtypeStruct((n, d), F32)]
    out_specs = [pl.BlockSpec((tm, d), lambda i, f: (i, 0))]
    args = [h, g.reshape(1, d), wg, wu, wd]
    if nxt is not None:
        rd, rf = d // ni, tf // ni
        up_blk, down_blk = (None, rd, tf), (None, rf, d)
        in_specs += [pl.BlockSpec(up_blk, lambda i, f: (nxt_layer, i, f)),
                     pl.BlockSpec(up_blk, lambda i, f: (nxt_layer, i, f)),
                     pl.BlockSpec(down_blk, lambda i, f: (nxt_layer, f * ni + i, 0))]
        out_specs += [pl.BlockSpec(up_blk, lambda i, f: (0, i, f)),
                      pl.BlockSpec(up_blk, lambda i, f: (0, i, f)),
                      pl.BlockSpec(down_blk, lambda i, f: (0, f * ni + i, 0))]
        out_shape += [jax.ShapeDtypeStruct((1, d, dff), BF16), jax.ShapeDtypeStruct((1, d, dff), BF16),
                      jax.ShapeDtypeStruct((1, dff, d), BF16)]
        args += list(nxt)
    outs = pl.pallas_call(
        functools.partial(_ffn_kernel, n_cast=0 if nxt is None else 3),
        out_shape=out_shape,
        grid=(ni, nf),
        in_specs=in_specs,
        out_specs=out_specs,
        scratch_shapes=[pltpu.VMEM((tm, d), BF16)],
        compiler_params=pltpu.CompilerParams(dimension_semantics=("parallel", "arbitrary"),
                                             vmem_limit_bytes=FFN_VMEM_LIMIT),
        name="ffn",
    )(*args)
    return outs[0], tuple(outs[1:])


def _cast_kernel(x_ref, o_ref):
    o_ref[...] = x_ref[...].astype(o_ref.dtype)


def _cast_layer(w, layer):
    _, r, c = w.shape
    tr = min(256, r)
    return pl.pallas_call(
        _cast_kernel,
        out_shape=jax.ShapeDtypeStruct((1, r, c), BF16),
        grid=(r // tr,),
        in_specs=[pl.BlockSpec((None, tr, c), lambda i: (layer, i, 0))],
        out_specs=pl.BlockSpec((None, tr, c), lambda i: (0, i, 0)),
        compiler_params=_cparams("parallel"),
        name="cast_layer",
    )(w)


def _nmm_kernel(x_ref, g_ref, w_ref, o_ref, n_scr):
    @pl.when(pl.program_id(1) == 0)
    def _():
        n_scr[...] = _rms(x_ref[...], g_ref[...]).astype(BF16)

    o_ref[...] = _dot(n_scr[...], w_ref[...]).astype(o_ref.dtype)


def _norm_matmul(x, g, w, tm, tn):
    n, d = x.shape
    nout = w.shape[1]
    tm = min(tm, n)
    return pl.pallas_call(
        _nmm_kernel,
        out_shape=jax.ShapeDtypeStruct((n, nout), F32),
        grid=(n // tm, nout // tn),
        in_specs=[
            pl.BlockSpec((tm, d), lambda i, j: (i, 0)),
            pl.BlockSpec((1, d), lambda i, j: (0, 0)),
            pl.BlockSpec((d, tn), lambda i, j: (0, j)),
        ],
        out_specs=pl.BlockSpec((tm, tn), lambda i, j: (i, j)),
        scratch_shapes=[pltpu.VMEM((tm, d), BF16)],
        compiler_params=_cparams("parallel", "arbitrary"),
        name="norm_matmul",
    )(x, g.reshape(1, d), w)


def _out2_kernel(res_ref, a1_ref, a2_ref, w1_ref, w2_ref, o_ref):
    o_ref[...] = res_ref[...] + _dot(a1_ref[...], w1_ref[...]) + _dot(a2_ref[...], w2_ref[...])


def _out2(res, a1, a2, w):
    n, d = res.shape
    k1 = a1.shape[1]
    tm = min(ROW_TILE, n)
    return pl.pallas_call(
        _out2_kernel,
        out_shape=jax.ShapeDtypeStruct((n, d), F32),
        grid=(n // tm,),
        in_specs=[
            pl.BlockSpec((tm, d), lambda i: (i, 0)),
            pl.BlockSpec((tm, k1), lambda i: (i, 0)),
            pl.BlockSpec((tm, k1), lambda i: (i, 0)),
            pl.BlockSpec((k1, d), lambda i: (0, 0)),
            pl.BlockSpec((k1, d), lambda i: (1, 0)),
        ],
        out_specs=pl.BlockSpec((tm, d), lambda i: (i, 0)),
        compiler_params=_cparams("parallel"),
        name="mixer_out",
    )(res, a1, a2, w, w)


def _glu_kernel(res_ref, a_ref, wa_ref, wb_ref, o_ref):
    a = a_ref[...]
    za = _dot(a, wa_ref[...])
    zb = _dot(a, wb_ref[...])
    o_ref[...] = res_ref[...] + za * _sigmoid(zb)


def _glu_out(res, a, w):
    n, d = res.shape
    k = a.shape[1]
    tm = min(ROW_TILE, n)
    return pl.pallas_call(
        _glu_kernel,
        out_shape=jax.ShapeDtypeStruct((n, d), F32),
        grid=(n // tm,),
        in_specs=[
            pl.BlockSpec((tm, d), lambda i: (i, 0)),
            pl.BlockSpec((tm, k), lambda i: (i, 0)),
            pl.BlockSpec((k, d), lambda i: (0, 0)),
            pl.BlockSpec((k, d), lambda i: (0, 1)),
        ],
        out_specs=pl.BlockSpec((tm, d), lambda i: (i, 0)),
        compiler_params=_cparams("parallel"),
        name="glu_out",
    )(res, a, w, w)


def _ple_kernel(h_ref, g_ref, p_ref, wg_ref, wp_ref, o_ref):
    h = h_ref[...]
    gate = _sigmoid(_dot(_rms(h, g_ref[...]).astype(BF16), wg_ref[...]))
    proj = _dot(p_ref[...].astype(BF16), wp_ref[...])
    o_ref[...] = h + gate * proj


def _ple(h, g, p, wg, wp, layer):
    n, d = h.shape
    dp = p.shape[2]
    tm = min(ROW_TILE, n)
    return pl.pallas_call(
        _ple_kernel,
        out_shape=jax.ShapeDtypeStruct((n, d), F32),
        grid=(n // tm,),
        in_specs=[
            pl.BlockSpec((tm, d), lambda i: (i, 0)),
            pl.BlockSpec((1, d), lambda i: (0, 0)),
            pl.BlockSpec((None, tm, dp), lambda i: (layer, i, 0)),
            pl.BlockSpec((None, d, d), lambda i: (layer, 0, 0)),
            pl.BlockSpec((None, dp, d), lambda i: (layer, 0, 0)),
        ],
        out_specs=pl.BlockSpec((tm, d), lambda i: (i, 0)),
        compiler_params=_cparams("parallel"),
        name="ple",
    )(h, g.reshape(1, d), p, wg, wp)


def _bias_kernel(tbl_ref, o_ref):
    tq = o_ref.shape[1]
    tk = o_ref.shape[2]
    n_rel = tbl_ref.shape[2]
    width = pl.next_power_of_2(tq + tk)
    n_idx = lax.broadcasted_iota(jnp.int32, (n_rel, width), 1)
    c_idx = lax.broadcasted_iota(jnp.int32, (n_rel, width), 0)
    m = jnp.where(n_idx < tk, n_idx, n_idx - width)
    idx = jnp.clip(ATT_LEFT - m, -(CHUNK - 1), REL_CLIP) + (CHUNK - 1)
    onehot = (c_idx == idx).astype(F32)
    tbl = jnp.broadcast_to(tbl_ref[0], (8, n_rel))
    ext = jnp.dot(tbl, onehot, preferred_element_type=F32, precision=lax.Precision.HIGHEST)[0:1]
    x = jnp.broadcast_to(ext, (tq, width))
    rolled = pltpu.roll(x, 0, 1, stride=1, stride_axis=0)[:, :tk]
    qc = lax.broadcasted_iota(jnp.int32, (tq, tk), 0) // CHUNK
    kc = lax.broadcasted_iota(jnp.int32, (tq, tk), 1) // CHUNK
    ok = (kc >= qc) & (kc <= qc + N_LEFT_CHUNKS)
    o_ref[0] = jnp.where(ok, rolled * LOG2E, NEG)


def _bias_tiles(rel_bias):
    nh, n_rel = rel_bias.shape
    tk = ATT_SUB + ATT_LEFT
    return pl.pallas_call(
        _bias_kernel,
        out_shape=jax.ShapeDtypeStruct((nh, ATT_SUB, tk), F32),
        grid=(nh,),
        in_specs=[pl.BlockSpec((1, 1, n_rel), lambda h: (h, 0, 0))],
        out_specs=pl.BlockSpec((1, ATT_SUB, tk), lambda h: (h, 0, 0)),
        compiler_params=_cparams("parallel"),
        name="rel_bias_tiles",
    )(rel_bias.reshape(nh, 1, n_rel))


def _attn_kernel(q_ref, kp_ref, kc_ref, vp_ref, vc_ref, bias_ref, qg_ref, kg_ref, o_ref):
    has_prev = pl.program_id(2) > 0
    tq = q_ref.shape[0]
    w = 2 * HEAD_DIM
    tk = bias_ref.shape[2]
    nh = w // HEAD_DIM
    col = lax.broadcasted_iota(jnp.int32, (nh * ATT_SUB, tk), 1)
    head_of_lane = lax.broadcasted_iota(jnp.int32, (1, w), 1) // HEAD_DIM
    seg = (lax.broadcasted_iota(jnp.int32, (w, w), 0) // HEAD_DIM
           == lax.broadcasted_iota(jnp.int32, (w, w), 1) // HEAD_DIM)
    seg = jnp.where(seg, 1.0, 0.0).astype(BF16)

    def rms2(x, gain):
        sq = x * x
        hi = sq.astype(BF16)
        lo = (sq - hi.astype(F32)).astype(BF16)
        ss = _dot(jnp.concatenate([hi, lo], axis=1), jnp.concatenate([seg, seg], axis=0))
        return x * lax.rsqrt(ss * (1.0 / HEAD_DIM) + RMS_EPS) * gain

    def one_pair(pi):
        ln = slice(pi * w, (pi + 1) * w)
        q = rms2(q_ref[:, ln], qg_ref[...] * (HEAD_DIM ** -0.5 * LOG2E))
        k = jnp.concatenate([rms2(kp_ref[:, ln], kg_ref[...]), rms2(kc_ref[:, ln], kg_ref[...])], axis=0).astype(BF16)
        v = jnp.concatenate([vp_ref[:, ln], vc_ref[:, ln]], axis=0).astype(BF16)
        qh = [jnp.where(head_of_lane == hh, q, 0.0).astype(BF16) for hh in range(nh)]
        bias = jnp.concatenate([bias_ref[pi * nh + hh] for hh in range(nh)], axis=0)
        starts = [i * ATT_SUB for i in range(tq // ATT_SUB)]
        ob = []
        for g0 in range(0, len(starts), ATT_GROUP):
            jobs = starts[g0:g0 + ATT_GROUP]
            sc = [_dot_nt(jnp.concatenate([x[r0:r0 + ATT_SUB] for x in qh], axis=0), k[r0:r0 + tk]) + bias
                  for r0 in jobs]
            sc = [jnp.where(has_prev | (col >= ATT_LEFT - r0), s, NEG) if r0 < ATT_LEFT else s
                  for s, r0 in zip(sc, jobs)]
            pr = [jnp.exp2(s - jnp.max(s, axis=-1, keepdims=True)) for s in sc]
            den = [jnp.sum(p, axis=-1, keepdims=True) for p in pr]
            ob += [_dot(p.astype(BF16), v[r0:r0 + tk]) / d for p, d, r0 in zip(pr, den, jobs)]
        outs = []
        for o2 in ob:
            o = o2[:ATT_SUB]
            for hh in range(1, nh):
                o = jnp.where(head_of_lane == hh, o2[hh * ATT_SUB:(hh + 1) * ATT_SUB], o)
            outs.append(o)
        o_ref[:, ln] = jnp.concatenate(outs, axis=0).astype(o_ref.dtype)

    for pi in range(q_ref.shape[1] // w):
        one_pair(pi)


def _attention(proj, bias, q_gain, k_gain, bsz, seq, d_att):
    n = bsz * seq
    tq = ATT_TQ
    nqb = seq // tq
    w = 2 * HEAD_DIM
    npair = d_att // w
    pps = ATT_PAIRS if npair % ATT_PAIRS == 0 else 1
    ngrp = npair // pps
    wb = pps * w

    def cur(col0):
        return pl.BlockSpec((tq, wb), lambda hp, b, qb: (b * nqb + qb, col0 + hp))

    def prev(col0):
        return pl.BlockSpec((tq, wb), lambda hp, b, qb: (b * nqb + jnp.maximum(qb - 1, 0), col0 + hp))

    return pl.pallas_call(
        _attn_kernel,
        out_shape=jax.ShapeDtypeStruct((n, d_att), BF16),
        grid=(ngrp, bsz, nqb),
        in_specs=[
            cur(0), prev(ngrp), cur(ngrp), prev(2 * ngrp), cur(2 * ngrp),
            pl.BlockSpec((2 * pps, ATT_SUB, ATT_SUB + ATT_LEFT), lambda hp, b, qb: (hp, 0, 0)),
            pl.BlockSpec((1, w), lambda hp, b, qb: (0, 0)),
            pl.BlockSpec((1, w), lambda hp, b, qb: (0, 0)),
        ],
        out_specs=pl.BlockSpec((tq, wb), lambda hp, b, qb: (b * nqb + qb, hp)),
        compiler_params=_cparams("parallel", "parallel", "arbitrary"),
        name="band_attention",
    )(proj, proj, proj, proj, proj, bias, jnp.tile(q_gain, 2).reshape(1, w), jnp.tile(k_gain, 2).reshape(1, w))


def _split3(x):
    hi = x.astype(BF16)
    r1 = x - hi.astype(F32)
    mid = r1.astype(BF16)
    lo = (r1 - mid.astype(F32)).astype(BF16)
    return hi, mid, lo


def _token_shift(z, carry_ref, mu):
    rows = lax.broadcasted_iota(jnp.int32, z.shape, 0)
    prev = jnp.where(rows == 0, carry_ref[...], pltpu.roll(z, 1, 0))
    carry_ref[...] = z[z.shape[0] - 1:, :]
    return z + (prev - z) * mu


def _rwkv_kernel(r_ref, k_ref, v_ref, l_ref, mur_ref, muk_ref, muv_ref, mul_ref,
                 w0_ref, a0_ref, kk_ref, ka_ref, rk_ref, lnw_ref, lnb_ref,
                 wup_ref, aup_ref, gup_ref, o_ref,
                 s_scr, cr_scr, ck_scr, cv_scr, cl_scr):
    first = pl.program_id(1) == 0
    nb, L, d = r_ref.shape

    @pl.when(first)
    def _():
        s_scr[...] = jnp.zeros_like(s_scr)
        for c in (cr_scr, ck_scr, cv_scr, cl_scr):
            c[...] = jnp.zeros_like(c)

    def shifted(ref, carry, mu_ref):
        return jnp.concatenate([_token_shift(ref[b], carry.at[b], mu_ref[...]) for b in range(nb)], axis=0)

    r = shifted(r_ref, cr_scr, mur_ref)
    k = shifted(k_ref, ck_scr, muk_ref)
    v = shifted(v_ref, cv_scr, muv_ref)
    lo = shifted(l_ref, cl_scr, mul_ref)
    nw = wup_ref.shape[0]
    na = aup_ref.shape[0]
    xw, xa, xg = lo[:, :nw], lo[:, nw:nw + na], lo[:, nw + na:]

    wpre = w0_ref[...] + _bdot(jnp.tanh(xw), wup_ref[...])
    w_log = -(jnp.maximum(-wpre, 0.0) + jnp.log(1.0 + jnp.exp(-jnp.abs(wpre)))) - 0.5
    lw = -jnp.exp(w_log)
    a = _sigmoid(a0_ref[...] + _bdot(xa, aup_ref[...]))
    g = _bdot(_sigmoid(xg), gup_ref[...])

    kkf = k * kk_ref[...]
    k2 = k * (1.0 + (a - 1.0) * ka_ref[...])

    ti = lax.broadcasted_iota(jnp.int32, (nb * L, nb * L), 0)
    si = lax.broadcasted_iota(jnp.int32, (nb * L, nb * L), 1)
    tri = jnp.where((si <= ti) & (si // L == ti // L), 1.0, 0.0).astype(BF16)
    h3 = _split3(lw)
    cum = _dot(jnp.concatenate([tri, tri, tri], axis=1), jnp.concatenate(h3, axis=0))
    e_pos = jnp.exp(cum)
    e_neg = jnp.exp(-cum)
    e_prev = jnp.exp(cum - lw)

    rk = r * k2 * rk_ref[...]
    w2 = 2 * HEAD_DIM
    chains = [(b, j) for b in range(nb) for j in range(d // w2)]
    at = lambda x, c: x[c[0] * L:(c[0] + 1) * L, c[1] * w2:(c[1] + 1) * w2]
    lane = lax.broadcasted_iota(jnp.int32, (1, w2), 1)
    in_a = lane < HEAD_DIM
    row_l = lax.broadcasted_iota(jnp.int32, (L, w2), 0)
    idx_l = lax.broadcasted_iota(jnp.int32, (L, w2), 1) % HEAD_DIM
    strict = idx_l < row_l
    incl = idx_l <= row_l
    same_head = (lax.broadcasted_iota(jnp.int32, (w2, w2), 0) // HEAD_DIM
                 == lax.broadcasted_iota(jnp.int32, (w2, w2), 1) // HEAD_DIM)
    seg = jnp.where(same_head, 1.0, 0.0).astype(BF16)

    def segsum(xs):
        x = jnp.concatenate(xs, axis=0)
        hi = x.astype(BF16)
        lo = (x - hi.astype(F32)).astype(BF16)
        tot = _dot(jnp.concatenate([hi, lo], axis=1), jnp.concatenate([seg, seg], axis=0))
        return [tot[i * L:(i + 1) * L] for i in range(len(xs))]

    def bdiag(x):
        zero = jnp.zeros_like(x)
        return jnp.concatenate([jnp.where(in_a, x, zero), jnp.where(in_a, zero, x)], axis=0)

    def lower(x, mask):
        xb = x.astype(BF16)
        return jnp.where(mask, xb, jnp.zeros_like(xb))

    kss = segsum([at(kkf, c) * at(kkf, c) for c in chains])
    steps = max(1, (L - 1).bit_length())
    ar, bkc, rb, vb, vbd = [], [], [], [], []
    for c, ks in zip(chains, kss):
        kk = at(kkf, c) / jnp.maximum(jnp.sqrt(ks), 1e-12)
        ah = -kk * at(e_prev, c)
        bh = (kk * at(a, c) * at(e_neg, c)).astype(BF16)
        kh = (at(k2, c) * at(e_neg, c)).astype(BF16)
        rh = at(r, c) * at(e_pos, c)
        ar.append(jnp.concatenate([ah, rh], axis=0).astype(BF16))
        bkc.append(jnp.concatenate([bh, kh], axis=0))
        rb.append(jnp.concatenate([bdiag(bh), bdiag(kh)], axis=0))
        vb.append(at(v, c).astype(BF16))
        vbd.append(bdiag(vb[-1]))
    ids = range(len(chains))
    s0 = [s_scr[c] for c in chains]
    gm = [_dot_nt(ar[i], rb[i]) for i in ids]
    ps = [_dot_nt(ar[i], s0[i].astype(BF16)) for i in ids]
    pw = [lower(gm[i][:L, :w2], strict) for i in ids]
    u = [ps[i][:L] + _dot(lower(gm[i][:L, w2:], strict), vbd[i]) for i in ids]
    for st in range(steps):
        u = [u[i] + _dot(pw[i], bdiag(u[i].astype(BF16))) for i in ids]
        if st + 1 < steps:
            pw = [_dot(pw[i], bdiag(pw[i])).astype(BF16) for i in ids]
    y = []
    for i, c in zip(ids, chains):
        ub = u[i].astype(BF16)
        low = jnp.concatenate([lower(gm[i][L:, :w2], incl), lower(gm[i][L:, w2:], incl)], axis=1)
        uvd = jnp.concatenate([bdiag(ub), vbd[i]], axis=0)
        y.append(ps[i][L:] + _dot(low, uvd))
        uv = jnp.concatenate([ub, vb[i]], axis=0)
        w_last = at(e_pos, c)[L - 1:, :]
        s_scr[c] = jnp.where(same_head, s0[i] + _dot_tn(uv, bkc[i]), 0.0) * w_last
    yc = [yi - m * (1.0 / HEAD_DIM) for yi, m in zip(y, segsum(y))]
    var = segsum([c * c for c in yc])
    yn = [c * lax.rsqrt(vr * (1.0 / HEAD_DIM) + GN_EPS) for c, vr in zip(yc, var)]
    bs = segsum([at(rk, c) for c in chains])
    npair = d // w2
    grid2 = lambda xs: jnp.concatenate([jnp.concatenate(xs[b * npair:(b + 1) * npair], axis=1)
                                        for b in range(nb)], axis=0)
    out = (grid2(yn) * lnw_ref[...] + lnb_ref[...] + grid2(bs) * v) * g
    o_ref[...] = out.reshape(nb, L, d).astype(o_ref.dtype)


def _rwkv(proj, col0, bsz, seq, d, mu, w0, w_up, a0, a_up, g_up, k_k, k_a, r_k, lnx_w, lnx_b):
    L = RWKV_L
    nb = RWKV_NB if bsz % RWKV_NB == 0 else 1
    nt = seq // L
    nl = mu.shape[0] - 3 * d
    row = lambda x: x.reshape(1, -1)
    cb = col0 // d
    lb = (col0 + 3 * d) // nl
    proj3 = proj.reshape(bsz, seq, -1)

    def zspec(width, blk):
        return pl.BlockSpec((nb, L, width), lambda b, t: (b, t, blk))

    def pspec(shape):
        return pl.BlockSpec(shape, lambda b, t: (0,) * len(shape))

    npair = d // (2 * HEAD_DIM)
    out = pl.pallas_call(
        _rwkv_kernel,
        out_shape=jax.ShapeDtypeStruct((bsz, seq, d), BF16),
        grid=(bsz // nb, nt),
        in_specs=[
            zspec(d, cb), zspec(d, cb + 1), zspec(d, cb + 2), zspec(nl, lb),
            pspec((1, d)), pspec((1, d)), pspec((1, d)), pspec((1, nl)),
            pspec((1, d)), pspec((1, d)), pspec((1, d)), pspec((1, d)), pspec((1, d)),
            pspec((1, d)), pspec((1, d)),
            pspec(w_up.shape), pspec(a_up.shape), pspec(g_up.shape),
        ],
        out_specs=pl.BlockSpec((nb, L, d), lambda b, t: (b, t, 0)),
        scratch_shapes=[
            pltpu.VMEM((nb, npair, 2 * HEAD_DIM, 2 * HEAD_DIM), F32),
            pltpu.VMEM((nb, 1, d), F32), pltpu.VMEM((nb, 1, d), F32), pltpu.VMEM((nb, 1, d), F32),
            pltpu.VMEM((nb, 1, nl), F32),
        ],
        compiler_params=_cparams("parallel", "arbitrary"),
        name="rwkv7",
    )(proj3, proj3, proj3, proj3,
      row(mu[:d]), row(mu[d:2 * d]), row(mu[2 * d:3 * d]), row(mu[3 * d:]),
      row(w0), row(a0), row(k_k), row(k_a), row(r_k), row(lnx_w), row(lnx_b),
      w_up.astype(BF16), a_up.astype(BF16), g_up.astype(BF16))
    return out.reshape(bsz * seq, d)


SSM_GB = 8
SSM_NB = 2


def _cmul(ar, ai, br, bi):
    return ar * br - ai * bi, ar * bi + ai * br


def _s5_abar(lr, li, ldt):
    dt = jnp.exp(ldt)
    mag = jnp.exp(lr * dt)
    return mag * jnp.cos(li * dt), mag * jnp.sin(li * dt)


def _s5_prep_kernel(lrr_ref, lir_ref, ldtr_ref, lrc_ref, lic_ref, ldtc_ref,
                    bre_ref, bim_ref, cre_ref, cim_ref, k_ref, p_ref, q_ref):
    L = SSM_L
    w = k_ref.shape[2]
    lr, li = lrr_ref[0], lir_ref[0]
    a_re, a_im = _s5_abar(lr, li, ldtr_ref[0])
    den = lr * lr + li * li
    z_re = ((a_re - 1.0) * lr + a_im * li) / den
    z_im = (a_im * lr - (a_re - 1.0) * li) / den
    bb_re, bb_im = _cmul(z_re, z_im, bre_ref[0], bim_ref[0])
    cre, cim = cre_ref[0], cim_ref[0]
    ccat = jnp.concatenate([cre, -cim], axis=0)
    ac_re, ac_im = _s5_abar(lrc_ref[0], lic_ref[0], ldtc_ref[0])
    pr, pi = jnp.ones_like(a_re), jnp.zeros_like(a_im)
    qr, qi = ac_re, ac_im
    ptaus = []
    for tau in range(L):
        rb_re, rb_im = _cmul(pr, pi, bb_re, bb_im)
        ptau = jnp.concatenate([rb_re, rb_im], axis=1)
        ptaus.append(ptau)
        p_ref[0, (L - 1 - tau) * w:(L - tau) * w, :] = ptau.astype(p_ref.dtype)
        q_ref[0, :, tau * w:(tau + 1) * w] = jnp.concatenate(
            [cre * qr - cim * qi, -(cre * qi + cim * qr)], axis=0).astype(q_ref.dtype)
        pr, pi = _cmul(pr, pi, a_re, a_im)
        qr, qi = _cmul(qr, qi, ac_re, ac_im)
    pall = jnp.concatenate(ptaus[::-1], axis=0)
    p_hi = pall.astype(BF16)
    p_lo = (pall - p_hi.astype(F32)).astype(BF16)
    c_hi = ccat.astype(BF16)
    c_lo = (ccat - c_hi.astype(F32)).astype(BF16)
    k_ref[0, :L * w, :] = (_dot(p_hi, c_hi) + _dot(p_lo, c_hi) + _dot(p_hi, c_lo)).astype(k_ref.dtype)
    k_ref[0, L * w:, :] = jnp.zeros((w, w), k_ref.dtype)


def _s5_prep(lam_re, lam_im, log_dt, b_re, b_im, c_re, c_im):
    G, P = lam_re.shape
    gs = SSM_GROUP
    gb = SSM_GB
    J = G // gb
    L = SSM_L
    eye = jnp.eye(gb, dtype=F32)
    ldt = jnp.repeat(log_dt, P)

    def bdiag_b(b):
        bt = jnp.swapaxes(b, 1, 2).reshape(J, gb, gs, P)
        return (bt[:, :, :, None, :] * eye[None, :, None, :, None]).reshape(J, gb * gs, gb * P)

    def bdiag_c(c):
        ct = jnp.swapaxes(c, 1, 2).reshape(J, gb, P, gs)
        return (ct[:, :, :, None, :] * eye[None, :, None, :, None]).reshape(J, gb * P, gb * gs)

    row = lambda x: x.reshape(J, 1, gb * P)
    col = lambda x: x.reshape(J, gb * P, 1)

    def spec(*shape):
        return pl.BlockSpec((1,) + shape, lambda j: (j,) + (0,) * len(shape))

    w, sw = gb * gs, gb * P
    return pl.pallas_call(
        _s5_prep_kernel,
        out_shape=[jax.ShapeDtypeStruct((J, (L + 1) * w, w), BF16),
                   jax.ShapeDtypeStruct((J, L * w, 2 * sw), BF16),
                   jax.ShapeDtypeStruct((J, 2 * sw, L * w), BF16)],
        grid=(J,),
        in_specs=[spec(1, sw), spec(1, sw), spec(1, sw), spec(sw, 1), spec(sw, 1), spec(sw, 1),
                  spec(w, sw), spec(w, sw), spec(sw, w), spec(sw, w)],
        out_specs=[spec((L + 1) * w, w), spec(L * w, 2 * sw), spec(2 * sw, L * w)],
        compiler_params=_cparams("parallel"),
        name="s5_prep",
    )(row(lam_re), row(lam_im), row(ldt), col(lam_re), col(lam_im), col(ldt),
      bdiag_b(b_re), bdiag_b(b_im), bdiag_c(c_re), bdiag_c(c_im))


def _s5_kernel(u_ref, k_ref, p_ref, q_ref, lr_ref, li_ref, ldt_ref, d_ref, o_ref,
               gre_scr, gim_scr, hre_scr, him_scr, y_scr, *, nb):
    L = SSM_L
    nc = u_ref.shape[0] // L
    ncs = nc // nb
    sw = lr_ref.shape[2]
    w = u_ref.shape[1]
    us = [u_ref[pl.ds(s, nc, stride=L), :].astype(BF16) for s in range(L)]
    ucat = jnp.concatenate(us, axis=1)
    gall = _dot(ucat, p_ref[0])
    gre_scr[...] = gall[:, :sw]
    gim_scr[...] = gall[:, sw:]
    a_re, a_im = _s5_abar(lr_ref[0], li_ref[0], ldt_ref[0])
    al_re, al_im = a_re, a_im
    for _ in range(L.bit_length() - 1):
        al_re, al_im = _cmul(al_re, al_im, al_re, al_im)

    def body(c, carry):
        new = []
        for b, (hre, him) in enumerate(carry):
            row = b * ncs + c
            hre_scr[pl.ds(row, 1), :] = hre
            him_scr[pl.ds(row, 1), :] = him
            gr = gre_scr[pl.ds(row, 1), :]
            gi = gim_scr[pl.ds(row, 1), :]
            new.append((al_re * hre - al_im * him + gr, al_re * him + al_im * hre + gi))
        return tuple(new)

    zero = jnp.zeros((1, sw), F32)
    lax.fori_loop(0, ncs, body, tuple((zero, zero) for _ in range(nb)))
    hp = jnp.concatenate([hre_scr[...], him_scr[...]], axis=1).astype(BF16)
    ycarry = _dot(hp, q_ref[0])
    kall = k_ref[0]
    for t in range(0, L, 2):
        wt = jnp.concatenate([kall[(L - 1 - t) * w:, :], kall[(L - 2 - t) * w:L * w, :]], axis=1)
        yt = ycarry[:, t * w:(t + 2) * w] + _dot(ucat[:, :(t + 2) * w], wt)
        y_scr[pl.ds(t, nc, stride=L), :] = yt[:, :w]
        y_scr[pl.ds(t + 1, nc, stride=L), :] = yt[:, w:]
    y = y_scr[...] + d_ref[0] * u_ref[...]
    o_ref[...] = (0.5 * y * (1.0 + jnp.tanh(math.sqrt(2.0 / math.pi) * (y + 0.044715 * y * y * y)))).astype(o_ref.dtype)


def _s5(u, bsz, seq, mats, lam_re, lam_im, log_dt, d_skip):
    kb, pb, qb = mats
    G, P = lam_re.shape
    gb = SSM_GB
    J = G // gb
    L = SSM_L
    w = gb * SSM_GROUP
    sw = gb * P
    nb = SSM_NB if bsz % SSM_NB == 0 else 1
    rows = nb * seq
    nc = rows // L
    row = lambda x: x.reshape(J, 1, sw)

    def jspec(*shape):
        return pl.BlockSpec((1,) + shape, lambda j, b: (j,) + (0,) * len(shape))

    return pl.pallas_call(
        functools.partial(_s5_kernel, nb=nb),
        out_shape=jax.ShapeDtypeStruct((bsz * seq, G * SSM_GROUP), BF16),
        grid=(J, bsz // nb),
        in_specs=[
            pl.BlockSpec((rows, w), lambda j, b: (b, j)),
            jspec((L + 1) * w, w), jspec(L * w, 2 * sw), jspec(2 * sw, L * w),
            jspec(1, sw), jspec(1, sw), jspec(1, sw), jspec(1, w),
        ],
        out_specs=pl.BlockSpec((rows, w), lambda j, b: (b, j)),
        scratch_shapes=[pltpu.VMEM((nc, sw), F32)] * 4 + [pltpu.VMEM((rows, w), F32)],
        compiler_params=_cparams("parallel", "arbitrary"),
        name="s5_ssm",
    )(u, kb, pb, qb, row(lam_re), row(lam_im), row(jnp.repeat(log_dt, P)), d_skip.reshape(J, 1, w))


def kernel(x, p, ffn1_norm, ffn1_w_gate, ffn1_w_up, ffn1_w_down, mix_norm, ffn2_norm, ffn2_w_gate, ffn2_w_up, ffn2_w_down, ple_norm, ple_w_gate, ple_w_proj, ab_w_in, att_q_gain, att_k_gain, att_rel_bias, rwkv_mu, rwkv_w0, rwkv_w_up, rwkv_a0, rwkv_a_up, rwkv_g_up, rwkv_k_k, rwkv_k_a, rwkv_r_k, rwkv_lnx_w, rwkv_lnx_b, ab_w_out, ssm_w_in, ssm_lambda_re, ssm_lambda_im, ssm_log_dt, ssm_b_re, ssm_b_im, ssm_c_re, ssm_c_im, ssm_d, ssm_w_out):
    bsz, seq, d = x.shape
    depth = p.shape[0]
    n = bsz * seq
    bf = lambda w: w.astype(BF16)
    h = x.reshape(n, d)
    f1 = (ffn1_w_gate, ffn1_w_up, ffn1_w_down)
    f2 = (ffn2_w_gate, ffn2_w_up, ffn2_w_down)
    wcur = tuple(_cast_layer(w, 0) for w in f1)
    wpl = [bf(w) for w in (ple_w_gate, ple_w_proj)]
    pe = p.reshape(depth, n, -1)
    for i in range(depth):
        j = i // 2
        h, wcur = _ffn(h, ffn1_norm[i], wcur, 0, f2, i)
        if i % 2 == 0:
            d_att = att_rel_bias.shape[1] * HEAD_DIM
            d_rw = rwkv_w0.shape[1]
            n_in = ab_w_in.shape[2]
            tn = 1280 if n_in % 1280 == 0 else 128
            proj = _norm_matmul(h, mix_norm[i], bf(ab_w_in[j]), 2 * ROW_TILE, tn)
            bias = _bias_tiles(att_rel_bias[j])
            att = _attention(proj, bias, att_q_gain[j], att_k_gain[j], bsz, seq, d_att)
            rw = _rwkv(proj, 3 * d_att, bsz, seq, d_rw, rwkv_mu[j], rwkv_w0[j], rwkv_w_up[j],
                       rwkv_a0[j], rwkv_a_up[j], rwkv_g_up[j], rwkv_k_k[j], rwkv_k_a[j],
                       rwkv_r_k[j].reshape(-1), rwkv_lnx_w[j], rwkv_lnx_b[j])
            h = _out2(h, att, rw, bf(ab_w_out[j]))
        else:
            d_ssm = ssm_w_in.shape[2]
            u = _norm_matmul(h, mix_norm[i], bf(ssm_w_in[j]), ROW_TILE, d_ssm)
            mats = _s5_prep(ssm_lambda_re[j], ssm_lambda_im[j], ssm_log_dt[j], ssm_b_re[j],
                            ssm_b_im[j], ssm_c_re[j], ssm_c_im[j])
            y = _s5(u, bsz, seq, mats, ssm_lambda_re[j], ssm_lambda_im[j], ssm_log_dt[j], ssm_d[j])
            h = _glu_out(h, y, bf(ssm_w_out[j]))
        last = i + 1 == depth
        h, wcur = _ffn(h, ffn2_norm[i], wcur, 0, None if last else f1, i + 1)
        h = _ple(h, ple_norm[i], pe, wpl[0], wpl[1], i)
    return h.reshape(bsz, seq, d)
```

```python
import functools
import math

import jax
import jax.numpy as jnp
from jax import lax
from jax.experimental import pallas as pl
from jax.experimental.pallas import tpu as pltpu

F32 = jnp.float32
BF16 = jnp.bfloat16

RMS_EPS = 1e-6
GN_EPS = 64e-5
CHUNK = 64
N_LEFT_CHUNKS = 8
REL_CLIP = 128
HEAD_DIM = 64
ATT_LEFT = N_LEFT_CHUNKS * CHUNK
ATT_TQ = ATT_LEFT
ATT_SUB = 128
ATT_GROUP = 2
ATT_PAIRS = 8
RWKV_L = 64
RWKV_NB = 4
SSM_GROUP = 16
SSM_L = 16
NEG = -1e30
LOG2E = math.log2(math.e)

ROW_TILE = 512
FFN_ROW_TILE = 1024
FFN_VMEM_LIMIT = 58 * 2 ** 20
COL_TILE = 512
VMEM_LIMIT = 52 * 2 ** 20


def _cparams(*sem):
    return pltpu.CompilerParams(dimension_semantics=sem, vmem_limit_bytes=VMEM_LIMIT)


def _dot(a, b):
    return jnp.dot(a, b, preferred_element_type=F32)


def _dot_nt(a, b):
    return lax.dot_general(a, b, (((1,), (1,)), ((), ())), preferred_element_type=F32)


def _dot_tn(a, b):
    return lax.dot_general(a, b, (((0,), (0,)), ((), ())), preferred_element_type=F32)


def _bdot(a, b):
    return _dot(a.astype(BF16), b.astype(BF16))


def _rms(x, g):
    return x * lax.rsqrt(jnp.mean(x * x, axis=-1, keepdims=True) + RMS_EPS) * g


def _sigmoid(x):
    return 0.5 * jnp.tanh(0.5 * x) + 0.5


def _ffn_kernel(*refs, n_cast):
    h_ref, g_ref, wg_ref, wu_ref, wd_ref = refs[:5]
    cast_in = refs[5:5 + n_cast]
    o_ref = refs[5 + n_cast]
    cast_out = refs[6 + n_cast:6 + 2 * n_cast]
    n_scr = refs[6 + 2 * n_cast]

    @pl.when(pl.program_id(1) == 0)
    def _():
        h = h_ref[...]
        n_scr[...] = _rms(h, g_ref[...]).astype(BF16)
        o_ref[...] = h

    n = n_scr[...]
    gate = _dot(n, wg_ref[...])
    up = _dot(n, wu_ref[...])
    act = (0.5 * gate * _sigmoid(gate) * up).astype(BF16)
    o_ref[...] += _dot(act, wd_ref[...])
    for src, dst in zip(cast_in, cast_out):
        dst[...] = src[...].astype(BF16)


def _ffn(h, g, w, layer, nxt=None, nxt_layer=0):
    wg, wu, wd = w
    n, d = h.shape
    dff = wg.shape[2]
    tm = min(FFN_ROW_TILE, n)
    tf = min(COL_TILE, dff)
    ni, nf = n // tm, dff // tf
    in_specs = [
        pl.BlockSpec((tm, d), lambda i, f: (i, 0)),
        pl.BlockSpec((1, d), lambda i, f: (0, 0)),
        pl.BlockSpec((None, d, tf), lambda i, f: (layer, 0, f)),
        pl.BlockSpec((None, d, tf), lambda i, f: (layer, 0, f)),
        pl.BlockSpec((None, tf, d), lambda i, f: (layer, f, 0)),
    ]
    out_shape = [jax.ShapeDtypeStruct((n, d), F32)]
    out_specs = [pl.BlockSpec((tm, d), lambda i, f: (i, 0))]
    args = [h, g.reshape(1, d), wg, wu, wd]
    if nxt is not None:
        rd, rf = d // ni, tf // ni
        up_blk, down_blk = (None, rd, tf), (None, rf, d)
        in_specs += [pl.BlockSpec(up_blk, lambda i, f: (nxt_layer, i, f)),
                     pl.BlockSpec(up_blk, lambda i, f: (nxt_layer, i, f)),
                     pl.BlockSpec(down_blk, lambda i, f: (nxt_layer, f * ni + i, 0))]
        out_specs += [pl.BlockSpec(up_blk, lambda i, f: (0, i, f)),
                      pl.BlockSpec(up_blk, lambda i, f: (0, i, f)),
                      pl.BlockSpec(down_blk, lambda i, f: (0, f * ni + i, 0))]
        out_shape += [jax.ShapeDtypeStruct((1, d, dff), BF16), jax.ShapeDtypeStruct((1, d, dff), BF16),
                      jax.ShapeDtypeStruct((1, dff, d), BF16)]
        args += list(nxt)
    outs = pl.pallas_call(
        functools.partial(_ffn_kernel, n_cast=0 if nxt is None else 3),
        out_shape=out_shape,
        grid=(ni, nf),
        in_specs=in_specs,
        out_specs=out_specs,
        scratch_shapes=[pltpu.VMEM((tm, d), BF16)],
        compiler_params=pltpu.CompilerParams(dimension_semantics=("parallel", "arbitrary"),
                                             vmem_limit_bytes=FFN_VMEM_LIMIT),
        name="ffn",
    )(*args)
    return outs[0], tuple(outs[1:])


def _cast_kernel(x_ref, o_ref):
    o_ref[...] = x_ref[...].astype(o_ref.dtype)


def _cast_layer(w, layer):
    _, r, c = w.shape
    tr = min(256, r)
    return pl.pallas_call(
        _cast_kernel,
        out_shape=jax.ShapeDtypeStruct((1, r, c), BF16),
        grid=(r // tr,),
        in_specs=[pl.BlockSpec((None, tr, c), lambda i: (layer, i, 0))],
        out_specs=pl.BlockSpec((None, tr, c), lambda i: (0, i, 0)),
        compiler_params=_cparams("parallel"),
        name="cast_layer",
    )(w)


def _nmm_kernel(x_ref, g_ref, w_ref, o_ref, n_scr):
    @pl.when(pl.program_id(1) == 0)
    def _():
        n_scr[...] = _rms(x_ref[...], g_ref[...]).astype(BF16)

    o_ref[...] = _dot(n_scr[...], w_ref[...]).astype(o_ref.dtype)


def _norm_matmul(x, g, w, tm, tn):
    n, d = x.shape
    nout = w.shape[1]
    tm = min(tm, n)
    return pl.pallas_call(
        _nmm_kernel,
        out_shape=jax.ShapeDtypeStruct((n, nout), F32),
        grid=(n // tm, nout // tn),
        in_specs=[
            pl.BlockSpec((tm, d), lambda i, j: (i, 0)),
            pl.BlockSpec((1, d), lambda i, j: (0, 0)),
            pl.BlockSpec((d, tn), lambda i, j: (0, j)),
        ],
        out_specs=pl.BlockSpec((tm, tn), lambda i, j: (i, j)),
        scratch_shapes=[pltpu.VMEM((tm, d), BF16)],
        compiler_params=_cparams("parallel", "arbitrary"),
        name="norm_matmul",
    )(x, g.reshape(1, d), w)


def _out2_kernel(res_ref, a1_ref, a2_ref, w1_ref, w2_ref, o_ref):
    o_ref[...] = res_ref[...] + _dot(a1_ref[...], w1_ref[...]) + _dot(a2_ref[...], w2_ref[...])


def _out2(res, a1, a2, w):
    n, d = res.shape
    k1 = a1.shape[1]
    tm = min(ROW_TILE, n)
    return pl.pallas_call(
        _out2_kernel,
        out_shape=jax.ShapeDtypeStruct((n, d), F32),
        grid=(n // tm,),
        in_specs=[
            pl.BlockSpec((tm, d), lambda i: (i, 0)),
            pl.BlockSpec((tm, k1), lambda i: (i, 0)),
            pl.BlockSpec((tm, k1), lambda i: (i, 0)),
            pl.BlockSpec((k1, d), lambda i: (0, 0)),
            pl.BlockSpec((k1, d), lambda i: (1, 0)),
        ],
        out_specs=pl.BlockSpec((tm, d), lambda i: (i, 0)),
        compiler_params=_cparams("parallel"),
        name="mixer_out",
    )(res, a1, a2, w, w)


def _glu_kernel(res_ref, a_ref, wa_ref, wb_ref, o_ref):
    a = a_ref[...]
    za = _dot(a, wa_ref[...])
    zb = _dot(a, wb_ref[...])
    o_ref[...] = res_ref[...] + za * _sigmoid(zb)


def _glu_out(res, a, w):
    n, d = res.shape
    k = a.shape[1]
    tm = min(ROW_TILE, n)
    return pl.pallas_call(
        _glu_kernel,
        out_shape=jax.ShapeDtypeStruct((n, d), F32),
        grid=(n // tm,),
        in_specs=[
            pl.BlockSpec((tm, d), lambda i: (i, 0)),
            pl.BlockSpec((tm, k), lambda i: (i, 0)),
            pl.BlockSpec((k, d), lambda i: (0, 0)),
            pl.BlockSpec((k, d), lambda i: (0, 1)),
        ],
        out_specs=pl.BlockSpec((tm, d), lambda i: (i, 0)),
        compiler_params=_cparams("parallel"),
        name="glu_out",
    )(res, a, w, w)


def _ple_kernel(h_ref, g_ref, p_ref, wg_ref, wp_ref, o_ref):
    h = h_ref[...]
    gate = _sigmoid(_dot(_rms(h, g_ref[...]).astype(BF16), wg_ref[...]))
    proj = _dot(p_ref[...].astype(BF16), wp_ref[...])
    o_ref[...] = h + gate * proj


def _ple(h, g, p, wg, wp, layer):
    n, d = h.shape
    dp = p.shape[2]
    tm = min(ROW_TILE, n)
    return pl.pallas_call(
        _ple_kernel,
        out_shape=jax.ShapeDtypeStruct((n, d), F32),
        grid=(n // tm,),
        in_specs=[
            pl.BlockSpec((tm, d), lambda i: (i, 0)),
            pl.BlockSpec((1, d), lambda i: (0, 0)),
            pl.BlockSpec((None, tm, dp), lambda i: (layer, i, 0)),
            pl.BlockSpec((None, d, d), lambda i: (layer, 0, 0)),
            pl.BlockSpec((None, dp, d), lambda i: (layer, 0, 0)),
        ],
        out_specs=pl.BlockSpec((tm, d), lambda i: (i, 0)),
        compiler_params=_cparams("parallel"),
        name="ple",
    )(h, g.reshape(1, d), p, wg, wp)


def _bias_kernel(tbl_ref, o_ref):
    tq = o_ref.shape[1]
    tk = o_ref.shape[2]
    n_rel = tbl_ref.shape[2]
    width = pl.next_power_of_2(tq + tk)
    n_idx = lax.broadcasted_iota(jnp.int32, (n_rel, width), 1)
    c_idx = lax.broadcasted_iota(jnp.int32, (n_rel, width), 0)
    m = jnp.where(n_idx < tk, n_idx, n_idx - width)
    idx = jnp.clip(ATT_LEFT - m, -(CHUNK - 1), REL_CLIP) + (CHUNK - 1)
    onehot = (c_idx == idx).astype(F32)
    tbl = jnp.broadcast_to(tbl_ref[0], (8, n_rel))
    ext = jnp.dot(tbl, onehot, preferred_element_type=F32, precision=lax.Precision.HIGHEST)[0:1]
    x = jnp.broadcast_to(ext, (tq, width))
    rolled = pltpu.roll(x, 0, 1, stride=1, stride_axis=0)[:, :tk]
    qc = lax.broadcasted_iota(jnp.int32, (tq, tk), 0) // CHUNK
    kc = lax.broadcasted_iota(jnp.int32, (tq, tk), 1) // CHUNK
    ok = (kc >= qc) & (kc <= qc + N_LEFT_CHUNKS)
    o_ref[0] = jnp.where(ok, rolled * LOG2E, NEG)


def _bias_tiles(rel_bias):
    nh, n_rel = rel_bias.shape
    tk = ATT_SUB + ATT_LEFT
    return pl.pallas_call(
        _bias_kernel,
        out_shape=jax.ShapeDtypeStruct((nh, ATT_SUB, tk), F32),
        grid=(nh,),
        in_specs=[pl.BlockSpec((1, 1, n_rel), lambda h: (h, 0, 0))],
        out_specs=pl.BlockSpec((1, ATT_SUB, tk), lambda h: (h, 0, 0)),
        compiler_params=_cparams("parallel"),
        name="rel_bias_tiles",
    )(rel_bias.reshape(nh, 1, n_rel))


def _attn_kernel(q_ref, kp_ref, kc_ref, vp_ref, vc_ref, bias_ref, qg_ref, kg_ref, o_ref):
    has_prev = pl.program_id(2) > 0
    tq = q_ref.shape[0]
    w = 2 * HEAD_DIM
    tk = bias_ref.shape[2]
    nh = w // HEAD_DIM
    col = lax.broadcasted_iota(jnp.int32, (nh * ATT_SUB, tk), 1)
    head_of_lane = lax.broadcasted_iota(jnp.int32, (1, w), 1) // HEAD_DIM
    seg = (lax.broadcasted_iota(jnp.int32, (w, w), 0) // HEAD_DIM
           == lax.broadcasted_iota(jnp.int32, (w, w), 1) // HEAD_DIM)
    seg = jnp.where(seg, 1.0, 0.0).astype(BF16)

    def rms2(x, gain):
        sq = x * x
        hi = sq.astype(BF16)
        lo = (sq - hi.astype(F32)).astype(BF16)
        ss = _dot(jnp.concatenate([hi, lo], axis=1), jnp.concatenate([seg, seg], axis=0))
        return x * lax.rsqrt(ss * (1.0 / HEAD_DIM) + RMS_EPS) * gain

    def one_pair(pi):
        ln = slice(pi * w, (pi + 1) * w)
        q = rms2(q_ref[:, ln], qg_ref[...] * (HEAD_DIM ** -0.5 * LOG2E))
        k = jnp.concatenate([rms2(kp_ref[:, ln], kg_ref[...]), rms2(kc_ref[:, ln], kg_ref[...])], axis=0).astype(BF16)
        v = jnp.concatenate([vp_ref[:, ln], vc_ref[:, ln]], axis=0).astype(BF16)
        qh = [jnp.where(head_of_lane == hh, q, 0.0).astype(BF16) for hh in range(nh)]
        bias = jnp.concatenate([bias_ref[pi * nh + hh] for hh in range(nh)], axis=0)
        starts = [i * ATT_SUB for i in range(tq // ATT_SUB)]
        ob = []
        for g0 in range(0, len(starts), ATT_GROUP):
            jobs = starts[g0:g0 + ATT_GROUP]
            sc = [_dot_nt(jnp.concatenate([x[r0:r0 + ATT_SUB] for x in qh], axis=0), k[r0:r0 + tk]) + bias
                  for r0 in jobs]
            sc = [jnp.where(has_prev | (col >= ATT_LEFT - r0), s, NEG) if r0 < ATT_LEFT else s
                  for s, r0 in zip(sc, jobs)]
            pr = [jnp.exp2(s - jnp.max(s, axis=-1, keepdims=True)) for s in sc]
            den = [jnp.sum(p, axis=-1, keepdims=True) for p in pr]
            ob += [_dot(p.astype(BF16), v[r0:r0 + tk]) / d for p, d, r0 in zip(pr, den, jobs)]
        outs = []
        for o2 in ob:
            o = o2[:ATT_SUB]
            for hh in range(1, nh):
                o = jnp.where(head_of_lane == hh, o2[hh * ATT_SUB:(hh + 1) * ATT_SUB], o)
            outs.append(o)
        o_ref[:, ln] = jnp.concatenate(outs, axis=0).astype(o_ref.dtype)

    for pi in range(q_ref.shape[1] // w):
        one_pair(pi)


def _attention(proj, bias, q_gain, k_gain, bsz, seq, d_att):
    n = bsz * seq
    tq = ATT_TQ
    nqb = seq // tq
    w = 2 * HEAD_DIM
    npair = d_att // w
    pps = ATT_PAIRS if npair % ATT_PAIRS == 0 else 1
    ngrp = npair // pps
    wb = pps * w

    def cur(col0):
        return pl.BlockSpec((tq, wb), lambda hp, b, qb: (b * nqb + qb, col0 + hp))

    def prev(col0):
        return pl.BlockSpec((tq, wb), lambda hp, b, qb: (b * nqb + jnp.maximum(qb - 1, 0), col0 + hp))

    return pl.pallas_call(
        _attn_kernel,
        out_shape=jax.ShapeDtypeStruct((n, d_att), BF16),
        grid=(ngrp, bsz, nqb),
        in_specs=[
            cur(0), prev(ngrp), cur(ngrp), prev(2 * ngrp), cur(2 * ngrp),
            pl.BlockSpec((2 * pps, ATT_SUB, ATT_SUB + ATT_LEFT), lambda hp, b, qb: (hp, 0, 0)),
            pl.BlockSpec((1, w), lambda hp, b, qb: (0, 0)),
            pl.BlockSpec((1, w), lambda hp, b, qb: (0, 0)),
        ],
        out_specs=pl.BlockSpec((tq, wb), lambda hp, b, qb: (b * nqb + qb, hp)),
        compiler_params=_cparams("parallel", "parallel", "arbitrary"),
        name="band_attention",
    )(proj, proj, proj, proj, proj, bias, jnp.tile(q_gain, 2).reshape(1, w), jnp.tile(k_gain, 2).reshape(1, w))


def _split3(x):
    hi = x.astype(BF16)
    r1 = x - hi.astype(F32)
    mid = r1.astype(BF16)
    lo = (r1 - mid.astype(F32)).astype(BF16)
    return hi, mid, lo


def _token_shift(z, carry_ref, mu):
    rows = lax.broadcasted_iota(jnp.int32, z.shape, 0)
    prev = jnp.where(rows == 0, carry_ref[...], pltpu.roll(z, 1, 0))
    carry_ref[...] = z[z.shape[0] - 1:, :]
    return z + (prev - z) * mu


def _rwkv_kernel(r_ref, k_ref, v_ref, l_ref, mur_ref, muk_ref, muv_ref, mul_ref,
                 w0_ref, a0_ref, kk_ref, ka_ref, rk_ref, lnw_ref, lnb_ref,
                 wup_ref, aup_ref, gup_ref, o_ref,
                 s_scr, cr_scr, ck_scr, cv_scr, cl_scr):
    first = pl.program_id(1) == 0
    nb, L, d = r_ref.shape

    @pl.when(first)
    def _():
        s_scr[...] = jnp.zeros_like(s_scr)
        for c in (cr_scr, ck_scr, cv_scr, cl_scr):
            c[...] = jnp.zeros_like(c)

    def shifted(ref, carry, mu_ref):
        return jnp.concatenate([_token_shift(ref[b], carry.at[b], mu_ref[...]) for b in range(nb)], axis=0)

    r = shifted(r_ref, cr_scr, mur_ref)
    k = shifted(k_ref, ck_scr, muk_ref)
    v = shifted(v_ref, cv_scr, muv_ref)
    lo = shifted(l_ref, cl_scr, mul_ref)
    nw = wup_ref.shape[0]
    na = aup_ref.shape[0]
    xw, xa, xg = lo[:, :nw], lo[:, nw:nw + na], lo[:, nw + na:]

    wpre = w0_ref[...] + _bdot(jnp.tanh(xw), wup_ref[...])
    w_log = -(jnp.maximum(-wpre, 0.0) + jnp.log(1.0 + jnp.exp(-jnp.abs(wpre)))) - 0.5
    lw = -jnp.exp(w_log)
    a = _sigmoid(a0_ref[...] + _bdot(xa, aup_ref[...]))
    g = _bdot(_sigmoid(xg), gup_ref[...])

    kkf = k * kk_ref[...]
    k2 = k * (1.0 + (a - 1.0) * ka_ref[...])

    ti = lax.broadcasted_iota(jnp.int32, (nb * L, nb * L), 0)
    si = lax.broadcasted_iota(jnp.int32, (nb * L, nb * L), 1)
    tri = jnp.where((si <= ti) & (si // L == ti // L), 1.0, 0.0).astype(BF16)
    h3 = _split3(lw)
    cum = _dot(jnp.concatenate([tri, tri, tri], axis=1), jnp.concatenate(h3, axis=0))
    e_pos = jnp.exp(cum)
    e_neg = jnp.exp(-cum)
    e_prev = jnp.exp(cum - lw)

    rk = r * k2 * rk_ref[...]
    w2 = 2 * HEAD_DIM
    chains = [(b, j) for b in range(nb) for j in range(d // w2)]
    at = lambda x, c: x[c[0] * L:(c[0] + 1) * L, c[1] * w2:(c[1] + 1) * w2]
    lane = lax.broadcasted_iota(jnp.int32, (1, w2), 1)
    in_a = lane < HEAD_DIM
    row_l = lax.broadcasted_iota(jnp.int32, (L, w2), 0)
    idx_l = lax.broadcasted_iota(jnp.int32, (L, w2), 1) % HEAD_DIM
    strict = idx_l < row_l
    incl = idx_l <= row_l
    same_head = (lax.broadcasted_iota(jnp.int32, (w2, w2), 0) // HEAD_DIM
                 == lax.broadcasted_iota(jnp.int32, (w2, w2), 1) // HEAD_DIM)
    seg = jnp.where(same_head, 1.0, 0.0).astype(BF16)

    def segsum(xs):
        x = jnp.concatenate(xs, axis=0)
        hi = x.astype(BF16)
        lo = (x - hi.astype(F32)).astype(BF16)
        tot = _dot(jnp.concatenate([hi, lo], axis=1), jnp.concatenate([seg, seg], axis=0))
        return [tot[i * L:(i + 1) * L] for i in range(len(xs))]

    def bdiag(x):
        zero = jnp.zeros_like(x)
        return jnp.concatenate([jnp.where(in_a, x, zero), jnp.where(in_a, zero, x)], axis=0)

    def lower(x, mask):
        xb = x.astype(BF16)
        return jnp.where(mask, xb, jnp.zeros_like(xb))

    kss = segsum([at(kkf, c) * at(kkf, c) for c in chains])
    steps = max(1, (L - 1).bit_length())
    ar, bkc, rb, vb, vbd = [], [], [], [], []
    for c, ks in zip(chains, kss):
        kk = at(kkf, c) / jnp.maximum(jnp.sqrt(ks), 1e-12)
        ah = -kk * at(e_prev, c)
        bh = (kk * at(a, c) * at(e_neg, c)).astype(BF16)
        kh = (at(k2, c) * at(e_neg, c)).astype(BF16)
        rh = at(r, c) * at(e_pos, c)
        ar.append(jnp.concatenate([ah, rh], axis=0).astype(BF16))
        bkc.append(jnp.concatenate([bh, kh], axis=0))
        rb.append(jnp.concatenate([bdiag(bh), bdiag(kh)], axis=0))
        vb.append(at(v, c).astype(BF16))
        vbd.append(bdiag(vb[-1]))
    ids = range(len(chains))
    s0 = [s_scr[c] for c in chains]
    gm = [_dot_nt(ar[i], rb[i]) for i in ids]
    ps = [_dot_nt(ar[i], s0[i].astype(BF16)) for i in ids]
    pw = [lower(gm[i][:L, :w2], strict) for i in ids]
    u = [ps[i][:L] + _dot(lower(gm[i][:L, w2:], strict), vbd[i]) for i in ids]
    for st in range(steps):
        u = [u[i] + _dot(pw[i], bdiag(u[i].astype(BF16))) for i in ids]
        if st + 1 < steps:
            pw = [_dot(pw[i], bdiag(pw[i])).astype(BF16) for i in ids]
    y = []
    for i, c in zip(ids, chains):
        ub = u[i].astype(BF16)
        low = jnp.concatenate([lower(gm[i][L:, :w2], incl), lower(gm[i][L:, w2:], incl)], axis=1)
        uvd = jnp.concatenate([bdiag(ub), vbd[i]], axis=0)
        y.append(ps[i][L:] + _dot(low, uvd))
        uv = jnp.concatenate([ub, vb[i]], axis=0)
        w_last = at(e_pos, c)[L - 1:, :]
        s_scr[c] = jnp.where(same_head, s0[i] + _dot_tn(uv, bkc[i]), 0.0) * w_last
    yc = [yi - m * (1.0 / HEAD_DIM) for yi, m in zip(y, segsum(y))]
    var = segsum([c * c for c in yc])
    yn = [c * lax.rsqrt(vr * (1.0 / HEAD_DIM) + GN_EPS) for c, vr in zip(yc, var)]
    bs = segsum([at(rk, c) for c in chains])
    npair = d // w2
    grid2 = lambda xs: jnp.concatenate([jnp.concatenate(xs[b * npair:(b + 1) * npair], axis=1)
                                        for b in range(nb)], axis=0)
    out = (grid2(yn) * lnw_ref[...] + lnb_ref[...] + grid2(bs) * v) * g
    o_ref[...] = out.reshape(nb, L, d).astype(o_ref.dtype)


def _rwkv(proj, col0, bsz, seq, d, mu, w0, w_up, a0, a_up, g_up, k_k, k_a, r_k, lnx_w, lnx_b):
    L = RWKV_L
    nb = RWKV_NB if bsz % RWKV_NB == 0 else 1
    nt = seq // L
    nl = mu.shape[0] - 3 * d
    row = lambda x: x.reshape(1, -1)
    cb = col0 // d
    lb = (col0 + 3 * d) // nl
    proj3 = proj.reshape(bsz, seq, -1)

    def zspec(width, blk):
        return pl.BlockSpec((nb, L, width), lambda b, t: (b, t, blk))

    def pspec(shape):
        return pl.BlockSpec(shape, lambda b, t: (0,) * len(shape))

    npair = d // (2 * HEAD_DIM)
    out = pl.pallas_call(
        _rwkv_kernel,
        out_shape=jax.ShapeDtypeStruct((bsz, seq, d), BF16),
        grid=(bsz // nb, nt),
        in_specs=[
            zspec(d, cb), zspec(d, cb + 1), zspec(d, cb + 2), zspec(nl, lb),
            pspec((1, d)), pspec((1, d)), pspec((1, d)), pspec((1, nl)),
            pspec((1, d)), pspec((1, d)), pspec((1, d)), pspec((1, d)), pspec((1, d)),
            pspec((1, d)), pspec((1, d)),
            pspec(w_up.shape), pspec(a_up.shape), pspec(g_up.shape),
        ],
        out_specs=pl.BlockSpec((nb, L, d), lambda b, t: (b, t, 0)),
        scratch_shapes=[
            pltpu.VMEM((nb, npair, 2 * HEAD_DIM, 2 * HEAD_DIM), F32),
            pltpu.VMEM((nb, 1, d), F32), pltpu.VMEM((nb, 1, d), F32), pltpu.VMEM((nb, 1, d), F32),
            pltpu.VMEM((nb, 1, nl), F32),
        ],
        compiler_params=_cparams("parallel", "arbitrary"),
        name="rwkv7",
    )(proj3, proj3, proj3, proj3,
      row(mu[:d]), row(mu[d:2 * d]), row(mu[2 * d:3 * d]), row(mu[3 * d:]),
      row(w0), row(a0), row(k_k), row(k_a), row(r_k), row(lnx_w), row(lnx_b),
      w_up.astype(BF16), a_up.astype(BF16), g_up.astype(BF16))
    return out.reshape(bsz * seq, d)


SSM_GB = 8
SSM_NB = 2


def _cmul(ar, ai, br, bi):
    return ar * br - ai * bi, ar * bi + ai * br


def _s5_abar(lr, li, ldt):
    dt = jnp.exp(ldt)
    mag = jnp.exp(lr * dt)
    return mag * jnp.cos(li * dt), mag * jnp.sin(li * dt)


def _s5_prep_kernel(lrr_ref, lir_ref, ldtr_ref, lrc_ref, lic_ref, ldtc_ref,
                    bre_ref, bim_ref, cre_ref, cim_ref, k_ref, p_ref, q_ref):
    L = SSM_L
    w = k_ref.shape[2]
    lr, li = lrr_ref[0], lir_ref[0]
    a_re, a_im = _s5_abar(lr, li, ldtr_ref[0])
    den = lr * lr + li * li
    z_re = ((a_re - 1.0) * lr + a_im * li) / den
    z_im = (a_im * lr - (a_re - 1.0) * li) / den
    bb_re, bb_im = _cmul(z_re, z_im, bre_ref[0], bim_ref[0])
    cre, cim = cre_ref[0], cim_ref[0]
    ccat = jnp.concatenate([cre, -cim], axis=0)
    ac_re, ac_im = _s5_abar(lrc_ref[0], lic_ref[0], ldtc_ref[0])
    pr, pi = jnp.ones_like(a_re), jnp.zeros_like(a_im)
    qr, qi = ac_re, ac_im
    ptaus = []
    for tau in range(L):
        rb_re, rb_im = _cmul(pr, pi, bb_re, bb_im)
        ptau = jnp.concatenate([rb_re, rb_im], axis=1)
        ptaus.append(ptau)
        p_ref[0, (L - 1 - tau) * w:(L - tau) * w, :] = ptau.astype(p_ref.dtype)
        q_ref[0, :, tau * w:(tau + 1) * w] = jnp.concatenate(
            [cre * qr - cim * qi, -(cre * qi + cim * qr)], axis=0).astype(q_ref.dtype)
        pr, pi = _cmul(pr, pi, a_re, a_im)
        qr, qi = _cmul(qr, qi, ac_re, ac_im)
    pall = jnp.concatenate(ptaus[::-1], axis=0)
    p_hi = pall.astype(BF16)
    p_lo = (pall - p_hi.astype(F32)).astype(BF16)
    c_hi = ccat.astype(BF16)
    c_lo = (ccat - c_hi.astype(F32)).astype(BF16)
    k_ref[0, :L * w, :] = (_dot(p_hi, c_hi) + _dot(p_lo, c_hi) + _dot(p_hi, c_lo)).astype(k_ref.dtype)
    k_ref[0, L * w:, :] = jnp.zeros((w, w), k_ref.dtype)


def _s5_prep(lam_re, lam_im, log_dt, b_re, b_im, c_re, c_im):
    G, P = lam_re.shape
    gs = SSM_GROUP
    gb = SSM_GB
    J = G // gb
    L = SSM_L
    eye = jnp.eye(gb, dtype=F32)
    ldt = jnp.repeat(log_dt, P)

    def bdiag_b(b):
        bt = jnp.swapaxes(b, 1, 2).reshape(J, gb, gs, P)
        return (bt[:, :, :, None, :] * eye[None, :, None, :, None]).reshape(J, gb * gs, gb * P)

    def bdiag_c(c):
        ct = jnp.swapaxes(c, 1, 2).reshape(J, gb, P, gs)
        return (ct[:, :, :, None, :] * eye[None, :, None, :, None]).reshape(J, gb * P, gb * gs)

    row = lambda x: x.reshape(J, 1, gb * P)
    col = lambda x: x.reshape(J, gb * P, 1)

    def spec(*shape):
        return pl.BlockSpec((1,) + shape, lambda j: (j,) + (0,) * len(shape))

    w, sw = gb * gs, gb * P
    return pl.pallas_call(
        _s5_prep_kernel,
        out_shape=[jax.ShapeDtypeStruct((J, (L + 1) * w, w), BF16),
                   jax.ShapeDtypeStruct((J, L * w, 2 * sw), BF16),
                   jax.ShapeDtypeStruct((J, 2 * sw, L * w), BF16)],
        grid=(J,),
        in_specs=[spec(1, sw), spec(1, sw), spec(1, sw), spec(sw, 1), spec(sw, 1), spec(sw, 1),
                  spec(w, sw), spec(w, sw), spec(sw, w), spec(sw, w)],
        out_specs=[spec((L + 1) * w, w), spec(L * w, 2 * sw), spec(2 * sw, L * w)],
        compiler_params=_cparams("parallel"),
        name="s5_prep",
    )(row(lam_re), row(lam_im), row(ldt), col(lam_re), col(lam_im), col(ldt),
      bdiag_b(b_re), bdiag_b(b_im), bdiag_c(c_re), bdiag_c(c_im))


def _s5_kernel(u_ref, k_ref, p_ref, q_ref, lr_ref, li_ref, ldt_ref, d_ref, o_ref,
               gre_scr, gim_scr, hre_scr, him_scr, y_scr, *, nb):
    L = SSM_L
    nc = u_ref.shape[0] // L
    ncs = nc // nb
    sw = lr_ref.shape[2]
    w = u_ref.shape[1]
    us = [u_ref[pl.ds(s, nc, stride=L), :].astype(BF16) for s in range(L)]
    ucat = jnp.concatenate(us, axis=1)
    gall = _dot(ucat, p_ref[0])
    gre_scr[...] = gall[:, :sw]
    gim_scr[...] = gall[:, sw:]
    a_re, a_im = _s5_abar(lr_ref[0], li_ref[0], ldt_ref[0])
    al_re, al_im = a_re, a_im
    for _ in range(L.bit_length() - 1):
        al_re, al_im = _cmul(al_re, al_im, al_re, al_im)

    def body(c, carry):
        new = []
        for b, (hre, him) in enumerate(carry):
            row = b * ncs + c
            hre_scr[pl.ds(row, 1), :] = hre
            him_scr[pl.ds(row, 1), :] = him
            gr = gre_scr[pl.ds(row, 1), :]
            gi = gim_scr[pl.ds(row, 1), :]
            new.append((al_re * hre - al_im * him + gr, al_re * him + al_im * hre + gi))
        return tuple(new)

    zero = jnp.zeros((1, sw), F32)
    lax.fori_loop(0, ncs, body, tuple((zero, zero) for _ in range(nb)))
    hp = jnp.concatenate([hre_scr[...], him_scr[...]], axis=1).astype(BF16)
    ycarry = _dot(hp, q_ref[0])
    kall = k_ref[0]
    for t in range(0, L, 2):
        wt = jnp.concatenate([kall[(L - 1 - t) * w:, :], kall[(L - 2 - t) * w:L * w, :]], axis=1)
        yt = ycarry[:, t * w:(t + 2) * w] + _dot(ucat[:, :(t + 2) * w], wt)
        y_scr[pl.ds(t, nc, stride=L), :] = yt[:, :w]
        y_scr[pl.ds(t + 1, nc, stride=L), :] = yt[:, w:]
    y = y_scr[...] + d_ref[0] * u_ref[...]
    o_ref[...] = (0.5 * y * (1.0 + jnp.tanh(math.sqrt(2.0 / math.pi) * (y + 0.044715 * y * y * y)))).astype(o_ref.dtype)


def _s5(u, bsz, seq, mats, lam_re, lam_im, log_dt, d_skip):
    kb, pb, qb = mats
    G, P = lam_re.shape
    gb = SSM_GB
    J = G // gb
    L = SSM_L
    w = gb * SSM_GROUP
    sw = gb * P
    nb = SSM_NB if bsz % SSM_NB == 0 else 1
    rows = nb * seq
    nc = rows // L
    row = lambda x: x.reshape(J, 1, sw)

    def jspec(*shape):
        return pl.BlockSpec((1,) + shape, lambda j, b: (j,) + (0,) * len(shape))

    return pl.pallas_call(
        functools.partial(_s5_kernel, nb=nb),
        out_shape=jax.ShapeDtypeStruct((bsz * seq, G * SSM_GROUP), BF16),
        grid=(J, bsz // nb),
        in_specs=[
            pl.BlockSpec((rows, w), lambda j, b: (b, j)),
            jspec((L + 1) * w, w), jspec(L * w, 2 * sw), jspec(2 * sw, L * w),
            jspec(1, sw), jspec(1, sw), jspec(1, sw), jspec(1, w),
        ],
        out_specs=pl.BlockSpec((rows, w), lambda j, b: (b, j)),
        scratch_shapes=[pltpu.VMEM((nc, sw), F32)] * 4 + [pltpu.VMEM((rows, w), F32)],
        compiler_params=_cparams("parallel", "arbitrary"),
        name="s5_ssm",
    )(u, kb, pb, qb, row(lam_re), row(lam_im), row(jnp.repeat(log_dt, P)), d_skip.reshape(J, 1, w))


def kernel(x, p, ffn1_norm, ffn1_w_gate, ffn1_w_up, ffn1_w_down, mix_norm, ffn2_norm, ffn2_w_gate, ffn2_w_up, ffn2_w_down, ple_norm, ple_w_gate, ple_w_proj, ab_w_in, att_q_gain, att_k_gain, att_rel_bias, rwkv_mu, rwkv_w0, rwkv_w_up, rwkv_a0, rwkv_a_up, rwkv_g_up, rwkv_k_k, rwkv_k_a, rwkv_r_k, rwkv_lnx_w, rwkv_lnx_b, ab_w_out, ssm_w_in, ssm_lambda_re, ssm_lambda_im, ssm_log_dt, ssm_b_re, ssm_b_im, ssm_c_re, ssm_c_im, ssm_d, ssm_w_out):
    bsz, seq, d = x.shape
    depth = p.shape[0]
    n = bsz * seq
    bf = lambda w: w.astype(BF16)
    h = x.reshape(n, d)
    f1 = (ffn1_w_gate, ffn1_w_up, ffn1_w_down)
    f2 = (ffn2_w_gate, ffn2_w_up, ffn2_w_down)
    wcur = tuple(_cast_layer(w, 0) for w in f1)
    wpl = [bf(w) for w in (ple_w_gate, ple_w_proj)]
    pe = p.reshape(depth, n, -1)
    for i in range(depth):
        j = i // 2
        h, wcur = _ffn(h, ffn1_norm[i], wcur, 0, f2, i)
        if i % 2 == 0:
            d_att = att_rel_bias.shape[1] * HEAD_DIM
            d_rw = rwkv_w0.shape[1]
            n_in = ab_w_in.shape[2]
            tn = 1280 if n_in % 1280 == 0 else 128
            proj = _norm_matmul(h, mix_norm[i], bf(ab_w_in[j]), 2 * ROW_TILE, tn)
            bias = _bias_tiles(att_rel_bias[j])
            att = _attention(proj, bias, att_q_gain[j], att_k_gain[j], bsz, seq, d_att)
            rw = _rwkv(proj, 3 * d_att, bsz, seq, d_rw, rwkv_mu[j], rwkv_w0[j], rwkv_w_up[j],
                       rwkv_a0[j], rwkv_a_up[j], rwkv_g_up[j], rwkv_k_k[j], rwkv_k_a[j],
                       rwkv_r_k[j].reshape(-1), rwkv_lnx_w[j], rwkv_lnx_b[j])
            h = _out2(h, att, rw, bf(ab_w_out[j]))
        else:
            d_ssm = ssm_w_in.shape[2]
            u = _norm_matmul(h, mix_norm[i], bf(ssm_w_in[j]), ROW_TILE, d_ssm)
            mats = _s5_prep(ssm_lambda_re[j], ssm_lambda_im[j], ssm_log_dt[j], ssm_b_re[j],
                            ssm_b_im[j], ssm_c_re[j], ssm_c_im[j])
            y = _s5(u, bsz, seq, mats, ssm_lambda_re[j], ssm_lambda_im[j], ssm_log_dt[j], ssm_d[j])
            h = _glu_out(h, y, bf(ssm_w_out[j]))
        last = i + 1 == depth
        h, wcur = _ffn(h, ffn2_norm[i], wcur, 0, None if last else f1, i + 1)
        h = _ple(h, ple_norm[i], pe, wpl[0], wpl[1], i)
    return h.reshape(bsz, seq, d)
```

```python
import functools
import math

import jax
import jax.numpy as jnp
from jax import lax
from jax.experimental import pallas as pl
from jax.experimental.pallas import tpu as pltpu

F32 = jnp.float32
BF16 = jnp.bfloat16

RMS_EPS = 1e-6
GN_EPS = 64e-5
CHUNK = 64
N_LEFT_CHUNKS = 8
REL_CLIP = 128
HEAD_DIM = 64
ATT_LEFT = N_LEFT_CHUNKS * CHUNK
ATT_TQ = ATT_LEFT
ATT_SUB = 128
ATT_GROUP = 2
ATT_PAIRS = 8
RWKV_L = 64
RWKV_NB = 4
SSM_GROUP = 16
SSM_L = 16
NEG = -1e30
LOG2E = math.log2(math.e)

ROW_TILE = 512
FFN_ROW_TILE = 1024
FFN_VMEM_LIMIT = 58 * 2 ** 20
COL_TILE = 512
VMEM_LIMIT = 52 * 2 ** 20


def _cparams(*sem):
    return pltpu.CompilerParams(dimension_semantics=sem, vmem_limit_bytes=VMEM_LIMIT)


def _dot(a, b):
    return jnp.dot(a, b, preferred_element_type=F32)


def _dot_nt(a, b):
    return lax.dot_general(a, b, (((1,), (1,)), ((), ())), preferred_element_type=F32)


def _dot_tn(a, b):
    return lax.dot_general(a, b, (((0,), (0,)), ((), ())), preferred_element_type=F32)


def _bdot(a, b):
    return _dot(a.astype(BF16), b.astype(BF16))


def _rms(x, g):
    return x * lax.rsqrt(jnp.mean(x * x, axis=-1, keepdims=True) + RMS_EPS) * g


def _sigmoid(x):
    return 0.5 * jnp.tanh(0.5 * x) + 0.5


def _ffn_kernel(*refs, n_cast):
    h_hbm, g_ref, wg_ref, wu_ref, wd_ref = refs[:5]
    cast_in = refs[5:5 + n_cast]
    o_ref = refs[5 + n_cast]
    cast_out = refs[6 + n_cast:6 + 2 * n_cast]
    n_scr, h_buf, sem = refs[6 + 2 * n_cast:9 + 2 * n_cast]
    i = pl.program_id(0)
    tm = h_buf.shape[1]

    def h_copy(tile, slot):
        return pltpu.make_async_copy(h_hbm.at[pl.ds(tile * tm, tm), :], h_buf.at[slot], sem.at[slot])

    @pl.when(pl.program_id(1) == 0)
    def _():
        slot = i % 2

        @pl.when(i == 0)
        def _():
            h_copy(0, 0).start()

        h_copy(i, slot).wait()

        @pl.when(i + 1 < pl.num_programs(0))
        def _():
            h_copy(i + 1, 1 - slot).start()

        h = h_buf[slot]
        n_scr[...] = _rms(h, g_ref[...]).astype(BF16)
        o_ref[...] = h

    n = n_scr[...]
    gate = _dot(n, wg_ref[...])
    up = _dot(n, wu_ref[...])
    act = (0.5 * gate * _sigmoid(gate) * up).astype(BF16)
    o_ref[...] += _dot(act, wd_ref[...])
    for src, dst in zip(cast_in, cast_out):
        dst[...] = src[...].astype(BF16)


def _ffn(h, g, w, layer, nxt=None, nxt_layer=0):
    wg, wu, wd = w
    n, d = h.shape
    dff = wg.shape[2]
    tm = min(FFN_ROW_TILE, n)
    tf = min(COL_TILE, dff)
    ni, nf = n // tm, dff // tf
    in_specs = [
        pl.BlockSpec(memory_space=pl.ANY),
        pl.BlockSpec((1, d), lambda i, f: (0, 0)),
        pl.BlockSpec((None, d, tf), lambda i, f: (layer, 0, f)),
        pl.BlockSpec((None, d, tf), lambda i, f: (layer, 0, f)),
        pl.BlockSpec((None, tf, d), lambda i, f: (layer, f, 0)),
    ]
    out_shape = [jax.ShapeDtypeStruct((n, d), F32)]
    out_specs = [pl.BlockSpec((tm, d), lambda i, f: (i, 0))]
    args = [h, g.reshape(1, d), wg, wu, wd]
    if nxt is not None:
        rd, rf = d // ni, tf // ni
        up_blk, down_blk = (None, rd, tf), (None, rf, d)
        in_specs += [pl.BlockSpec(up_blk, lambda i, f: (nxt_layer, i, f)),
                     pl.BlockSpec(up_blk, lambda i, f: (nxt_layer, i, f)),
                     pl.BlockSpec(down_blk, lambda i, f: (nxt_layer, f * ni + i, 0))]
        out_specs += [pl.BlockSpec(up_blk, lambda i, f: (0, i, f)),
                      pl.BlockSpec(up_blk, lambda i, f: (0, i, f)),
                      pl.BlockSpec(down_blk, lambda i, f: (0, f * ni + i, 0))]
        out_shape += [jax.ShapeDtypeStruct((1, d, dff), BF16), jax.ShapeDtypeStruct((1, d, dff), BF16),
                      jax.ShapeDtypeStruct((1, dff, d), BF16)]
        args += list(nxt)
    outs = pl.pallas_call(
        functools.partial(_ffn_kernel, n_cast=0 if nxt is None else 3),
        out_shape=out_shape,
        grid=(ni, nf),
        in_specs=in_specs,
        out_specs=out_specs,
        scratch_shapes=[pltpu.VMEM((tm, d), BF16), pltpu.VMEM((2, tm, d), F32), pltpu.SemaphoreType.DMA((2,))],
        compiler_params=pltpu.CompilerParams(dimension_semantics=("arbitrary", "arbitrary"),
                                             vmem_limit_bytes=FFN_VMEM_LIMIT),
        name="ffn",
    )(*args)
    return outs[0], tuple(outs[1:])


def _cast_kernel(x_ref, o_ref):
    o_ref[...] = x_ref[...].astype(o_ref.dtype)


def _cast_layer(w, layer):
    _, r, c = w.shape
    tr = min(256, r)
    return pl.pallas_call(
        _cast_kernel,
        out_shape=jax.ShapeDtypeStruct((1, r, c), BF16),
        grid=(r // tr,),
        in_specs=[pl.BlockSpec((None, tr, c), lambda i: (layer, i, 0))],
        out_specs=pl.BlockSpec((None, tr, c), lambda i: (0, i, 0)),
        compiler_params=_cparams("parallel"),
        name="cast_layer",
    )(w)


def _nmm_kernel(x_ref, g_ref, w_ref, o_ref, n_scr):
    @pl.when(pl.program_id(1) == 0)
    def _():
        n_scr[...] = _rms(x_ref[...], g_ref[...]).astype(BF16)

    o_ref[...] = _dot(n_scr[...], w_ref[...]).astype(o_ref.dtype)


def _norm_matmul(x, g, w, tm, tn):
    n, d = x.shape
    nout = w.shape[1]
    tm = min(tm, n)
    return pl.pallas_call(
        _nmm_kernel,
        out_shape=jax.ShapeDtypeStruct((n, nout), F32),
        grid=(n // tm, nout // tn),
        in_specs=[
            pl.BlockSpec((tm, d), lambda i, j: (i, 0)),
            pl.BlockSpec((1, d), lambda i, j: (0, 0)),
            pl.BlockSpec((d, tn), lambda i, j: (0, j)),
        ],
        out_specs=pl.BlockSpec((tm, tn), lambda i, j: (i, j)),
        scratch_shapes=[pltpu.VMEM((tm, d), BF16)],
        compiler_params=_cparams("parallel", "arbitrary"),
        name="norm_matmul",
    )(x, g.reshape(1, d), w)


def _out2_kernel(res_ref, a1_ref, a2_ref, w1_ref, w2_ref, o_ref):
    o_ref[...] = res_ref[...] + _dot(a1_ref[...], w1_ref[...]) + _dot(a2_ref[...], w2_ref[...])


def _out2(res, a1, a2, w):
    n, d = res.shape
    k1 = a1.shape[1]
    tm = min(ROW_TILE, n)
    return pl.pallas_call(
        _out2_kernel,
        out_shape=jax.ShapeDtypeStruct((n, d), F32),
        grid=(n // tm,),
        in_specs=[
            pl.BlockSpec((tm, d), lambda i: (i, 0)),
            pl.BlockSpec((tm, k1), lambda i: (i, 0)),
            pl.BlockSpec((tm, k1), lambda i: (i, 0)),
            pl.BlockSpec((k1, d), lambda i: (0, 0)),
            pl.BlockSpec((k1, d), lambda i: (1, 0)),
        ],
        out_specs=pl.BlockSpec((tm, d), lambda i: (i, 0)),
        compiler_params=_cparams("parallel"),
        name="mixer_out",
    )(res, a1, a2, w, w)


def _glu_kernel(res_ref, a_ref, wa_ref, wb_ref, o_ref):
    a = a_ref[...]
    za = _dot(a, wa_ref[...])
    zb = _dot(a, wb_ref[...])
    o_ref[...] = res_ref[...] + za * _sigmoid(zb)


def _glu_out(res, a, w):
    n, d = res.shape
    k = a.shape[1]
    tm = min(ROW_TILE, n)
    return pl.pallas_call(
        _glu_kernel,
        out_shape=jax.ShapeDtypeStruct((n, d), F32),
        grid=(n // tm,),
        in_specs=[
            pl.BlockSpec((tm, d), lambda i: (i, 0)),
            pl.BlockSpec((tm, k), lambda i: (i, 0)),
            pl.BlockSpec((k, d), lambda i: (0, 0)),
            pl.BlockSpec((k, d), lambda i: (0, 1)),
        ],
        out_specs=pl.BlockSpec((tm, d), lambda i: (i, 0)),
        compiler_params=_cparams("parallel"),
        name="glu_out",
    )(res, a, w, w)


def _ple_kernel(h_ref, g_ref, p_ref, wg_ref, wp_ref, o_ref):
    h = h_ref[...]
    gate = _sigmoid(_dot(_rms(h, g_ref[...]).astype(BF16), wg_ref[...]))
    proj = _dot(p_ref[...].astype(BF16), wp_ref[...])
    o_ref[...] = h + gate * proj


def _ple(h, g, p, wg, wp, layer):
    n, d = h.shape
    dp = p.shape[2]
    tm = min(ROW_TILE, n)
    return pl.pallas_call(
        _ple_kernel,
        out_shape=jax.ShapeDtypeStruct((n, d), F32),
        grid=(n // tm,),
        in_specs=[
            pl.BlockSpec((tm, d), lambda i: (i, 0)),
            pl.BlockSpec((1, d), lambda i: (0, 0)),
            pl.BlockSpec((None, tm, dp), lambda i: (layer, i, 0)),
            pl.BlockSpec((None, d, d), lambda i: (layer, 0, 0)),
            pl.BlockSpec((None, dp, d), lambda i: (layer, 0, 0)),
        ],
        out_specs=pl.BlockSpec((tm, d), lambda i: (i, 0)),
        compiler_params=_cparams("parallel"),
        name="ple",
    )(h, g.reshape(1, d), p, wg, wp)


def _bias_kernel(tbl_ref, o_ref):
    tq = o_ref.shape[1]
    tk = o_ref.shape[2]
    n_rel = tbl_ref.shape[2]
    width = pl.next_power_of_2(tq + tk)
    n_idx = lax.broadcasted_iota(jnp.int32, (n_rel, width), 1)
    c_idx = lax.broadcasted_iota(jnp.int32, (n_rel, width), 0)
    m = jnp.where(n_idx < tk, n_idx, n_idx - width)
    idx = jnp.clip(ATT_LEFT - m, -(CHUNK - 1), REL_CLIP) + (CHUNK - 1)
    onehot = (c_idx == idx).astype(F32)
    tbl = jnp.broadcast_to(tbl_ref[0], (8, n_rel))
    ext = jnp.dot(tbl, onehot, preferred_element_type=F32, precision=lax.Precision.HIGHEST)[0:1]
    x = jnp.broadcast_to(ext, (tq, width))
    rolled = pltpu.roll(x, 0, 1, stride=1, stride_axis=0)[:, :tk]
    qc = lax.broadcasted_iota(jnp.int32, (tq, tk), 0) // CHUNK
    kc = lax.broadcasted_iota(jnp.int32, (tq, tk), 1) // CHUNK
    ok = (kc >= qc) & (kc <= qc + N_LEFT_CHUNKS)
    o_ref[0] = jnp.where(ok, rolled * LOG2E, NEG)


def _bias_tiles(rel_bias):
    nh, n_rel = rel_bias.shape
    tk = ATT_SUB + ATT_LEFT
    return pl.pallas_call(
        _bias_kernel,
        out_shape=jax.ShapeDtypeStruct((nh, ATT_SUB, tk), F32),
        grid=(nh,),
        in_specs=[pl.BlockSpec((1, 1, n_rel), lambda h: (h, 0, 0))],
        out_specs=pl.BlockSpec((1, ATT_SUB, tk), lambda h: (h, 0, 0)),
        compiler_params=_cparams("parallel"),
        name="rel_bias_tiles",
    )(rel_bias.reshape(nh, 1, n_rel))


def _attn_kernel(q_ref, kp_ref, kc_ref, vp_ref, vc_ref, bias_ref, qg_ref, kg_ref, o_ref):
    has_prev = pl.program_id(2) > 0
    tq = q_ref.shape[0]
    w = 2 * HEAD_DIM
    tk = bias_ref.shape[2]
    nh = w // HEAD_DIM
    col = lax.broadcasted_iota(jnp.int32, (nh * ATT_SUB, tk), 1)
    head_of_lane = lax.broadcasted_iota(jnp.int32, (1, w), 1) // HEAD_DIM
    seg = (lax.broadcasted_iota(jnp.int32, (w, w), 0) // HEAD_DIM
           == lax.broadcasted_iota(jnp.int32, (w, w), 1) // HEAD_DIM)
    seg = jnp.where(seg, 1.0, 0.0).astype(BF16)

    def rms2(x, gain):
        sq = x * x
        hi = sq.astype(BF16)
        lo = (sq - hi.astype(F32)).astype(BF16)
        ss = _dot(jnp.concatenate([hi, lo], axis=1), jnp.concatenate([seg, seg], axis=0))
        return x * lax.rsqrt(ss * (1.0 / HEAD_DIM) + RMS_EPS) * gain

    def one_pair(pi):
        ln = slice(pi * w, (pi + 1) * w)
        q = rms2(q_ref[:, ln], qg_ref[...] * (HEAD_DIM ** -0.5 * LOG2E))
        k = jnp.concatenate([rms2(kp_ref[:, ln], kg_ref[...]), rms2(kc_ref[:, ln], kg_ref[...])], axis=0).astype(BF16)
        v = jnp.concatenate([vp_ref[:, ln], vc_ref[:, ln]], axis=0).astype(BF16)
        qh = [jnp.where(head_of_lane == hh, q, 0.0).astype(BF16) for hh in range(nh)]
        bias = jnp.concatenate([bias_ref[pi * nh + hh] for hh in range(nh)], axis=0)
        starts = [i * ATT_SUB for i in range(tq // ATT_SUB)]
        ob = []
        for g0 in range(0, len(starts), ATT_GROUP):
            jobs = starts[g0:g0 + ATT_GROUP]
            sc = [_dot_nt(jnp.concatenate([x[r0:r0 + ATT_SUB] for x in qh], axis=0), k[r0:r0 + tk]) + bias
                  for r0 in jobs]
            sc = [jnp.where(has_prev | (col >= ATT_LEFT - r0), s, NEG) if r0 < ATT_LEFT else s
                  for s, r0 in zip(sc, jobs)]
            pr = [jnp.exp2(s - jnp.max(s, axis=-1, keepdims=True)) for s in sc]
            den = [jnp.sum(p, axis=-1, keepdims=True) for p in pr]
            ob += [_dot(p.astype(BF16), v[r0:r0 + tk]) / d for p, d, r0 in zip(pr, den, jobs)]
        outs = []
        for o2 in ob:
            o = o2[:ATT_SUB]
            for hh in range(1, nh):
                o = jnp.where(head_of_lane == hh, o2[hh * ATT_SUB:(hh + 1) * ATT_SUB], o)
            outs.append(o)
        o_ref[:, ln] = jnp.concatenate(outs, axis=0).astype(o_ref.dtype)

    for pi in range(q_ref.shape[1] // w):
        one_pair(pi)


def _attention(proj, bias, q_gain, k_gain, bsz, seq, d_att):
    n = bsz * seq
    tq = ATT_TQ
    nqb = seq // tq
    w = 2 * HEAD_DIM
    npair = d_att // w
    pps = ATT_PAIRS if npair % ATT_PAIRS == 0 else 1
    ngrp = npair // pps
    wb = pps * w

    def cur(col0):
        return pl.BlockSpec((tq, wb), lambda hp, b, qb: (b * nqb + qb, col0 + hp))

    def prev(col0):
        return pl.BlockSpec((tq, wb), lambda hp, b, qb: (b * nqb + jnp.maximum(qb - 1, 0), col0 + hp))

    return pl.pallas_call(
        _attn_kernel,
        out_shape=jax.ShapeDtypeStruct((n, d_att), BF16),
        grid=(ngrp, bsz, nqb),
        in_specs=[
            cur(0), prev(ngrp), cur(ngrp), prev(2 * ngrp), cur(2 * ngrp),
            pl.BlockSpec((2 * pps, ATT_SUB, ATT_SUB + ATT_LEFT), lambda hp, b, qb: (hp, 0, 0)),
            pl.BlockSpec((1, w), lambda hp, b, qb: (0, 0)),
            pl.BlockSpec((1, w), lambda hp, b, qb: (0, 0)),
        ],
        out_specs=pl.BlockSpec((tq, wb), lambda hp, b, qb: (b * nqb + qb, hp)),
        compiler_params=_cparams("parallel", "parallel", "arbitrary"),
        name="band_attention",
    )(proj, proj, proj, proj, proj, bias, jnp.tile(q_gain, 2).reshape(1, w), jnp.tile(k_gain, 2).reshape(1, w))


def _split3(x):
    hi = x.astype(BF16)
    r1 = x - hi.astype(F32)
    mid = r1.astype(BF16)
    lo = (r1 - mid.astype(F32)).astype(BF16)
    return hi, mid, lo


def _token_shift(z, carry_ref, mu):
    rows = lax.broadcasted_iota(jnp.int32, z.shape, 0)
    prev = jnp.where(rows == 0, carry_ref[...], pltpu.roll(z, 1, 0))
    carry_ref[...] = z[z.shape[0] - 1:, :]
    return z + (prev - z) * mu


def _rwkv_kernel(r_ref, k_ref, v_ref, l_ref, mur_ref, muk_ref, muv_ref, mul_ref,
                 w0_ref, a0_ref, kk_ref, ka_ref, rk_ref, lnw_ref, lnb_ref,
                 wup_ref, aup_ref, gup_ref, o_ref,
                 s_scr, cr_scr, ck_scr, cv_scr, cl_scr):
    first = pl.program_id(1) == 0
    nb, L, d = r_ref.shape

    @pl.when(first)
    def _():
        s_scr[...] = jnp.zeros_like(s_scr)
        for c in (cr_scr, ck_scr, cv_scr, cl_scr):
            c[...] = jnp.zeros_like(c)

    def shifted(ref, carry, mu_ref):
        return jnp.concatenate([_token_shift(ref[b], carry.at[b], mu_ref[...]) for b in range(nb)], axis=0)

    r = shifted(r_ref, cr_scr, mur_ref)
    k = shifted(k_ref, ck_scr, muk_ref)
    v = shifted(v_ref, cv_scr, muv_ref)
    lo = shifted(l_ref, cl_scr, mul_ref)
    nw = wup_ref.shape[0]
    na = aup_ref.shape[0]
    xw, xa, xg = lo[:, :nw], lo[:, nw:nw + na], lo[:, nw + na:]

    wpre = w0_ref[...] + _bdot(jnp.tanh(xw), wup_ref[...])
    w_log = -(jnp.maximum(-wpre, 0.0) + jnp.log(1.0 + jnp.exp(-jnp.abs(wpre)))) - 0.5
    lw = -jnp.exp(w_log)
    a = _sigmoid(a0_ref[...] + _bdot(xa, aup_ref[...]))
    g = _bdot(_sigmoid(xg), gup_ref[...])

    kkf = k * kk_ref[...]
    k2 = k * (1.0 + (a - 1.0) * ka_ref[...])

    ti = lax.broadcasted_iota(jnp.int32, (nb * L, nb * L), 0)
    si = lax.broadcasted_iota(jnp.int32, (nb * L, nb * L), 1)
    tri = jnp.where((si <= ti) & (si // L == ti // L), 1.0, 0.0).astype(BF16)
    h3 = _split3(lw)
    cum = _dot(jnp.concatenate([tri, tri, tri], axis=1), jnp.concatenate(h3, axis=0))
    e_pos = jnp.exp(cum)
    e_neg = jnp.exp(-cum)
    e_prev = jnp.exp(cum - lw)

    rk = r * k2 * rk_ref[...]
    w2 = 2 * HEAD_DIM
    chains = [(b, j) for b in range(nb) for j in range(d // w2)]
    at = lambda x, c: x[c[0] * L:(c[0] + 1) * L, c[1] * w2:(c[1] + 1) * w2]
    lane = lax.broadcasted_iota(jnp.int32, (1, w2), 1)
    in_a = lane < HEAD_DIM
    row_l = lax.broadcasted_iota(jnp.int32, (L, w2), 0)
    idx_l = lax.broadcasted_iota(jnp.int32, (L, w2), 1) % HEAD_DIM
    strict = idx_l < row_l
    incl = idx_l <= row_l
    same_head = (lax.broadcasted_iota(jnp.int32, (w2, w2), 0) // HEAD_DIM
                 == lax.broadcasted_iota(jnp.int32, (w2, w2), 1) // HEAD_DIM)
    seg = jnp.where(same_head, 1.0, 0.0).astype(BF16)

    def segsum(xs):
        x = jnp.concatenate(xs, axis=0)
        hi = x.astype(BF16)
        lo = (x - hi.astype(F32)).astype(BF16)
        tot = _dot(jnp.concatenate([hi, lo], axis=1), jnp.concatenate([seg, seg], axis=0))
        return [tot[i * L:(i + 1) * L] for i in range(len(xs))]

    def bdiag(x):
        zero = jnp.zeros_like(x)
        return jnp.concatenate([jnp.where(in_a, x, zero), jnp.where(in_a, zero, x)], axis=0)

    def lower(x, mask):
        xb = x.astype(BF16)
        return jnp.where(mask, xb, jnp.zeros_like(xb))

    kss = segsum([at(kkf, c) * at(kkf, c) for c in chains])
    steps = max(1, (L - 1).bit_length())
    ar, bkc, rb, vb, vbd = [], [], [], [], []
    for c, ks in zip(chains, kss):
        kk = at(kkf, c) / jnp.maximum(jnp.sqrt(ks), 1e-12)
        ah = -kk * at(e_prev, c)
        bh = (kk * at(a, c) * at(e_neg, c)).astype(BF16)
        kh = (at(k2, c) * at(e_neg, c)).astype(BF16)
        rh = at(r, c) * at(e_pos, c)
        ar.append(jnp.concatenate([ah, rh], axis=0).astype(BF16))
        bkc.append(jnp.concatenate([bh, kh], axis=0))
        rb.append(jnp.concatenate([bdiag(bh), bdiag(kh)], axis=0))
        vb.append(at(v, c).astype(BF16))
        vbd.append(bdiag(vb[-1]))
    ids = range(len(chains))
    s0 = [s_scr[c] for c in chains]
    gm = [_dot_nt(ar[i], rb[i]) for i in ids]
    ps = [_dot_nt(ar[i], s0[i].astype(BF16)) for i in ids]
    pw = [lower(gm[i][:L, :w2], strict) for i in ids]
    u = [ps[i][:L] + _dot(lower(gm[i][:L, w2:], strict), vbd[i]) for i in ids]
    for st in range(steps):
        u = [u[i] + _dot(pw[i], bdiag(u[i].astype(BF16))) for i in ids]
        if st + 1 < steps:
            pw = [_dot(pw[i], bdiag(pw[i])).astype(BF16) for i in ids]
    y = []
    for i, c in zip(ids, chains):
        ub = u[i].astype(BF16)
        low = jnp.concatenate([lower(gm[i][L:, :w2], incl), lower(gm[i][L:, w2:], incl)], axis=1)
        uvd = jnp.concatenate([bdiag(ub), vbd[i]], axis=0)
        y.append(ps[i][L:] + _dot(low, uvd))
        uv = jnp.concatenate([ub, vb[i]], axis=0)
        w_last = at(e_pos, c)[L - 1:, :]
        s_scr[c] = jnp.where(same_head, s0[i] + _dot_tn(uv, bkc[i]), 0.0) * w_last
    yc = [yi - m * (1.0 / HEAD_DIM) for yi, m in zip(y, segsum(y))]
    var = segsum([c * c for c in yc])
    yn = [c * lax.rsqrt(vr * (1.0 / HEAD_DIM) + GN_EPS) for c, vr in zip(yc, var)]
    bs = segsum([at(rk, c) for c in chains])
    npair = d // w2
    grid2 = lambda xs: jnp.concatenate([jnp.concatenate(xs[b * npair:(b + 1) * npair], axis=1)
                                        for b in range(nb)], axis=0)
    out = (grid2(yn) * lnw_ref[...] + lnb_ref[...] + grid2(bs) * v) * g
    o_ref[...] = out.reshape(nb, L, d).astype(o_ref.dtype)


def _rwkv(proj, col0, bsz, seq, d, mu, w0, w_up, a0, a_up, g_up, k_k, k_a, r_k, lnx_w, lnx_b):
    L = RWKV_L
    nb = RWKV_NB if bsz % RWKV_NB == 0 else 1
    nt = seq // L
    nl = mu.shape[0] - 3 * d
    row = lambda x: x.reshape(1, -1)
    cb = col0 // d
    lb = (col0 + 3 * d) // nl
    proj3 = proj.reshape(bsz, seq, -1)

    def zspec(width, blk):
        return pl.BlockSpec((nb, L, width), lambda b, t: (b, t, blk))

    def pspec(shape):
        return pl.BlockSpec(shape, lambda b, t: (0,) * len(shape))

    npair = d // (2 * HEAD_DIM)
    out = pl.pallas_call(
        _rwkv_kernel,
        out_shape=jax.ShapeDtypeStruct((bsz, seq, d), BF16),
        grid=(bsz // nb, nt),
        in_specs=[
            zspec(d, cb), zspec(d, cb + 1), zspec(d, cb + 2), zspec(nl, lb),
            pspec((1, d)), pspec((1, d)), pspec((1, d)), pspec((1, nl)),
            pspec((1, d)), pspec((1, d)), pspec((1, d)), pspec((1, d)), pspec((1, d)),
            pspec((1, d)), pspec((1, d)),
            pspec(w_up.shape), pspec(a_up.shape), pspec(g_up.shape),
        ],
        out_specs=pl.BlockSpec((nb, L, d), lambda b, t: (b, t, 0)),
        scratch_shapes=[
            pltpu.VMEM((nb, npair, 2 * HEAD_DIM, 2 * HEAD_DIM), F32),
            pltpu.VMEM((nb, 1, d), F32), pltpu.VMEM((nb, 1, d), F32), pltpu.VMEM((nb, 1, d), F32),
            pltpu.VMEM((nb, 1, nl), F32),
        ],
        compiler_params=_cparams("parallel", "arbitrary"),
        name="rwkv7",
    )(proj3, proj3, proj3, proj3,
      row(mu[:d]), row(mu[d:2 * d]), row(mu[2 * d:3 * d]), row(mu[3 * d:]),
      row(w0), row(a0), row(k_k), row(k_a), row(r_k), row(lnx_w), row(lnx_b),
      w_up.astype(BF16), a_up.astype(BF16), g_up.astype(BF16))
    return out.reshape(bsz * seq, d)


SSM_GB = 8
SSM_NB = 2


def _cmul(ar, ai, br, bi):
    return ar * br - ai * bi, ar * bi + ai * br


def _s5_abar(lr, li, ldt):
    dt = jnp.exp(ldt)
    mag = jnp.exp(lr * dt)
    return mag * jnp.cos(li * dt), mag * jnp.sin(li * dt)


def _s5_prep_kernel(lrr_ref, lir_ref, ldtr_ref, lrc_ref, lic_ref, ldtc_ref,
                    bre_ref, bim_ref, cre_ref, cim_ref, k_ref, p_ref, q_ref):
    L = SSM_L
    w = k_ref.shape[2]
    lr, li = lrr_ref[0], lir_ref[0]
    a_re, a_im = _s5_abar(lr, li, ldtr_ref[0])
    den = lr * lr + li * li
    z_re = ((a_re - 1.0) * lr + a_im * li) / den
    z_im = (a_im * lr - (a_re - 1.0) * li) / den
    bb_re, bb_im = _cmul(z_re, z_im, bre_ref[0], bim_ref[0])
    cre, cim = cre_ref[0], cim_ref[0]
    ccat = jnp.concatenate([cre, -cim], axis=0)
    ac_re, ac_im = _s5_abar(lrc_ref[0], lic_ref[0], ldtc_ref[0])
    pr, pi = jnp.ones_like(a_re), jnp.zeros_like(a_im)
    qr, qi = ac_re, ac_im
    ptaus = []
    for tau in range(L):
        rb_re, rb_im = _cmul(pr, pi, bb_re, bb_im)
        ptau = jnp.concatenate([rb_re, rb_im], axis=1)
        ptaus.append(ptau)
        p_ref[0, (L - 1 - tau) * w:(L - tau) * w, :] = ptau.astype(p_ref.dtype)
        q_ref[0, :, tau * w:(tau + 1) * w] = jnp.concatenate(
            [cre * qr - cim * qi, -(cre * qi + cim * qr)], axis=0).astype(q_ref.dtype)
        pr, pi = _cmul(pr, pi, a_re, a_im)
        qr, qi = _cmul(qr, qi, ac_re, ac_im)
    pall = jnp.concatenate(ptaus[::-1], axis=0)
    p_hi = pall.astype(BF16)
    p_lo = (pall - p_hi.astype(F32)).astype(BF16)
    c_hi = ccat.astype(BF16)
    c_lo = (ccat - c_hi.astype(F32)).astype(BF16)
    k_ref[0, :L * w, :] = (_dot(p_hi, c_hi) + _dot(p_lo, c_hi) + _dot(p_hi, c_lo)).astype(k_ref.dtype)
    k_ref[0, L * w:, :] = jnp.zeros((w, w), k_ref.dtype)


def _s5_prep(lam_re, lam_im, log_dt, b_re, b_im, c_re, c_im):
    G, P = lam_re.shape
    gs = SSM_GROUP
    gb = SSM_GB
    J = G // gb
    L = SSM_L
    eye = jnp.eye(gb, dtype=F32)
    ldt = jnp.repeat(log_dt, P)

    def bdiag_b(b):
        bt = jnp.swapaxes(b, 1, 2).reshape(J, gb, gs, P)
        return (bt[:, :, :, None, :] * eye[None, :, None, :, None]).reshape(J, gb * gs, gb * P)

    def bdiag_c(c):
        ct = jnp.swapaxes(c, 1, 2).reshape(J, gb, P, gs)
        return (ct[:, :, :, None, :] * eye[None, :, None, :, None]).reshape(J, gb * P, gb * gs)

    row = lambda x: x.reshape(J, 1, gb * P)
    col = lambda x: x.reshape(J, gb * P, 1)

    def spec(*shape):
        return pl.BlockSpec((1,) + shape, lambda j: (j,) + (0,) * len(shape))

    w, sw = gb * gs, gb * P
    return pl.pallas_call(
        _s5_prep_kernel,
        out_shape=[jax.ShapeDtypeStruct((J, (L + 1) * w, w), BF16),
                   jax.ShapeDtypeStruct((J, L * w, 2 * sw), BF16),
                   jax.ShapeDtypeStruct((J, 2 * sw, L * w), BF16)],
        grid=(J,),
        in_specs=[spec(1, sw), spec(1, sw), spec(1, sw), spec(sw, 1), spec(sw, 1), spec(sw, 1),
                  spec(w, sw), spec(w, sw), spec(sw, w), spec(sw, w)],
        out_specs=[spec((L + 1) * w, w), spec(L * w, 2 * sw), spec(2 * sw, L * w)],
        compiler_params=_cparams("parallel"),
        name="s5_prep",
    )(row(lam_re), row(lam_im), row(ldt), col(lam_re), col(lam_im), col(ldt),
      bdiag_b(b_re), bdiag_b(b_im), bdiag_c(c_re), bdiag_c(c_im))


def _s5_kernel(u_ref, k_ref, p_ref, q_ref, lr_ref, li_ref, ldt_ref, d_ref, o_ref,
               gre_scr, gim_scr, hre_scr, him_scr, y_scr, *, nb):
    L = SSM_L
    nc = u_ref.shape[0] // L
    ncs = nc // nb
    sw = lr_ref.shape[2]
    w = u_ref.shape[1]
    us = [u_ref[pl.ds(s, nc, stride=L), :].astype(BF16) for s in range(L)]
    ucat = jnp.concatenate(us, axis=1)
    gall = _dot(ucat, p_ref[0])
    gre_scr[...] = gall[:, :sw]
    gim_scr[...] = gall[:, sw:]
    a_re, a_im = _s5_abar(lr_ref[0], li_ref[0], ldt_ref[0])
    al_re, al_im = a_re, a_im
    for _ in range(L.bit_length() - 1):
        al_re, al_im = _cmul(al_re, al_im, al_re, al_im)

    def body(c, carry):
        new = []
        for b, (hre, him) in enumerate(carry):
            row = b * ncs + c
            hre_scr[pl.ds(row, 1), :] = hre
            him_scr[pl.ds(row, 1), :] = him
            gr = gre_scr[pl.ds(row, 1), :]
            gi = gim_scr[pl.ds(row, 1), :]
            new.append((al_re * hre - al_im * him + gr, al_re * him + al_im * hre + gi))
        return tuple(new)

    zero = jnp.zeros((1, sw), F32)
    lax.fori_loop(0, ncs, body, tuple((zero, zero) for _ in range(nb)))
    hp = jnp.concatenate([hre_scr[...], him_scr[...]], axis=1).astype(BF16)
    ycarry = _dot(hp, q_ref[0])
    kall = k_ref[0]
    for t in range(0, L, 2):
        wt = jnp.concatenate([kall[(L - 1 - t) * w:, :], kall[(L - 2 - t) * w:L * w, :]], axis=1)
        yt = ycarry[:, t * w:(t + 2) * w] + _dot(ucat[:, :(t + 2) * w], wt)
        y_scr[pl.ds(t, nc, stride=L), :] = yt[:, :w]
        y_scr[pl.ds(t + 1, nc, stride=L), :] = yt[:, w:]
    y = y_scr[...] + d_ref[0] * u_ref[...]
    o_ref[...] = (0.5 * y * (1.0 + jnp.tanh(math.sqrt(2.0 / math.pi) * (y + 0.044715 * y * y * y)))).astype(o_ref.dtype)


def _s5(u, bsz, seq, mats, lam_re, lam_im, log_dt, d_skip):
    kb, pb, qb = mats
    G, P = lam_re.shape
    gb = SSM_GB
    J = G // gb
    L = SSM_L
    w = gb * SSM_GROUP
    sw = gb * P
    nb = SSM_NB if bsz % SSM_NB == 0 else 1
    rows = nb * seq
    nc = rows // L
    row = lambda x: x.reshape(J, 1, sw)

    def jspec(*shape):
        return pl.BlockSpec((1,) + shape, lambda j, b: (j,) + (0,) * len(shape))

    return pl.pallas_call(
        functools.partial(_s5_kernel, nb=nb),
        out_shape=jax.ShapeDtypeStruct((bsz * seq, G * SSM_GROUP), BF16),
        grid=(J, bsz // nb),
        in_specs=[
            pl.BlockSpec((rows, w), lambda j, b: (b, j)),
            jspec((L + 1) * w, w), jspec(L * w, 2 * sw), jspec(2 * sw, L * w),
            jspec(1, sw), jspec(1, sw), jspec(1, sw), jspec(1, w),
        ],
        out_specs=pl.BlockSpec((rows, w), lambda j, b: (b, j)),
        scratch_shapes=[pltpu.VMEM((nc, sw), F32)] * 4 + [pltpu.VMEM((rows, w), F32)],
        compiler_params=_cparams("parallel", "arbitrary"),
        name="s5_ssm",
    )(u, kb, pb, qb, row(lam_re), row(lam_im), row(jnp.repeat(log_dt, P)), d_skip.reshape(J, 1, w))


def kernel(x, p, ffn1_norm, ffn1_w_gate, ffn1_w_up, ffn1_w_down, mix_norm, ffn2_norm, ffn2_w_gate, ffn2_w_up, ffn2_w_down, ple_norm, ple_w_gate, ple_w_proj, ab_w_in, att_q_gain, att_k_gain, att_rel_bias, rwkv_mu, rwkv_w0, rwkv_w_up, rwkv_a0, rwkv_a_up, rwkv_g_up, rwkv_k_k, rwkv_k_a, rwkv_r_k, rwkv_lnx_w, rwkv_lnx_b, ab_w_out, ssm_w_in, ssm_lambda_re, ssm_lambda_im, ssm_log_dt, ssm_b_re, ssm_b_im, ssm_c_re, ssm_c_im, ssm_d, ssm_w_out):
    bsz, seq, d = x.shape
    depth = p.shape[0]
    n = bsz * seq
    bf = lambda w: w.astype(BF16)
    h = x.reshape(n, d)
    f1 = (ffn1_w_gate, ffn1_w_up, ffn1_w_down)
    f2 = (ffn2_w_gate, ffn2_w_up, ffn2_w_down)
    wcur = tuple(_cast_layer(w, 0) for w in f1)
    wpl = [bf(w) for w in (ple_w_gate, ple_w_proj)]
    pe = p.reshape(depth, n, -1)
    for i in range(depth):
        j = i // 2
        h, wcur = _ffn(h, ffn1_norm[i], wcur, 0, f2, i)
        if i % 2 == 0:
            d_att = att_rel_bias.shape[1] * HEAD_DIM
            d_rw = rwkv_w0.shape[1]
            n_in = ab_w_in.shape[2]
            tn = 1280 if n_in % 1280 == 0 else 128
            proj = _norm_matmul(h, mix_norm[i], bf(ab_w_in[j]), 2 * ROW_TILE, tn)
            bias = _bias_tiles(att_rel_bias[j])
            att = _attention(proj, bias, att_q_gain[j], att_k_gain[j], bsz, seq, d_att)
            rw = _rwkv(proj, 3 * d_att, bsz, seq, d_rw, rwkv_mu[j], rwkv_w0[j], rwkv_w_up[j],
                       rwkv_a0[j], rwkv_a_up[j], rwkv_g_up[j], rwkv_k_k[j], rwkv_k_a[j],
                       rwkv_r_k[j].reshape(-1), rwkv_lnx_w[j], rwkv_lnx_b[j])
            h = _out2(h, att, rw, bf(ab_w_out[j]))
        else:
            d_ssm = ssm_w_in.shape[2]
            u = _norm_matmul(h, mix_norm[i], bf(ssm_w_in[j]), ROW_TILE, d_ssm)
            mats = _s5_prep(ssm_lambda_re[j], ssm_lambda_im[j], ssm_log_dt[j], ssm_b_re[j],
                            ssm_b_im[j], ssm_c_re[j], ssm_c_im[j])
            y = _s5(u, bsz, seq, mats, ssm_lambda_re[j], ssm_lambda_im[j], ssm_log_dt[j], ssm_d[j])
            h = _glu_out(h, y, bf(ssm_w_out[j]))
        last = i + 1 == depth
        h, wcur = _ffn(h, ffn2_norm[i], wcur, 0, None if last else f1, i + 1)
        h = _ple(h, ple_norm[i], pe, wpl[0], wpl[1], i)
    return h.reshape(bsz, seq, d)
```
